```python
import math
import jax, jax.numpy as jnp
from jax import lax
import numpy as np

D_MODEL = 4096
BATCH = 1
SEQ = 16384
DEPTH = 4

N_MIXERS = 2
LRU_WIDTH = D_MODEL
LRU_HEADS = 16
LRU_BLOCK = LRU_WIDTH // LRU_HEADS
CONV_WIDTH = 4
LRU_C = 8.0
HEAD_DIM = 128
N_Q_HEADS = D_MODEL // HEAD_DIM
N_KV_HEADS = 8
Q_PER_KV = N_Q_HEADS // N_KV_HEADS
WINDOW = 128
ATTN_BLOCK = WINDOW
N_EXPERTS = 64
TOP_K = 8
N_GROUPS = 8
TOPK_GROUPS = 4
EXPERT_HIDDEN = 192
SHARED_HIDDEN = EXPERT_HIDDEN
ROUTED_SCALE = 2.5
EXPERT_BLOCK = 128
DN_ALPHA = (2.0 * DEPTH) ** 0.25
DN_BETA = (8.0 * DEPTH) ** -0.25
LN_EPS = 1e-5
N_LRU_LAYERS = (DEPTH + 1) // 2
N_ATTN_LAYERS = DEPTH // 2

kernel_name = "hybrid_rglru_swa_sink_alibi_moe_deepnorm"


def layer_norm(x, g, b):
    xf = x.astype(jnp.float32)
    mu = jnp.mean(xf, axis=-1, keepdims=True)
    xc = xf - mu
    var = jnp.mean(xc * xc, axis=-1, keepdims=True)
    return (xc * lax.rsqrt(var + LN_EPS) * g.astype(jnp.float32) + b.astype(jnp.float32)).astype(x.dtype)


def rglru_block(x, w_in, b_in, conv_w, conv_b, gx_w, gx_b, ga_w, ga_b, a_param, w_out):
    B, T, _ = x.shape
    u = x @ w_in + b_in
    xr, yg = u[..., :LRU_WIDTH], u[..., LRU_WIDTH:]
    gate_branch = jax.nn.gelu(yg)
    xc = lax.conv_general_dilated(xr, conv_w[:, None, :], window_strides=(1,),
                                  padding=[(CONV_WIDTH - 1, 0)],
                                  dimension_numbers=('NWC', 'WIO', 'NWC'),
                                  feature_group_count=LRU_WIDTH) + conv_b
    xh = xc.reshape(B, T, LRU_HEADS, LRU_BLOCK)
    gate_x = jax.nn.sigmoid((jnp.einsum('btnd,nde->btne', xh, gx_w) + gx_b).astype(jnp.float32)).reshape(B, T, LRU_WIDTH)
    gate_a = jax.nn.sigmoid((jnp.einsum('btnd,nde->btne', xh, ga_w) + ga_b).astype(jnp.float32)).reshape(B, T, LRU_WIDTH)
    log_a = -LRU_C * gate_a * jax.nn.softplus(-a_param.astype(jnp.float32))
    a = jnp.exp(log_a)
    mult = jnp.sqrt(-jnp.expm1(2.0 * log_a))
    b_in_t = xc.astype(jnp.float32) * gate_x * mult

    def step(h, ab):
        a_t, b_t = ab
        h = a_t * h + b_t
        return h, h

    _, hs = lax.scan(step, jnp.zeros((B, LRU_WIDTH), jnp.float32),
                     (jnp.swapaxes(a, 0, 1), jnp.swapaxes(b_in_t, 0, 1)))
    h = jnp.swapaxes(hs, 0, 1)
    y = (h * gate_branch.astype(jnp.float32)).astype(x.dtype)
    return y @ w_out


def swa_sink_alibi_attention(x, w_qkv, sinks, w_o):
    B, T, _ = x.shape
    nb = T // ATTN_BLOCK
    qkv = x @ w_qkv
    nq, nkv = N_Q_HEADS * HEAD_DIM, N_KV_HEADS * HEAD_DIM
    q = qkv[..., :nq].reshape(B, nb, ATTN_BLOCK, N_KV_HEADS, Q_PER_KV, HEAD_DIM)
    k = qkv[..., nq:nq + nkv].reshape(B, nb, ATTN_BLOCK, N_KV_HEADS, HEAD_DIM)
    v = qkv[..., nq + nkv:].reshape(B, nb, ATTN_BLOCK, N_KV_HEADS, HEAD_DIM)

    def band(t):
        prev = jnp.pad(t, ((0, 0), (1, 0), (0, 0), (0, 0), (0, 0)))[:, :-1]
        return jnp.concatenate([prev, t], axis=2)

    kb, vb = band(k), band(v)
    scores = jnp.einsum('bnqkgd,bnskd->bnkgqs', q, kb,
                        preferred_element_type=jnp.float32) * (HEAD_DIM ** -0.5)
    i = jnp.arange(ATTN_BLOCK)[:, None]
    j = jnp.arange(2 * ATTN_BLOCK)[None, :]
    dist = ATTN_BLOCK + i - j
    in_band = (dist >= 0) & (dist < WINDOW)
    first_pad = (jnp.arange(nb)[:, None, None] == 0) & (j < ATTN_BLOCK)[None]
    valid = in_band[None] & ~first_pad
    slopes = jnp.exp2(-8.0 * jnp.arange(1, N_Q_HEADS + 1, dtype=jnp.float32) / N_Q_HEADS)
    slopes = slopes.reshape(N_KV_HEADS, Q_PER_KV)
    alibi = -slopes[:, :, None, None] * dist.astype(jnp.float32)
    logits = jnp.where(valid[None, :, None, None], scores + alibi[None, None], -jnp.inf)
    sink = sinks.astype(jnp.float32).reshape(N_KV_HEADS, Q_PER_KV)[None, None, :, :, None, None]
    m = jnp.maximum(jnp.max(logits, axis=-1, keepdims=True), sink)
    p = jnp.exp(logits - m)
    probs = p / (jnp.sum(p, axis=-1, keepdims=True) + jnp.exp(sink - m))
    o = jnp.einsum('bnkgqs,bnskd->bnqkgd', probs.astype(vb.dtype), vb)
    return o.reshape(B, T, nq) @ w_o


def moe_ffn(x2d, l, router_w, router_b, w_up, w_down, shared_up, shared_down):
    N, D = x2d.shape
    scores = jax.nn.sigmoid((x2d @ router_w[l]).astype(jnp.float32))
    sel = scores + router_b[l].astype(jnp.float32)
    grp = sel.reshape(N, N_GROUPS, N_EXPERTS // N_GROUPS)
    grp_score = jnp.sum(lax.top_k(grp, 2)[0], axis=-1)
    _, top_g = lax.top_k(grp_score, TOPK_GROUPS)
    gmask = jnp.any(top_g[..., None] == jnp.arange(N_GROUPS), axis=1)
    emask = jnp.repeat(gmask, N_EXPERTS // N_GROUPS, axis=1)
    _, idx = lax.top_k(jnp.where(emask, sel, -jnp.inf), TOP_K)
    gate = jnp.take_along_axis(scores, idx, axis=1)
    gate = gate / jnp.sum(gate, axis=-1, keepdims=True) * ROUTED_SCALE

    A = N * TOP_K
    NB = -(-A // EXPERT_BLOCK) + N_EXPERTS
    flat_e = idx.reshape(-1)
    flat_tok = jnp.arange(A, dtype=jnp.int32) // TOP_K
    flat_g = gate.reshape(-1)
    order = jnp.argsort(flat_e)
    se, stok, sg = flat_e[order], flat_tok[order], flat_g[order]
    counts = jnp.bincount(flat_e, length=N_EXPERTS)
    padded = (counts + EXPERT_BLOCK - 1) // EXPERT_BLOCK * EXPERT_BLOCK
    start = jnp.cumsum(counts) - counts
    pend = jnp.cumsum(padded)
    pstart = pend - padded
    dest = pstart[se] + jnp.arange(A) - start[se]
    row_tok = jnp.zeros((NB * EXPERT_BLOCK,), jnp.int32).at[dest].set(stok)
    row_g = jnp.zeros((NB * EXPERT_BLOCK,), jnp.float32).at[dest].set(sg)
    block_e = jnp.minimum(jnp.searchsorted(pend, jnp.arange(NB) * EXPERT_BLOCK, side='right'),
                          N_EXPERTS - 1).astype(jnp.int32)

    def body(acc, blk):
        tok, g, e = blk
        hu = x2d[tok] @ w_up[l, e]
        y = (jax.nn.silu(hu[:, :EXPERT_HIDDEN]) * hu[:, EXPERT_HIDDEN:]) @ w_down[l, e]
        return acc.at[tok].add(y.astype(jnp.float32) * g[:, None]), None

    routed, _ = lax.scan(body, jnp.zeros((N, D), jnp.float32),
                         (row_tok.reshape(NB, EXPERT_BLOCK), row_g.reshape(NB, EXPERT_BLOCK), block_e))
    su = x2d @ shared_up[l]
    shared = (jax.nn.silu(su[:, :SHARED_HIDDEN]) * su[:, SHARED_HIDDEN:]) @ shared_down[l]
    return (routed + shared.astype(jnp.float32)).astype(x2d.dtype)


def setup_inputs(seed: int = 0) -> dict:
    key = jax.random.key(seed)
    ks = jax.random.split(key, 24)
    f32 = jnp.float32
    W, D, E, H, S = LRU_WIDTH, D_MODEL, N_EXPERTS, EXPERT_HIDDEN, SHARED_HIDDEN
    nrm = lambda k, shp, s: jax.random.normal(k, shp, f32) * s
    a0 = jax.random.uniform(ks[9], (N_LRU_LAYERS, W), f32, 0.9, 0.999)
    s0 = a0 ** (1.0 / LRU_C)
    qkv_cols = (N_Q_HEADS + 2 * N_KV_HEADS) * HEAD_DIM
    return {
        "x": nrm(ks[0], (BATCH, SEQ, D), 1.0),
        "lru_w_in": nrm(ks[1], (N_LRU_LAYERS, D, 2 * W), D ** -0.5),
        "lru_b_in": nrm(ks[2], (N_LRU_LAYERS, 2 * W), 0.01),
        "lru_conv_w": nrm(ks[3], (N_LRU_LAYERS, CONV_WIDTH, W), CONV_WIDTH ** -0.5),
        "lru_conv_b": nrm(ks[4], (N_LRU_LAYERS, W), 0.01),
        "lru_gx_w": nrm(ks[5], (N_LRU_LAYERS, LRU_HEADS, LRU_BLOCK, LRU_BLOCK), LRU_BLOCK ** -0.5),
        "lru_gx_b": nrm(ks[6], (N_LRU_LAYERS, LRU_HEADS, LRU_BLOCK), 0.01),
        "lru_ga_w": nrm(ks[7], (N_LRU_LAYERS, LRU_HEADS, LRU_BLOCK, LRU_BLOCK), LRU_BLOCK ** -0.5),
        "lru_ga_b": nrm(ks[8], (N_LRU_LAYERS, LRU_HEADS, LRU_BLOCK), 0.01),
        "lru_a_param": jnp.log(s0) - jnp.log1p(-s0),
        "lru_w_out": nrm(ks[10], (N_LRU_LAYERS, W, D), W ** -0.5 * DN_BETA),
        "attn_w_qkv": nrm(ks[11], (N_ATTN_LAYERS, D, qkv_cols), D ** -0.5),
        "attn_sinks": nrm(ks[12], (N_ATTN_LAYERS, N_Q_HEADS), 1.0),
        "attn_w_o": nrm(ks[13], (N_ATTN_LAYERS, N_Q_HEADS * HEAD_DIM, D), (N_Q_HEADS * HEAD_DIM) ** -0.5 * DN_BETA),
        "ln_g": 1.0 + nrm(ks[14], (DEPTH, 2, D), 0.02),
        "ln_b": nrm(ks[15], (DEPTH, 2, D), 0.02),
        "moe_router_w": nrm(ks[16], (DEPTH, D, E), D ** -0.5),
        "moe_router_b": nrm(ks[17], (DEPTH, E), 0.01),
        "moe_w_up": nrm(ks[18], (DEPTH, E, D, 2 * H), D ** -0.5),
        "moe_w_down": nrm(ks[19], (DEPTH, E, H, D), H ** -0.5 * DN_BETA),
        "moe_shared_up": nrm(ks[20], (DEPTH, D, 2 * S), D ** -0.5),
        "moe_shared_down": nrm(ks[21], (DEPTH, S, D), S ** -0.5 * DN_BETA),
    }


def reference(x, lru_w_in, lru_b_in, lru_conv_w, lru_conv_b, lru_gx_w, lru_gx_b, lru_ga_w, lru_ga_b,
              lru_a_param, lru_w_out, attn_w_qkv, attn_sinks, attn_w_o, ln_g, ln_b,
              moe_router_w, moe_router_b, moe_w_up, moe_w_down, moe_shared_up, moe_shared_down):
    B, T, D = x.shape
    h = x
    for layer in range(DEPTH):
        j = layer // N_MIXERS
        if layer % N_MIXERS == 0:
            mix = rglru_block(h, lru_w_in[j], lru_b_in[j], lru_conv_w[j], lru_conv_b[j],
                              lru_gx_w[j], lru_gx_b[j], lru_ga_w[j], lru_ga_b[j],
                              lru_a_param[j], lru_w_out[j])
        else:
            mix = swa_sink_alibi_attention(h, attn_w_qkv[j], attn_sinks[j], attn_w_o[j])
        h = layer_norm(DN_ALPHA * h + mix, ln_g[layer, 0], ln_b[layer, 0])
        ffn = moe_ffn(h.reshape(B * T, D), layer, moe_router_w, moe_router_b, moe_w_up,
                      moe_w_down, moe_shared_up, moe_shared_down).reshape(B, T, D)
        h = layer_norm(DN_ALPHA * h + ffn, ln_g[layer, 1], ln_b[layer, 1])
    return h
```

```python
import functools

import jax
import jax.numpy as jnp
from jax import lax
from jax.experimental import pallas as pl
from jax.experimental.pallas import tpu as pltpu

HEAD_DIM = 128
WINDOW = 128
CONV_WIDTH = 4
LRU_C = 8.0
TOP_K = 8
N_GROUPS = 8
TOPK_GROUPS = 4
ROUTED_SCALE = 2.5
LN_EPS = 1e-5

V7X_VMEM_BYTES = 64 * 1024 * 1024
VMEM_LIMIT = V7X_VMEM_BYTES - 8 * 1024 * 1024
EXPERT_ROWS = 256

f32 = jnp.float32
bf16 = jnp.bfloat16


def _cparams(*sem):
    return pltpu.CompilerParams(dimension_semantics=sem, vmem_limit_bytes=VMEM_LIMIT)


def _pick(n, pref):
    t = min(pref, n)
    while n % t:
        t //= 2
    return t


def _mm_kernel(x_ref, w_ref, o_ref):
    o_ref[...] = jnp.dot(x_ref[...], w_ref[...], preferred_element_type=f32).astype(o_ref.dtype)


def _matmul(x, w, out_dtype):
    M, K = x.shape
    N = w.shape[1]
    tm, tn = _pick(M, 1024), _pick(N, 512)
    return pl.pallas_call(
        _mm_kernel,
        grid=(M // tm, N // tn),
        in_specs=[pl.BlockSpec((tm, K), lambda i, j: (i, 0)),
                  pl.BlockSpec((K, tn), lambda i, j: (0, j))],
        out_specs=pl.BlockSpec((tm, tn), lambda i, j: (i, j)),
        out_shape=jax.ShapeDtypeStruct((M, N), out_dtype),
        compiler_params=_cparams("parallel", "arbitrary"),
        name="matmul",
    )(x, w)


def _mm_res_kernel(x_ref, w_ref, r_ref, o_ref, *, alpha):
    o_ref[...] = alpha * r_ref[...] + jnp.dot(x_ref[...], w_ref[...], preferred_element_type=f32)


def _matmul_residual(x, w, res, alpha):
    M, K = x.shape
    N = w.shape[1]
    tm, tn = _pick(M, 1024), _pick(N, 512)
    return pl.pallas_call(
        functools.partial(_mm_res_kernel, alpha=alpha),
        grid=(M // tm, N // tn),
        in_specs=[pl.BlockSpec((tm, K), lambda i, j: (i, 0)),
                  pl.BlockSpec((K, tn), lambda i, j: (0, j)),
                  pl.BlockSpec((tm, tn), lambda i, j: (i, j))],
        out_specs=pl.BlockSpec((tm, tn), lambda i, j: (i, j)),
        out_shape=jax.ShapeDtypeStruct((M, N), f32),
        compiler_params=_cparams("parallel", "arbitrary"),
        name="matmul_residual",
    )(x, w, res)


def _gelu_tanh(x):
    return 0.5 * x * (1.0 + jnp.tanh(0.7978845608028654 * (x + 0.044715 * (x * x * x))))


def _lru_in_kernel(x_ref, wx_ref, wy_ref, bx_ref, by_ref, xr_ref, gb_ref):
    x = x_ref[...]
    xr_ref[...] = jnp.dot(x, wx_ref[...], preferred_element_type=f32) + bx_ref[...]
    gb_ref[...] = _gelu_tanh(jnp.dot(x, wy_ref[...], preferred_element_type=f32) + by_ref[...])


def _lru_in(x, wx, wy, bx, by):
    M, K = x.shape
    W = wx.shape[1]
    tm, tn = _pick(M, 1024), _pick(W, 256)
    wspec = pl.BlockSpec((K, tn), lambda i, j: (0, j))
    bspec = pl.BlockSpec((1, tn), lambda i, j: (0, j))
    ospec = pl.BlockSpec((tm, tn), lambda i, j: (i, j))
    return pl.pallas_call(
        _lru_in_kernel,
        grid=(M // tm, W // tn),
        in_specs=[pl.BlockSpec((tm, K), lambda i, j: (i, 0)), wspec, wspec, bspec, bspec],
        out_specs=[ospec, ospec],
        out_shape=[jax.ShapeDtypeStruct((M, W), f32)] * 2,
        compiler_params=_cparams("parallel", "arbitrary"),
        name="lru_in",
    )(x, wx, wy, bx, by)


def _layer_norm_rows(z, g, b):
    mu = jnp.mean(z, axis=-1, keepdims=True)
    zc = z - mu
    var = jnp.mean(zc * zc, axis=-1, keepdims=True)
    return zc * lax.rsqrt(var + LN_EPS) * g + b


def _store_hidden(h, h_ref, hb_ref, hp_ref):
    h_ref[...] = h
    hb = h.astype(bf16)
    hb_ref[...] = hb
    half = h.shape[1] // 2
    bits = lax.bitcast_convert_type(hb.astype(f32), jnp.uint32)
    hp_ref[...] = (bits[:, :half] >> 16) | bits[:, half:]


def _ln_kernel(z_ref, g_ref, b_ref, h_ref, hb_ref, hp_ref):
    _store_hidden(_layer_norm_rows(z_ref[...], g_ref[...], b_ref[...]), h_ref, hb_ref, hp_ref)


def _hidden_out(M, D, tm):
    specs = [pl.BlockSpec((tm, D), lambda i: (i, 0)),
             pl.BlockSpec((tm, D), lambda i: (i, 0)),
             pl.BlockSpec((tm, D // 2), lambda i: (i, 0))]
    shapes = [jax.ShapeDtypeStruct((M, D), f32),
              jax.ShapeDtypeStruct((M, D), bf16),
              jax.ShapeDtypeStruct((M, D // 2), jnp.uint32)]
    return specs, shapes


def _layer_norm(z, g, b):
    M, D = z.shape
    tm = _pick(M, 256)
    specs, shapes = _hidden_out(M, D, tm)
    vspec = pl.BlockSpec((1, D), lambda i: (0, 0))
    return pl.pallas_call(
        _ln_kernel,
        grid=(M // tm,),
        in_specs=[pl.BlockSpec((tm, D), lambda i: (i, 0)), vspec, vspec],
        out_specs=specs,
        out_shape=shapes,
        compiler_params=_cparams("parallel"),
        name="layer_norm",
    )(z, g.reshape(1, D), b.reshape(1, D))


def _softplus(x):
    return jnp.maximum(x, 0.0) + jnp.log1p(jnp.exp(-jnp.abs(x)))


def _lru_core_kernel(xr_ref, gb_ref, cw_ref, cb_ref, gxw_ref, gxb_ref, gaw_ref, gab_ref, ap_ref,
                     y_ref, tail_sc, h_sc, a_sc, b_sc, hs_sc, *, heads, blk):
    tb = xr_ref.shape[0]

    @pl.when(pl.program_id(1) == 0)
    def _():
        tail_sc[...] = jnp.zeros_like(tail_sc)
        h_sc[...] = jnp.zeros_like(h_sc)

    xr = xr_ref[...]
    ext = jnp.concatenate([tail_sc[...], xr], axis=0)
    tail_sc[...] = xr[tb - 8:, :]
    cw = cw_ref[...]
    xc = cb_ref[...]
    for j in range(CONV_WIDTH):
        off = 8 - (CONV_WIDTH - 1) + j
        xc = xc + cw[j:j + 1, :] * ext[off:off + tb, :]
    xcb = xc.astype(bf16)

    for hd in range(heads):
        sl = slice(hd * blk, (hd + 1) * blk)
        xh = xcb[:, sl]
        gate_x = jax.nn.sigmoid(jnp.dot(xh, gxw_ref[hd], preferred_element_type=f32) + gxb_ref[:, sl])
        gate_a = jax.nn.sigmoid(jnp.dot(xh, gaw_ref[hd], preferred_element_type=f32) + gab_ref[:, sl])
        log_a = (-LRU_C * _softplus(-ap_ref[:, sl])) * gate_a
        a_sc[:, sl] = jnp.exp(log_a)
        th = jnp.tanh(log_a)
        b_sc[:, sl] = xc[:, sl] * gate_x * jnp.sqrt(-2.0 * th / (1.0 - th))

    def body(i, h):
        base = pl.multiple_of(i * 8, 8)
        a8 = a_sc[pl.ds(base, 8), :]
        b8 = b_sc[pl.ds(base, 8), :]
        rows = []
        for r in range(8):
            h = a8[r:r + 1, :] * h + b8[r:r + 1, :]
            rows.append(h)
        hs_sc[pl.ds(base, 8), :] = jnp.concatenate(rows, axis=0)
        return h

    h_sc[0:1, :] = lax.fori_loop(0, tb // 8, body, h_sc[0:1, :])
    y_ref[...] = (hs_sc[...] * gb_ref[...]).astype(y_ref.dtype)


def _lru_core(xr, gb, conv_w, conv_b, gx_w, gx_b, ga_w, ga_b, a_param):
    T, W = xr.shape
    nh, blk = gx_w.shape[0], gx_w.shape[1]
    wb = _pick(W, 1024)
    heads = wb // blk
    tb = _pick(T, 256)
    xspec = pl.BlockSpec((tb, wb), lambda c, t: (t, c))
    vspec = pl.BlockSpec((1, wb), lambda c, t: (0, c))
    gspec = pl.BlockSpec((heads, blk, blk), lambda c, t: (c, 0, 0))
    return pl.pallas_call(
        functools.partial(_lru_core_kernel, heads=heads, blk=blk),
        grid=(W // wb, T // tb),
        in_specs=[xspec, xspec, pl.BlockSpec((CONV_WIDTH, wb), lambda c, t: (0, c)), vspec,
                  gspec, vspec, gspec, vspec, vspec],
        out_specs=xspec,
        out_shape=jax.ShapeDtypeStruct((T, W), bf16),
        scratch_shapes=[pltpu.VMEM((8, wb), f32), pltpu.VMEM((8, wb), f32),
                        pltpu.VMEM((tb, wb), f32), pltpu.VMEM((tb, wb), f32), pltpu.VMEM((tb, wb), f32)],
        compiler_params=_cparams("parallel", "arbitrary"),
        name="lru_core",
    )(xr, gb, conv_w, conv_b.reshape(1, W), gx_w.astype(bf16), gx_b.reshape(1, W),
      ga_w.astype(bf16), ga_b.reshape(1, W), a_param.reshape(1, W))


def _attn_kernel(q_ref, kp_ref, kc_ref, vp_ref, vc_ref, bias_ref, sink_ref, o_ref, *, group):
    blk = q_ref.shape[0]
    q = jnp.concatenate([q_ref[:, g * HEAD_DIM:(g + 1) * HEAD_DIM] for g in range(group)], axis=0)
    k = jnp.concatenate([kp_ref[...], kc_ref[...]], axis=0)
    v = jnp.concatenate([vp_ref[...], vc_ref[...]], axis=0)
    s = lax.dot_general(q, k, (((1,), (1,)), ((), ())), preferred_element_type=f32) * (HEAD_DIM ** -0.5)
    logits = s + bias_ref[0]
    col = lax.broadcasted_iota(jnp.int32, logits.shape, 1)
    no_prev = jnp.logical_and(pl.program_id(1) == 0, col < blk)
    logits = jnp.where(no_prev, -jnp.inf, logits)
    sink = sink_ref[0]
    m = jnp.maximum(jnp.max(logits, axis=-1, keepdims=True), sink)
    p = jnp.exp(logits - m)
    probs = p / (jnp.sum(p, axis=-1, keepdims=True) + jnp.exp(sink - m))
    o = jnp.dot(probs.astype(v.dtype), v, preferred_element_type=f32)
    for g in range(group):
        o_ref[:, g * HEAD_DIM:(g + 1) * HEAD_DIM] = o[g * blk:(g + 1) * blk, :].astype(o_ref.dtype)


def _attention(qkv, sinks, n_q, n_kv):
    T = qkv.shape[0]
    group = n_q // n_kv
    blk = WINDOW
    nb = T // blk
    i = jnp.arange(blk)[:, None]
    j = jnp.arange(2 * blk)[None, :]
    dist = blk + i - j
    in_band = (dist >= 0) & (dist < WINDOW)
    slopes = jnp.exp2(-8.0 * jnp.arange(1, n_q + 1, dtype=f32) / n_q)
    bias = jnp.where(in_band[None], -slopes[:, None, None] * dist.astype(f32)[None], -jnp.inf)
    bias = bias.reshape(n_kv, group * blk, 2 * blk)
    sink = jnp.repeat(sinks.astype(f32).reshape(n_kv, group), blk, axis=1).reshape(n_kv, group * blk, 1)

    kcol, vcol = n_q, n_q + n_kv
    prev = lambda n: jnp.maximum(n - 1, 0)
    hd = HEAD_DIM
    return pl.pallas_call(
        functools.partial(_attn_kernel, group=group),
        grid=(n_kv, nb),
        in_specs=[pl.BlockSpec((blk, group * hd), lambda k, n: (n, k)),
                  pl.BlockSpec((blk, hd), lambda k, n: (prev(n), kcol + k)),
                  pl.BlockSpec((blk, hd), lambda k, n: (n, kcol + k)),
                  pl.BlockSpec((blk, hd), lambda k, n: (prev(n), vcol + k)),
                  pl.BlockSpec((blk, hd), lambda k, n: (n, vcol + k)),
                  pl.BlockSpec((1, group * blk, 2 * blk), lambda k, n: (k, 0, 0)),
                  pl.BlockSpec((1, group * blk, 1), lambda k, n: (k, 0, 0))],
        out_specs=pl.BlockSpec((blk, group * hd), lambda k, n: (n, k)),
        out_shape=jax.ShapeDtypeStruct((T, n_q * hd), bf16),
        compiler_params=_cparams("parallel", "arbitrary"),
        name="swa_attention",
    )(qkv, qkv, qkv, qkv, qkv, bias, sink)


def _first_index_of_max(x, rows):
    m = jnp.max(x, axis=0, keepdims=True)
    idx = jnp.min(jnp.where(x == m, rows, x.shape[0]), axis=0, keepdims=True)
    return m, idx


def _router_kernel(h_ref, rwt_ref, rb_ref, idx_ref, gate_ref, rank_ref, cnt_ref, carry_sc, *, n_exp):
    tm = h_ref.shape[0]
    gsz = n_exp // N_GROUPS

    @pl.when(pl.program_id(0) == 0)
    def _():
        carry_sc[...] = jnp.zeros_like(carry_sc)

    logits = lax.dot_general(rwt_ref[...], h_ref[...], (((1,), (1,)), ((), ())),
                             precision=lax.Precision.HIGHEST, preferred_element_type=f32)
    scores = jax.nn.sigmoid(logits)
    sel = scores + rb_ref[...]

    grow = lax.broadcasted_iota(jnp.int32, (gsz, tm), 0)
    gscores = []
    for g in range(N_GROUPS):
        sg = sel[g * gsz:(g + 1) * gsz, :]
        m1, i1 = _first_index_of_max(sg, grow)
        m2 = jnp.max(jnp.where(grow == i1, -jnp.inf, sg), axis=0, keepdims=True)
        gscores.append(m1 + m2)
    gs = jnp.concatenate(gscores, axis=0)
    g_iota = lax.broadcasted_iota(jnp.int32, (N_GROUPS, tm), 0)
    gpick = jnp.zeros((N_GROUPS, tm), jnp.bool_)
    for _ in range(TOPK_GROUPS):
        _, gi = _first_index_of_max(jnp.where(gpick, -jnp.inf, gs), g_iota)
        gpick = jnp.logical_or(gpick, g_iota == gi)

    erow = lax.broadcasted_iota(jnp.int32, (n_exp, tm), 0)
    emask = jnp.concatenate(
        [jnp.broadcast_to(gpick[g:g + 1, :], (gsz, tm)) for g in range(N_GROUPS)], axis=0)
    cand = jnp.where(emask, sel, -jnp.inf)
    idxs, gates = [], []
    taken = jnp.zeros((n_exp, tm), jnp.bool_)
    for _ in range(TOP_K):
        _, ei = _first_index_of_max(jnp.where(taken, -jnp.inf, cand), erow)
        hit = erow == ei
        taken = jnp.logical_or(taken, hit)
        idxs.append(ei)
        gates.append(jnp.sum(jnp.where(hit, scores, 0.0), axis=0, keepdims=True))
    idx = jnp.concatenate(idxs, axis=0)
    gate = jnp.concatenate(gates, axis=0)
    gate = gate / jnp.sum(gate, axis=0, keepdims=True) * ROUTED_SCALE
    idx_ref[...] = idx
    gate_ref[...] = gate

    onehot = taken.astype(bf16)
    r_i = lax.broadcasted_iota(jnp.int32, (tm, tm), 0)
    c_i = lax.broadcasted_iota(jnp.int32, (tm, tm), 1)
    before = (r_i < c_i).astype(bf16)
    prefix = jnp.dot(onehot, before, preferred_element_type=f32) + carry_sc[:, 0:1]
    ranks = [jnp.sum(jnp.where(erow == idxs[k], prefix, 0.0), axis=0, keepdims=True) for k in range(TOP_K)]
    rank_ref[...] = jnp.concatenate(ranks, axis=0).astype(jnp.int32)
    total = carry_sc[...] + jnp.sum(taken.astype(f32), axis=1, keepdims=True)
    carry_sc[...] = total
    cnt_ref[...] = total


def _router(h, router_w, router_b):
    N, D = h.shape
    E = router_w.shape[1]
    tm = _pick(N, 512)
    kspec = pl.BlockSpec((TOP_K, tm), lambda i: (0, i))
    idx, gate, rank, cnt = pl.pallas_call(
        functools.partial(_router_kernel, n_exp=E),
        grid=(N // tm,),
        in_specs=[pl.BlockSpec((tm, D), lambda i: (i, 0)),
                  pl.BlockSpec((E, D), lambda i: (0, 0)),
                  pl.BlockSpec((E, 1), lambda i: (0, 0))],
        out_specs=[kspec, kspec, kspec, pl.BlockSpec((E, 128), lambda i: (0, 0))],
        out_shape=[jax.ShapeDtypeStruct((TOP_K, N), jnp.int32),
                   jax.ShapeDtypeStruct((TOP_K, N), f32),
                   jax.ShapeDtypeStruct((TOP_K, N), jnp.int32),
                   jax.ShapeDtypeStruct((E, 128), f32)],
        scratch_shapes=[pltpu.VMEM((E, 128), f32)],
        compiler_params=_cparams("arbitrary"),
        name="moe_router",
    )(h, router_w.T, router_b.reshape(E, 1))
    return idx, gate, rank, cnt[:, 0].astype(jnp.int32)


def _dispatch_kernel(dest_ref, hp_ref, xs_in_ref, xs_ref, idx_sm, sem_idx, sem):
    del xs_in_ref
    tm = hp_ref.shape[0]
    cp = pltpu.make_async_copy(dest_ref, idx_sm, sem_idx)
    cp.start()
    cp.wait()

    def row_copy(t, k):
        return pltpu.make_async_copy(hp_ref.at[pl.ds(t, 1)], xs_ref.at[pl.ds(idx_sm[k, t], 1)], sem)

    def start(t, c):
        for k in range(TOP_K):
            row_copy(t, k).start()
        return c

    def wait(t, c):
        for k in range(TOP_K):
            row_copy(t, k).wait()
        return c

    lax.fori_loop(0, tm, start, 0)
    lax.fori_loop(0, tm, wait, 0)


def _dispatch(hp, dest, rows):
    N, D2 = hp.shape
    tm = _pick(N, 256)
    xs0 = jnp.zeros((rows, D2), jnp.uint32)
    return pl.pallas_call(
        _dispatch_kernel,
        grid=(N // tm,),
        in_specs=[pl.BlockSpec((TOP_K, tm), lambda i: (0, i)),
                  pl.BlockSpec((tm, D2), lambda i: (i, 0)),
                  pl.BlockSpec(memory_space=pl.ANY)],
        out_specs=pl.BlockSpec(memory_space=pl.ANY),
        out_shape=jax.ShapeDtypeStruct((rows, D2), jnp.uint32),
        scratch_shapes=[pltpu.SMEM((TOP_K, tm), jnp.int32),
                        pltpu.SemaphoreType.DMA(()), pltpu.SemaphoreType.DMA(())],
        input_output_aliases={2: 0},
        compiler_params=_cparams("arbitrary"),
        name="moe_dispatch",
    )(dest, hp, xs0)


def _unpack_pairs(u):
    lo = lax.bitcast_convert_type(u << 16, f32).astype(bf16)
    hi = lax.bitcast_convert_type(u & jnp.uint32(0xFFFF0000), f32).astype(bf16)
    return lo, hi


def _silu(x):
    return x * jax.nn.sigmoid(x)


def _expert_kernel(be_ref, nu_ref, xs_ref, wg_ref, wv_ref, wd_ref, ys_ref):
    del be_ref

    @pl.when(pl.program_id(0) < nu_ref[0])
    def _():
        half = xs_ref.shape[1]
        lo, hi = _unpack_pairs(xs_ref[...])
        wg, wv = wg_ref[0], wv_ref[0]
        hg = (jnp.dot(lo, wg[:half], preferred_element_type=f32)
              + jnp.dot(hi, wg[half:], preferred_element_type=f32))
        hv = (jnp.dot(lo, wv[:half], preferred_element_type=f32)
              + jnp.dot(hi, wv[half:], preferred_element_type=f32))
        act = (_silu(hg) * hv).astype(bf16)
        ys_ref[...] = jnp.dot(act, wd_ref[0], preferred_element_type=f32)


def _experts(xs, w_gate, w_val, w_down, block_expert, n_used):
    R, D2 = xs.shape
    E, D, H = w_gate.shape
    B = EXPERT_ROWS
    live = lambda i, be, nu: jnp.minimum(i, nu[0] - 1)
    return pl.pallas_call(
        _expert_kernel,
        grid_spec=pltpu.PrefetchScalarGridSpec(
            num_scalar_prefetch=2,
            grid=(R // B,),
            in_specs=[pl.BlockSpec((B, D2), lambda i, be, nu: (live(i, be, nu), 0)),
                      pl.BlockSpec((1, D, H), lambda i, be, nu: (be[i], 0, 0)),
                      pl.BlockSpec((1, D, H), lambda i, be, nu: (be[i], 0, 0)),
                      pl.BlockSpec((1, H, D), lambda i, be, nu: (be[i], 0, 0))],
            out_specs=pl.BlockSpec((B, D), lambda i, be, nu: (live(i, be, nu), 0)),
        ),
        out_shape=jax.ShapeDtypeStruct((R, D), f32),
        compiler_params=_cparams("arbitrary"),
        name="moe_experts",
    )(block_expert, n_used, xs, w_gate, w_val, w_down)


def _combine_kernel(dest_ref, gate_ref, h_ref, hb_ref, sg_ref, sv_ref, sd_ref, g_ref, b_ref, ys_ref,
                    ho_ref, hbo_ref, hpo_ref, idx_sm, ybuf, sem_idx, sem, *, alpha):
    tm = h_ref.shape[0]
    cp = pltpu.make_async_copy(dest_ref, idx_sm, sem_idx)
    cp.start()
    cp.wait()

    def row_copy(t, k):
        return pltpu.make_async_copy(ys_ref.at[pl.ds(idx_sm[k, t], 1)], ybuf.at[k, pl.ds(t, 1)], sem)

    def start(t, c):
        for k in range(TOP_K):
            row_copy(t, k).start()
        return c

    def wait(t, c):
        for k in range(TOP_K):
            row_copy(t, k).wait()
        return c

    lax.fori_loop(0, tm, start, 0)
    hb = hb_ref[...]
    act = (_silu(jnp.dot(hb, sg_ref[...], preferred_element_type=f32))
           * jnp.dot(hb, sv_ref[...], preferred_element_type=f32)).astype(bf16)
    z = alpha * h_ref[...] + jnp.dot(act, sd_ref[...], preferred_element_type=f32)
    lax.fori_loop(0, tm, wait, 0)
    gate = gate_ref[...]
    for k in range(TOP_K):
        z = z + ybuf[k] * gate[:, k:k + 1]
    _store_hidden(_layer_norm_rows(z, g_ref[...], b_ref[...]), ho_ref, hbo_ref, hpo_ref)


def _combine(ys, dest, gate_t, h, hb, s_gate, s_val, s_down, ln_g, ln_b, alpha):
    N, D = h.shape
    H = s_gate.shape[1]
    tm = _pick(N, 128)
    specs, shapes = _hidden_out(N, D, tm)
    row = pl.BlockSpec((tm, D), lambda i: (i, 0))
    full = lambda a, b: pl.BlockSpec((a, b), lambda i: (0, 0))
    return pl.pallas_call(
        functools.partial(_combine_kernel, alpha=alpha),
        grid=(N // tm,),
        in_specs=[pl.BlockSpec((TOP_K, tm), lambda i: (0, i)),
                  pl.BlockSpec((tm, TOP_K), lambda i: (i, 0)),
                  row, row, full(D, H), full(D, H), full(H, D), full(1, D), full(1, D),
                  pl.BlockSpec(memory_space=pl.ANY)],
        out_specs=specs,
        out_shape=shapes,
        scratch_shapes=[pltpu.SMEM((TOP_K, tm), jnp.int32), pltpu.VMEM((TOP_K, tm, D), f32),
                        pltpu.SemaphoreType.DMA(()), pltpu.SemaphoreType.DMA(())],
        compiler_params=_cparams("arbitrary"),
        name="moe_combine",
    )(dest, gate_t, h, hb, s_gate, s_val, s_down, ln_g.reshape(1, D), ln_b.reshape(1, D), ys)


def _moe_layer(h, hb, hp, router_w, router_b, w_gate, w_val, w_down, s_gate, s_val, s_down,
               ln_g, ln_b, alpha):
    N, D = h.shape
    E = router_w.shape[1]
    B = EXPERT_ROWS
    idx, gate, rank, counts = _router(h, router_w, router_b)
    blocks = (counts + B - 1) // B
    bend = jnp.cumsum(blocks)
    pstart = (bend - blocks) * B
    n_blocks = (N * TOP_K) // B + E
    n_used = bend[-1:].astype(jnp.int32)
    block_expert = jnp.minimum(
        jnp.searchsorted(bend, jnp.arange(n_blocks, dtype=jnp.int32), side="right"), E - 1).astype(jnp.int32)
    onehot = idx[:, :, None] == jnp.arange(E, dtype=jnp.int32)
    dest = rank + jnp.sum(jnp.where(onehot, pstart.astype(jnp.int32), 0), axis=-1)

    xs = _dispatch(hp, dest, n_blocks * B)
    ys = _experts(xs, w_gate, w_val, w_down, block_expert, n_used)
    return _combine(ys, dest, gate.T, h, hb, s_gate, s_val, s_down, ln_g, ln_b, alpha)


def kernel(x, lru_w_in, lru_b_in, lru_conv_w, lru_conv_b, lru_gx_w, lru_gx_b, lru_ga_w, lru_ga_b,
           lru_a_param, lru_w_out, attn_w_qkv, attn_sinks, attn_w_o, ln_g, ln_b,
           moe_router_w, moe_router_b, moe_w_up, moe_w_down, moe_shared_up, moe_shared_down):
    Bsz, T, D = x.shape
    depth = ln_g.shape[0]
    alpha = (2.0 * depth) ** 0.25
    W = lru_w_in.shape[2] // 2
    H = moe_w_down.shape[2]
    n_q = D // HEAD_DIM
    n_kv = (attn_w_qkv.shape[2] // HEAD_DIM - n_q) // 2
    N = Bsz * T
    assert Bsz == 1, "sequence mixers below assume one sequence"

    h = x.reshape(N, D)
    hb = h.astype(bf16)
    hp = None
    for layer in range(depth):
        j = layer // 2
        if layer % 2 == 0:
            w_in = lru_w_in[j].astype(bf16)
            xr, gb = _lru_in(hb, w_in[:, :W], w_in[:, W:],
                             lru_b_in[j, :W].reshape(1, W), lru_b_in[j, W:].reshape(1, W))
            y = _lru_core(xr, gb, lru_conv_w[j], lru_conv_b[j], lru_gx_w[j], lru_gx_b[j].reshape(-1),
                          lru_ga_w[j], lru_ga_b[j].reshape(-1), lru_a_param[j])
            z = _matmul_residual(y, lru_w_out[j].astype(bf16), h, alpha)
        else:
            qkv = _matmul(hb, attn_w_qkv[j].astype(bf16), bf16)
            o = _attention(qkv, attn_sinks[j], n_q, n_kv)
            z = _matmul_residual(o, attn_w_o[j].astype(bf16), h, alpha)
        h, hb, hp = _layer_norm(z, ln_g[layer, 0], ln_b[layer, 0])
        h, hb, hp = _moe_layer(
            h, hb, hp, moe_router_w[layer], moe_router_b[layer],
            moe_w_up[layer, :, :, :H].astype(bf16), moe_w_up[layer, :, :, H:].astype(bf16),
            moe_w_down[layer].astype(bf16),
            moe_shared_up[layer, :, :H].astype(bf16), moe_shared_up[layer, :, H:].astype(bf16),
            moe_shared_down[layer].astype(bf16), ln_g[layer, 1], ln_b[layer, 1], alpha)
    return h.reshape(Bsz, T, D)
```

```python
import functools

import jax
import jax.numpy as jnp
from jax import lax
from jax.experimental import pallas as pl
from jax.experimental.pallas import tpu as pltpu

HEAD_DIM = 128
WINDOW = 128
CONV_WIDTH = 4
LRU_C = 8.0
TOP_K = 8
N_GROUPS = 8
TOPK_GROUPS = 4
ROUTED_SCALE = 2.5
LN_EPS = 1e-5

V7X_VMEM_BYTES = 64 * 1024 * 1024
VMEM_LIMIT = V7X_VMEM_BYTES - 8 * 1024 * 1024
EXPERT_ROWS = 256

f32 = jnp.float32
bf16 = jnp.bfloat16


def _cparams(*sem):
    return pltpu.CompilerParams(dimension_semantics=sem, vmem_limit_bytes=VMEM_LIMIT)


def _pick(n, pref):
    t = min(pref, n)
    while n % t:
        t //= 2
    return t


def _mm_kernel(x_ref, w_ref, o_ref):
    o_ref[...] = jnp.dot(x_ref[...], w_ref[...], preferred_element_type=f32).astype(o_ref.dtype)


def _matmul(x, w, out_dtype):
    M, K = x.shape
    N = w.shape[1]
    tm, tn = _pick(M, 1024), _pick(N, 512)
    return pl.pallas_call(
        _mm_kernel,
        grid=(M // tm, N // tn),
        in_specs=[pl.BlockSpec((tm, K), lambda i, j: (i, 0)),
                  pl.BlockSpec((K, tn), lambda i, j: (0, j))],
        out_specs=pl.BlockSpec((tm, tn), lambda i, j: (i, j)),
        out_shape=jax.ShapeDtypeStruct((M, N), out_dtype),
        compiler_params=_cparams("parallel", "arbitrary"),
        name="matmul",
    )(x, w)


def _mm_res_kernel(x_ref, w_ref, r_ref, o_ref, *, alpha):
    o_ref[...] = alpha * r_ref[...] + jnp.dot(x_ref[...], w_ref[...], preferred_element_type=f32)


def _matmul_residual(x, w, res, alpha):
    M, K = x.shape
    N = w.shape[1]
    tm, tn = _pick(M, 1024), _pick(N, 512)
    return pl.pallas_call(
        functools.partial(_mm_res_kernel, alpha=alpha),
        grid=(M // tm, N // tn),
        in_specs=[pl.BlockSpec((tm, K), lambda i, j: (i, 0)),
                  pl.BlockSpec((K, tn), lambda i, j: (0, j)),
                  pl.BlockSpec((tm, tn), lambda i, j: (i, j))],
        out_specs=pl.BlockSpec((tm, tn), lambda i, j: (i, j)),
        out_shape=jax.ShapeDtypeStruct((M, N), f32),
        compiler_params=_cparams("parallel", "arbitrary"),
        name="matmul_residual",
    )(x, w, res)


def _gelu_tanh(x):
    return 0.5 * x * (1.0 + jnp.tanh(0.7978845608028654 * (x + 0.044715 * (x * x * x))))


def _lru_in_kernel(x_ref, wx_ref, wy_ref, bx_ref, by_ref, xr_ref, gb_ref):
    x = x_ref[...]
    xr_ref[...] = jnp.dot(x, wx_ref[...], preferred_element_type=f32) + bx_ref[...]
    gb_ref[...] = _gelu_tanh(jnp.dot(x, wy_ref[...], preferred_element_type=f32) + by_ref[...])


def _lru_in(x, wx, wy, bx, by):
    M, K = x.shape
    W = wx.shape[1]
    tm, tn = _pick(M, 1024), _pick(W, 256)
    wspec = pl.BlockSpec((K, tn), lambda i, j: (0, j))
    bspec = pl.BlockSpec((1, tn), lambda i, j: (0, j))
    ospec = pl.BlockSpec((tm, tn), lambda i, j: (i, j))
    return pl.pallas_call(
        _lru_in_kernel,
        grid=(M // tm, W // tn),
        in_specs=[pl.BlockSpec((tm, K), lambda i, j: (i, 0)), wspec, wspec, bspec, bspec],
        out_specs=[ospec, ospec],
        out_shape=[jax.ShapeDtypeStruct((M, W), f32)] * 2,
        compiler_params=_cparams("parallel", "arbitrary"),
        name="lru_in",
    )(x, wx, wy, bx, by)


def _layer_norm_rows(z, g, b):
    mu = jnp.mean(z, axis=-1, keepdims=True)
    zc = z - mu
    var = jnp.mean(zc * zc, axis=-1, keepdims=True)
    return zc * lax.rsqrt(var + LN_EPS) * g + b


def _store_hidden(h, h_ref, hb_ref, hp_ref):
    h_ref[...] = h
    hb = h.astype(bf16)
    hb_ref[...] = hb
    half = h.shape[1] // 2
    bits = lax.bitcast_convert_type(hb.astype(f32), jnp.uint32)
    hp_ref[...] = (bits[:, :half] >> 16) | bits[:, half:]


def _ln_kernel(z_ref, g_ref, b_ref, h_ref, hb_ref, hp_ref):
    _store_hidden(_layer_norm_rows(z_ref[...], g_ref[...], b_ref[...]), h_ref, hb_ref, hp_ref)


def _hidden_out(M, D, tm):
    specs = [pl.BlockSpec((tm, D), lambda i: (i, 0)),
             pl.BlockSpec((tm, D), lambda i: (i, 0)),
             pl.BlockSpec((tm, D // 2), lambda i: (i, 0))]
    shapes = [jax.ShapeDtypeStruct((M, D), f32),
              jax.ShapeDtypeStruct((M, D), bf16),
              jax.ShapeDtypeStruct((M, D // 2), jnp.uint32)]
    return specs, shapes


def _layer_norm(z, g, b):
    M, D = z.shape
    tm = _pick(M, 256)
    specs, shapes = _hidden_out(M, D, tm)
    vspec = pl.BlockSpec((1, D), lambda i: (0, 0))
    return pl.pallas_call(
        _ln_kernel,
        grid=(M // tm,),
        in_specs=[pl.BlockSpec((tm, D), lambda i: (i, 0)), vspec, vspec],
        out_specs=specs,
        out_shape=shapes,
        compiler_params=_cparams("parallel"),
        name="layer_norm",
    )(z, g.reshape(1, D), b.reshape(1, D))


def _sigmoid(x):
    return 0.5 + 0.5 * jnp.tanh(0.5 * x)


def _softplus(x):
    return jnp.maximum(x, 0.0) + jnp.log1p(jnp.exp(-jnp.abs(x)))


def _lru_core_kernel(xr_ref, gb_ref, cw_ref, cb_ref, gxw_ref, gxb_ref, gaw_ref, gab_ref, ap_ref,
                     y_ref, tail_sc, h_sc, a_sc, b_sc, hs_sc, *, heads, blk):
    tb = xr_ref.shape[0]

    @pl.when(pl.program_id(1) == 0)
    def _():
        tail_sc[...] = jnp.zeros_like(tail_sc)
        h_sc[...] = jnp.zeros_like(h_sc)

    xr = xr_ref[...]
    ext = jnp.concatenate([tail_sc[...], xr], axis=0)
    tail_sc[...] = xr[tb - 8:, :]
    cw = cw_ref[...]
    xc = cb_ref[...]
    for j in range(CONV_WIDTH):
        off = 8 - (CONV_WIDTH - 1) + j
        xc = xc + cw[j:j + 1, :] * ext[off:off + tb, :]
    xcb = xc.astype(bf16)

    for hd in range(heads):
        sl = slice(hd * blk, (hd + 1) * blk)
        xh = xcb[:, sl]
        gate_x = _sigmoid(jnp.dot(xh, gxw_ref[hd], preferred_element_type=f32) + gxb_ref[:, sl])
        gate_a = _sigmoid(jnp.dot(xh, gaw_ref[hd], preferred_element_type=f32) + gab_ref[:, sl])
        a = jnp.exp((-LRU_C * _softplus(-ap_ref[:, sl])) * gate_a)
        a_sc[:, sl] = a
        b_sc[:, sl] = xc[:, sl] * gate_x * jnp.sqrt(1.0 - a * a)

    def body(i, h):
        base = pl.multiple_of(i * 8, 8)
        a8 = a_sc[pl.ds(base, 8), :]
        b8 = b_sc[pl.ds(base, 8), :]
        rows = []
        for r in range(8):
            h = a8[r:r + 1, :] * h + b8[r:r + 1, :]
            rows.append(h)
        hs_sc[pl.ds(base, 8), :] = jnp.concatenate(rows, axis=0)
        return h

    h_sc[0:1, :] = lax.fori_loop(0, tb // 8, body, h_sc[0:1, :])
    y_ref[...] = (hs_sc[...] * gb_ref[...]).astype(y_ref.dtype)


def _lru_core(xr, gb, conv_w, conv_b, gx_w, gx_b, ga_w, ga_b, a_param):
    T, W = xr.shape
    nh, blk = gx_w.shape[0], gx_w.shape[1]
    wb = _pick(W, 1024)
    heads = wb // blk
    tb = _pick(T, 256)
    xspec = pl.BlockSpec((tb, wb), lambda c, t: (t, c))
    vspec = pl.BlockSpec((1, wb), lambda c, t: (0, c))
    gspec = pl.BlockSpec((heads, blk, blk), lambda c, t: (c, 0, 0))
    return pl.pallas_call(
        functools.partial(_lru_core_kernel, heads=heads, blk=blk),
        grid=(W // wb, T // tb),
        in_specs=[xspec, xspec, pl.BlockSpec((CONV_WIDTH, wb), lambda c, t: (0, c)), vspec,
                  gspec, vspec, gspec, vspec, vspec],
        out_specs=xspec,
        out_shape=jax.ShapeDtypeStruct((T, W), bf16),
        scratch_shapes=[pltpu.VMEM((8, wb), f32), pltpu.VMEM((8, wb), f32),
                        pltpu.VMEM((tb, wb), f32), pltpu.VMEM((tb, wb), f32), pltpu.VMEM((tb, wb), f32)],
        compiler_params=_cparams("parallel", "arbitrary"),
        name="lru_core",
    )(xr, gb, conv_w, conv_b.reshape(1, W), gx_w.astype(bf16), gx_b.reshape(1, W),
      ga_w.astype(bf16), ga_b.reshape(1, W), a_param.reshape(1, W))


def _attn_kernel(q_ref, kp_ref, kc_ref, vp_ref, vc_ref, bias_ref, sink_ref, o_ref, *, group, n_kv):
    blk = q_ref.shape[0]
    hd = HEAD_DIM
    col = lax.broadcasted_iota(jnp.int32, (group * blk, 2 * blk), 1)
    no_prev = jnp.logical_and(pl.program_id(0) == 0, col < blk)
    for kv in range(n_kv):
        q = jnp.concatenate([q_ref[:, (kv * group + g) * hd:(kv * group + g + 1) * hd] for g in range(group)],
                            axis=0)
        ksl = slice(kv * hd, (kv + 1) * hd)
        k = jnp.concatenate([kp_ref[:, ksl], kc_ref[:, ksl]], axis=0)
        v = jnp.concatenate([vp_ref[:, ksl], vc_ref[:, ksl]], axis=0)
        s = lax.dot_general(q, k, (((1,), (1,)), ((), ())), preferred_element_type=f32) * (hd ** -0.5)
        logits = jnp.where(no_prev, -jnp.inf, s + bias_ref[kv])
        sink = sink_ref[kv]
        m = jnp.maximum(jnp.max(logits, axis=-1, keepdims=True), sink)
        p = jnp.exp(logits - m)
        probs = p / (jnp.sum(p, axis=-1, keepdims=True) + jnp.exp(sink - m))
        o = jnp.dot(probs.astype(v.dtype), v, preferred_element_type=f32)
        for g in range(group):
            o_ref[:, (kv * group + g) * hd:(kv * group + g + 1) * hd] = (
                o[g * blk:(g + 1) * blk, :].astype(o_ref.dtype))


def _attention(qkv, sinks, n_q, n_kv):
    T = qkv.shape[0]
    group = n_q // n_kv
    blk = WINDOW
    nb = T // blk
    i = jnp.arange(blk)[:, None]
    j = jnp.arange(2 * blk)[None, :]
    dist = blk + i - j
    in_band = (dist >= 0) & (dist < WINDOW)
    slopes = jnp.exp2(-8.0 * jnp.arange(1, n_q + 1, dtype=f32) / n_q)
    bias = jnp.where(in_band[None], -slopes[:, None, None] * dist.astype(f32)[None], -jnp.inf)
    bias = bias.reshape(n_kv, group * blk, 2 * blk)
    sink = jnp.repeat(sinks.astype(f32).reshape(n_kv, group), blk, axis=1).reshape(n_kv, group * blk, 1)

    assert group * n_kv == n_q
    prev = lambda n: jnp.maximum(n - 1, 0)
    kvw = n_kv * HEAD_DIM
    return pl.pallas_call(
        functools.partial(_attn_kernel, group=group, n_kv=n_kv),
        grid=(nb,),
        in_specs=[pl.BlockSpec((blk, n_q * HEAD_DIM), lambda n: (n, 0)),
                  pl.BlockSpec((blk, kvw), lambda n: (prev(n), group)),
                  pl.BlockSpec((blk, kvw), lambda n: (n, group)),
                  pl.BlockSpec((blk, kvw), lambda n: (prev(n), group + 1)),
                  pl.BlockSpec((blk, kvw), lambda n: (n, group + 1)),
                  pl.BlockSpec((n_kv, group * blk, 2 * blk), lambda n: (0, 0, 0)),
                  pl.BlockSpec((n_kv, group * blk, 1), lambda n: (0, 0, 0))],
        out_specs=pl.BlockSpec((blk, n_q * HEAD_DIM), lambda n: (n, 0)),
        out_shape=jax.ShapeDtypeStruct((T, n_q * HEAD_DIM), bf16),
        compiler_params=_cparams("arbitrary"),
        name="swa_attention",
    )(qkv, qkv, qkv, qkv, qkv, bias, sink)


def _first_index_of_max(x, rows):
    m = jnp.max(x, axis=0, keepdims=True)
    idx = jnp.min(jnp.where(x == m, rows, x.shape[0]), axis=0, keepdims=True)
    return m, idx


def _router_kernel(h_ref, rwt_ref, rb_ref, idx_ref, gate_ref, rank_ref, cnt_ref, carry_sc, *, n_exp):
    tm = h_ref.shape[0]
    gsz = n_exp // N_GROUPS

    @pl.when(pl.program_id(0) == 0)
    def _():
        carry_sc[...] = jnp.zeros_like(carry_sc)

    logits = lax.dot_general(rwt_ref[...], h_ref[...], (((1,), (1,)), ((), ())),
                             precision=lax.Precision.HIGHEST, preferred_element_type=f32)
    scores = jax.nn.sigmoid(logits)
    sel = scores + rb_ref[...]

    grow = lax.broadcasted_iota(jnp.int32, (gsz, tm), 0)
    gscores = []
    for g in range(N_GROUPS):
        sg = sel[g * gsz:(g + 1) * gsz, :]
        m1, i1 = _first_index_of_max(sg, grow)
        m2 = jnp.max(jnp.where(grow == i1, -jnp.inf, sg), axis=0, keepdims=True)
        gscores.append(m1 + m2)
    gs = jnp.concatenate(gscores, axis=0)
    g_iota = lax.broadcasted_iota(jnp.int32, (N_GROUPS, tm), 0)
    gpick = jnp.zeros((N_GROUPS, tm), jnp.bool_)
    for _ in range(TOPK_GROUPS):
        _, gi = _first_index_of_max(jnp.where(gpick, -jnp.inf, gs), g_iota)
        gpick = jnp.logical_or(gpick, g_iota == gi)

    erow = lax.broadcasted_iota(jnp.int32, (n_exp, tm), 0)
    emask = jnp.concatenate(
        [jnp.broadcast_to(gpick[g:g + 1, :], (gsz, tm)) for g in range(N_GROUPS)], axis=0)
    cand = jnp.where(emask, sel, -jnp.inf)
    idxs, gates = [], []
    taken = jnp.zeros((n_exp, tm), jnp.bool_)
    for _ in range(TOP_K):
        _, ei = _first_index_of_max(jnp.where(taken, -jnp.inf, cand), erow)
        hit = erow == ei
        taken = jnp.logical_or(taken, hit)
        idxs.append(ei)
        gates.append(jnp.sum(jnp.where(hit, scores, 0.0), axis=0, keepdims=True))
    idx = jnp.concatenate(idxs, axis=0)
    gate = jnp.concatenate(gates, axis=0)
    gate = gate / jnp.sum(gate, axis=0, keepdims=True) * ROUTED_SCALE
    idx_ref[...] = idx
    gate_ref[...] = gate

    onehot = taken.astype(bf16)
    r_i = lax.broadcasted_iota(jnp.int32, (tm, tm), 0)
    c_i = lax.broadcasted_iota(jnp.int32, (tm, tm), 1)
    before = (r_i < c_i).astype(bf16)
    prefix = jnp.dot(onehot, before, preferred_element_type=f32) + carry_sc[:, 0:1]
    ranks = [jnp.sum(jnp.where(erow == idxs[k], prefix, 0.0), axis=0, keepdims=True) for k in range(TOP_K)]
    rank_ref[...] = jnp.concatenate(ranks, axis=0).astype(jnp.int32)
    total = carry_sc[...] + jnp.sum(taken.astype(f32), axis=1, keepdims=True)
    carry_sc[...] = total
    cnt_ref[...] = total


def _router(h, router_w, router_b):
    N, D = h.shape
    E = router_w.shape[1]
    tm = _pick(N, 512)
    kspec = pl.BlockSpec((TOP_K, tm), lambda i: (0, i))
    idx, gate, rank, cnt = pl.pallas_call(
        functools.partial(_router_kernel, n_exp=E),
        grid=(N // tm,),
        in_specs=[pl.BlockSpec((tm, D), lambda i: (i, 0)),
                  pl.BlockSpec((E, D), lambda i: (0, 0)),
                  pl.BlockSpec((E, 1), lambda i: (0, 0))],
        out_specs=[kspec, kspec, kspec, pl.BlockSpec((E, 128), lambda i: (0, 0))],
        out_shape=[jax.ShapeDtypeStruct((TOP_K, N), jnp.int32),
                   jax.ShapeDtypeStruct((TOP_K, N), f32),
                   jax.ShapeDtypeStruct((TOP_K, N), jnp.int32),
                   jax.ShapeDtypeStruct((E, 128), f32)],
        scratch_shapes=[pltpu.VMEM((E, 128), f32)],
        compiler_params=_cparams("arbitrary"),
        name="moe_router",
    )(h, router_w.T, router_b.reshape(E, 1))
    return idx, gate, rank, cnt[:, 0].astype(jnp.int32)


def _dispatch_kernel(dest_ref, hp_ref, xs_in_ref, xs_ref, idx_sm, sem_idx, sem):
    del xs_in_ref
    tm = hp_ref.shape[0]
    cp = pltpu.make_async_copy(dest_ref, idx_sm, sem_idx)
    cp.start()
    cp.wait()

    def row_copy(t, k):
        return pltpu.make_async_copy(hp_ref.at[pl.ds(t, 1)], xs_ref.at[pl.ds(idx_sm[k, t], 1)], sem)

    def start(t, c):
        for k in range(TOP_K):
            row_copy(t, k).start()
        return c

    def wait(t, c):
        for k in range(TOP_K):
            row_copy(t, k).wait()
        return c

    lax.fori_loop(0, tm, start, 0)
    lax.fori_loop(0, tm, wait, 0)


def _dispatch(hp, dest, rows):
    N, D2 = hp.shape
    tm = _pick(N, 256)
    xs0 = jnp.zeros((rows, D2), jnp.uint32)
    return pl.pallas_call(
        _dispatch_kernel,
        grid=(N // tm,),
        in_specs=[pl.BlockSpec((TOP_K, tm), lambda i: (0, i)),
                  pl.BlockSpec((tm, D2), lambda i: (i, 0)),
                  pl.BlockSpec(memory_space=pl.ANY)],
        out_specs=pl.BlockSpec(memory_space=pl.ANY),
        out_shape=jax.ShapeDtypeStruct((rows, D2), jnp.uint32),
        scratch_shapes=[pltpu.SMEM((TOP_K, tm), jnp.int32),
                        pltpu.SemaphoreType.DMA(()), pltpu.SemaphoreType.DMA(())],
        input_output_aliases={2: 0},
        compiler_params=_cparams("arbitrary"),
        name="moe_dispatch",
    )(dest, hp, xs0)


def _unpack_pairs(u):
    lo = lax.bitcast_convert_type(u << 16, f32).astype(bf16)
    hi = lax.bitcast_convert_type(u & jnp.uint32(0xFFFF0000), f32).astype(bf16)
    return lo, hi


def _silu(x):
    return x * jax.nn.sigmoid(x)


def _expert_kernel(be_ref, nu_ref, xs_ref, wg_ref, wv_ref, wd_ref, ys_ref):
    del be_ref

    @pl.when(pl.program_id(0) < nu_ref[0])
    def _():
        half = xs_ref.shape[1]
        lo, hi = _unpack_pairs(xs_ref[...])
        wg, wv = wg_ref[0], wv_ref[0]
        hg = (jnp.dot(lo, wg[:half], preferred_element_type=f32)
              + jnp.dot(hi, wg[half:], preferred_element_type=f32))
        hv = (jnp.dot(lo, wv[:half], preferred_element_type=f32)
              + jnp.dot(hi, wv[half:], preferred_element_type=f32))
        act = (_silu(hg) * hv).astype(bf16)
        ys_ref[...] = jnp.dot(act, wd_ref[0], preferred_element_type=f32)


def _experts(xs, w_gate, w_val, w_down, block_expert, n_used):
    R, D2 = xs.shape
    E, D, H = w_gate.shape
    B = EXPERT_ROWS
    live = lambda i, be, nu: jnp.minimum(i, nu[0] - 1)
    return pl.pallas_call(
        _expert_kernel,
        grid_spec=pltpu.PrefetchScalarGridSpec(
            num_scalar_prefetch=2,
            grid=(R // B,),
            in_specs=[pl.BlockSpec((B, D2), lambda i, be, nu: (live(i, be, nu), 0)),
                      pl.BlockSpec((1, D, H), lambda i, be, nu: (be[i], 0, 0)),
                      pl.BlockSpec((1, D, H), lambda i, be, nu: (be[i], 0, 0)),
                      pl.BlockSpec((1, H, D), lambda i, be, nu: (be[i], 0, 0))],
            out_specs=pl.BlockSpec((B, D), lambda i, be, nu: (live(i, be, nu), 0)),
        ),
        out_shape=jax.ShapeDtypeStruct((R, D), f32),
        compiler_params=_cparams("arbitrary"),
        name="moe_experts",
    )(block_expert, n_used, xs, w_gate, w_val, w_down)


def _combine_kernel(dest_ref, gate_ref, h_ref, hb_ref, sg_ref, sv_ref, sd_ref, g_ref, b_ref, ys_ref,
                    ho_ref, hbo_ref, hpo_ref, idx_sm, ybuf, sem_idx, sem, *, alpha):
    tm = h_ref.shape[0]
    cp = pltpu.make_async_copy(dest_ref, idx_sm, sem_idx)
    cp.start()
    cp.wait()

    def row_copy(t, k):
        return pltpu.make_async_copy(ys_ref.at[pl.ds(idx_sm[k, t], 1)], ybuf.at[k, pl.ds(t, 1)], sem)

    def start(t, c):
        for k in range(TOP_K):
            row_copy(t, k).start()
        return c

    def wait(t, c):
        for k in range(TOP_K):
            row_copy(t, k).wait()
        return c

    lax.fori_loop(0, tm, start, 0)
    hb = hb_ref[...]
    act = (_silu(jnp.dot(hb, sg_ref[...], preferred_element_type=f32))
           * jnp.dot(hb, sv_ref[...], preferred_element_type=f32)).astype(bf16)
    z = alpha * h_ref[...] + jnp.dot(act, sd_ref[...], preferred_element_type=f32)
    lax.fori_loop(0, tm, wait, 0)
    gate = gate_ref[...]
    for k in range(TOP_K):
        z = z + ybuf[k] * gate[:, k:k + 1]
    _store_hidden(_layer_norm_rows(z, g_ref[...], b_ref[...]), ho_ref, hbo_ref, hpo_ref)


def _combine(ys, dest, gate_t, h, hb, s_gate, s_val, s_down, ln_g, ln_b, alpha):
    N, D = h.shape
    H = s_gate.shape[1]
    tm = _pick(N, 128)
    specs, shapes = _hidden_out(N, D, tm)
    row = pl.BlockSpec((tm, D), lambda i: (i, 0))
    full = lambda a, b: pl.BlockSpec((a, b), lambda i: (0, 0))
    return pl.pallas_call(
        functools.partial(_combine_kernel, alpha=alpha),
        grid=(N // tm,),
        in_specs=[pl.BlockSpec((TOP_K, tm), lambda i: (0, i)),
                  pl.BlockSpec((tm, TOP_K), lambda i: (i, 0)),
                  row, row, full(D, H), full(D, H), full(H, D), full(1, D), full(1, D),
                  pl.BlockSpec(memory_space=pl.ANY)],
        out_specs=specs,
        out_shape=shapes,
        scratch_shapes=[pltpu.SMEM((TOP_K, tm), jnp.int32), pltpu.VMEM((TOP_K, tm, D), f32),
                        pltpu.SemaphoreType.DMA(()), pltpu.SemaphoreType.DMA(())],
        compiler_params=_cparams("arbitrary"),
        name="moe_combine",
    )(dest, gate_t, h, hb, s_gate, s_val, s_down, ln_g.reshape(1, D), ln_b.reshape(1, D), ys)


def _moe_layer(h, hb, hp, router_w, router_b, w_gate, w_val, w_down, s_gate, s_val, s_down,
               ln_g, ln_b, alpha):
    N, D = h.shape
    E = router_w.shape[1]
    B = EXPERT_ROWS
    idx, gate, rank, counts = _router(h, router_w, router_b)
    blocks = (counts + B - 1) // B
    bend = jnp.cumsum(blocks)
    pstart = (bend - blocks) * B
    n_blocks = (N * TOP_K) // B + E
    n_used = bend[-1:].astype(jnp.int32)
    block_ids = jnp.arange(n_blocks, dtype=jnp.int32)
    block_expert = jnp.minimum(jnp.sum(bend[None, :] <= block_ids[:, None], axis=1), E - 1).astype(jnp.int32)
    onehot = idx[:, :, None] == jnp.arange(E, dtype=jnp.int32)
    dest = rank + jnp.sum(jnp.where(onehot, pstart.astype(jnp.int32), 0), axis=-1)

    xs = _dispatch(hp, dest, n_blocks * B)
    ys = _experts(xs, w_gate, w_val, w_down, block_expert, n_used)
    return _combine(ys, dest, gate.T, h, hb, s_gate, s_val, s_down, ln_g, ln_b, alpha)


def kernel(x, lru_w_in, lru_b_in, lru_conv_w, lru_conv_b, lru_gx_w, lru_gx_b, lru_ga_w, lru_ga_b,
           lru_a_param, lru_w_out, attn_w_qkv, attn_sinks, attn_w_o, ln_g, ln_b,
           moe_router_w, moe_router_b, moe_w_up, moe_w_down, moe_shared_up, moe_shared_down):
    Bsz, T, D = x.shape
    depth = ln_g.shape[0]
    alpha = (2.0 * depth) ** 0.25
    W = lru_w_in.shape[2] // 2
    H = moe_w_down.shape[2]
    n_q = D // HEAD_DIM
    n_kv = (attn_w_qkv.shape[2] // HEAD_DIM - n_q) // 2
    N = Bsz * T
    assert Bsz == 1, "sequence mixers below assume one sequence"

    h = x.reshape(N, D)
    hb = h.astype(bf16)
    hp = None
    for layer in range(depth):
        j = layer // 2
        if layer % 2 == 0:
            w_in = lru_w_in[j].astype(bf16)
            xr, gb = _lru_in(hb, w_in[:, :W], w_in[:, W:],
                             lru_b_in[j, :W].reshape(1, W), lru_b_in[j, W:].reshape(1, W))
            y = _lru_core(xr, gb, lru_conv_w[j], lru_conv_b[j], lru_gx_w[j], lru_gx_b[j].reshape(-1),
                          lru_ga_w[j], lru_ga_b[j].reshape(-1), lru_a_param[j])
            z = _matmul_residual(y, lru_w_out[j].astype(bf16), h, alpha)
        else:
            qkv = _matmul(hb, attn_w_qkv[j].astype(bf16), bf16)
            o = _attention(qkv, attn_sinks[j], n_q, n_kv)
            z = _matmul_residual(o, attn_w_o[j].astype(bf16), h, alpha)
        h, hb, hp = _layer_norm(z, ln_g[layer, 0], ln_b[layer, 0])
        h, hb, hp = _moe_layer(
            h, hb, hp, moe_router_w[layer], moe_router_b[layer],
            moe_w_up[layer, :, :, :H].astype(bf16), moe_w_up[layer, :, :, H:].astype(bf16),
            moe_w_down[layer].astype(bf16),
            moe_shared_up[layer, :, :H].astype(bf16), moe_shared_up[layer, :, H:].astype(bf16),
            moe_shared_down[layer].astype(bf16), ln_g[layer, 1], ln_b[layer, 1], alpha)
    return h.reshape(Bsz, T, D)
```

```python
import functools

import jax
import jax.numpy as jnp
from jax import lax
from jax.experimental import pallas as pl
from jax.experimental.pallas import tpu as pltpu

HEAD_DIM = 128
WINDOW = 128
CONV_WIDTH = 4
LRU_C = 8.0
TOP_K = 8
N_GROUPS = 8
TOPK_GROUPS = 4
ROUTED_SCALE = 2.5
LN_EPS = 1e-5

V7X_VMEM_BYTES = 64 * 1024 * 1024
VMEM_LIMIT = V7X_VMEM_BYTES - 8 * 1024 * 1024
EXPERT_ROWS = 256

f32 = jnp.float32
bf16 = jnp.bfloat16


def _cparams(*sem):
    return pltpu.CompilerParams(dimension_semantics=sem, vmem_limit_bytes=VMEM_LIMIT)


def _pick(n, pref):
    t = min(pref, n)
    while n % t:
        t //= 2
    return t


def _mm_kernel(x_ref, w_ref, o_ref):
    o_ref[...] = jnp.dot(x_ref[...], w_ref[...], preferred_element_type=f32).astype(o_ref.dtype)


def _matmul(x, w, out_dtype):
    M, K = x.shape
    N = w.shape[1]
    tm, tn = _pick(M, 1024), _pick(N, 512)
    return pl.pallas_call(
        _mm_kernel,
        grid=(M // tm, N // tn),
        in_specs=[pl.BlockSpec((tm, K), lambda i, j: (i, 0)),
                  pl.BlockSpec((K, tn), lambda i, j: (0, j))],
        out_specs=pl.BlockSpec((tm, tn), lambda i, j: (i, j)),
        out_shape=jax.ShapeDtypeStruct((M, N), out_dtype),
        compiler_params=_cparams("parallel", "arbitrary"),
        name="matmul",
    )(x, w)


def _mm_res_kernel(x_ref, w_ref, r_ref, o_ref, *, alpha):
    o_ref[...] = alpha * r_ref[...] + jnp.dot(x_ref[...], w_ref[...], preferred_element_type=f32)


def _matmul_residual(x, w, res, alpha):
    M, K = x.shape
    N = w.shape[1]
    tm, tn = _pick(M, 1024), _pick(N, 512)
    return pl.pallas_call(
        functools.partial(_mm_res_kernel, alpha=alpha),
        grid=(M // tm, N // tn),
        in_specs=[pl.BlockSpec((tm, K), lambda i, j: (i, 0)),
                  pl.BlockSpec((K, tn), lambda i, j: (0, j)),
                  pl.BlockSpec((tm, tn), lambda i, j: (i, j))],
        out_specs=pl.BlockSpec((tm, tn), lambda i, j: (i, j)),
        out_shape=jax.ShapeDtypeStruct((M, N), f32),
        compiler_params=_cparams("parallel", "arbitrary"),
        name="matmul_residual",
    )(x, w, res)


def _gelu_tanh(x):
    return 0.5 * x * (1.0 + jnp.tanh(0.7978845608028654 * (x + 0.044715 * (x * x * x))))


def _lru_in_kernel(x_ref, wx_ref, wy_ref, bx_ref, by_ref, xr_ref, gb_ref):
    x = x_ref[...]
    xr_ref[...] = jnp.dot(x, wx_ref[...], preferred_element_type=f32) + bx_ref[...]
    gb_ref[...] = _gelu_tanh(jnp.dot(x, wy_ref[...], preferred_element_type=f32) + by_ref[...])


def _lru_in(x, wx, wy, bx, by):
    M, K = x.shape
    W = wx.shape[1]
    tm, tn = _pick(M, 1024), _pick(W, 256)
    wspec = pl.BlockSpec((K, tn), lambda i, j: (0, j))
    bspec = pl.BlockSpec((1, tn), lambda i, j: (0, j))
    ospec = pl.BlockSpec((tm, tn), lambda i, j: (i, j))
    return pl.pallas_call(
        _lru_in_kernel,
        grid=(M // tm, W // tn),
        in_specs=[pl.BlockSpec((tm, K), lambda i, j: (i, 0)), wspec, wspec, bspec, bspec],
        out_specs=[ospec, ospec],
        out_shape=[jax.ShapeDtypeStruct((M, W), f32)] * 2,
        compiler_params=_cparams("parallel", "arbitrary"),
        name="lru_in",
    )(x, wx, wy, bx, by)


def _layer_norm_rows(z, g, b):
    mu = jnp.mean(z, axis=-1, keepdims=True)
    zc = z - mu
    var = jnp.mean(zc * zc, axis=-1, keepdims=True)
    return zc * lax.rsqrt(var + LN_EPS) * g + b


def _pack_pairs(x):
    half = x.shape[1] // 2
    bits = lax.bitcast_convert_type(x.astype(bf16).astype(f32), jnp.uint32)
    return (bits[:, :half] >> 16) | bits[:, half:]


def _store_hidden(h, h_ref, hb_ref, hp_ref):
    h_ref[...] = h
    hb_ref[...] = h.astype(bf16)
    hp_ref[...] = _pack_pairs(h)


def _ln_kernel(z_ref, g_ref, b_ref, h_ref, hb_ref, hp_ref):
    _store_hidden(_layer_norm_rows(z_ref[...], g_ref[...], b_ref[...]), h_ref, hb_ref, hp_ref)


def _hidden_out(M, D, tm):
    specs = [pl.BlockSpec((tm, D), lambda i: (i, 0)),
             pl.BlockSpec((tm, D), lambda i: (i, 0)),
             pl.BlockSpec((tm, D // 2), lambda i: (i, 0))]
    shapes = [jax.ShapeDtypeStruct((M, D), f32),
              jax.ShapeDtypeStruct((M, D), bf16),
              jax.ShapeDtypeStruct((M, D // 2), jnp.uint32)]
    return specs, shapes


def _layer_norm(z, g, b):
    M, D = z.shape
    tm = _pick(M, 256)
    specs, shapes = _hidden_out(M, D, tm)
    vspec = pl.BlockSpec((1, D), lambda i: (0, 0))
    return pl.pallas_call(
        _ln_kernel,
        grid=(M // tm,),
        in_specs=[pl.BlockSpec((tm, D), lambda i: (i, 0)), vspec, vspec],
        out_specs=specs,
        out_shape=shapes,
        compiler_params=_cparams("parallel"),
        name="layer_norm",
    )(z, g.reshape(1, D), b.reshape(1, D))


def _sigmoid(x):
    return 0.5 + 0.5 * jnp.tanh(0.5 * x)


def _softplus(x):
    return jnp.maximum(x, 0.0) + jnp.log1p(jnp.exp(-jnp.abs(x)))


def _lru_core_kernel(xr_ref, gb_ref, cw_ref, cb_ref, gxw_ref, gxb_ref, gaw_ref, gab_ref, ap_ref,
                     y_ref, tail_sc, h_sc, a_sc, b_sc, hs_sc, *, heads, blk):
    tb = xr_ref.shape[0]

    @pl.when(pl.program_id(1) == 0)
    def _():
        tail_sc[...] = jnp.zeros_like(tail_sc)
        h_sc[...] = jnp.zeros_like(h_sc)

    xr = xr_ref[...]
    ext = jnp.concatenate([tail_sc[...], xr], axis=0)
    tail_sc[...] = xr[tb - 8:, :]
    cw = cw_ref[...]
    xc = cb_ref[...]
    for j in range(CONV_WIDTH):
        off = 8 - (CONV_WIDTH - 1) + j
        xc = xc + cw[j:j + 1, :] * ext[off:off + tb, :]
    xcb = xc.astype(bf16)

    for hd in range(heads):
        sl = slice(hd * blk, (hd + 1) * blk)
        xh = xcb[:, sl]
        gate_x = _sigmoid(jnp.dot(xh, gxw_ref[hd], preferred_element_type=f32) + gxb_ref[:, sl])
        gate_a = _sigmoid(jnp.dot(xh, gaw_ref[hd], preferred_element_type=f32) + gab_ref[:, sl])
        a = jnp.exp((-LRU_C * _softplus(-ap_ref[:, sl])) * gate_a)
        a_sc[:, sl] = a
        b_sc[:, sl] = xc[:, sl] * gate_x * jnp.sqrt(1.0 - a * a)

    def body(i, h):
        base = pl.multiple_of(i * 8, 8)
        a8 = a_sc[pl.ds(base, 8), :]
        b8 = b_sc[pl.ds(base, 8), :]
        rows = []
        for r in range(8):
            h = a8[r:r + 1, :] * h + b8[r:r + 1, :]
            rows.append(h)
        hs_sc[pl.ds(base, 8), :] = jnp.concatenate(rows, axis=0)
        return h

    h_sc[0:1, :] = lax.fori_loop(0, tb // 8, body, h_sc[0:1, :])
    y_ref[...] = (hs_sc[...] * gb_ref[...]).astype(y_ref.dtype)


def _lru_core(xr, gb, conv_w, conv_b, gx_w, gx_b, ga_w, ga_b, a_param):
    T, W = xr.shape
    nh, blk = gx_w.shape[0], gx_w.shape[1]
    wb = _pick(W, 1024)
    heads = wb // blk
    tb = _pick(T, 256)
    xspec = pl.BlockSpec((tb, wb), lambda c, t: (t, c))
    vspec = pl.BlockSpec((1, wb), lambda c, t: (0, c))
    gspec = pl.BlockSpec((heads, blk, blk), lambda c, t: (c, 0, 0))
    return pl.pallas_call(
        functools.partial(_lru_core_kernel, heads=heads, blk=blk),
        grid=(W // wb, T // tb),
        in_specs=[xspec, xspec, pl.BlockSpec((CONV_WIDTH, wb), lambda c, t: (0, c)), vspec,
                  gspec, vspec, gspec, vspec, vspec],
        out_specs=xspec,
        out_shape=jax.ShapeDtypeStruct((T, W), bf16),
        scratch_shapes=[pltpu.VMEM((8, wb), f32), pltpu.VMEM((8, wb), f32),
                        pltpu.VMEM((tb, wb), f32), pltpu.VMEM((tb, wb), f32), pltpu.VMEM((tb, wb), f32)],
        compiler_params=_cparams("parallel", "arbitrary"),
        name="lru_core",
    )(xr, gb, conv_w, conv_b.reshape(1, W), gx_w.astype(bf16), gx_b.reshape(1, W),
      ga_w.astype(bf16), ga_b.reshape(1, W), a_param.reshape(1, W))


def _attn_kernel(q_ref, kp_ref, kc_ref, vp_ref, vc_ref, bias_ref, sink_ref, o_ref, *, group, n_kv):
    blk = q_ref.shape[0]
    hd = HEAD_DIM
    col = lax.broadcasted_iota(jnp.int32, (group * blk, 2 * blk), 1)
    no_prev = jnp.logical_and(pl.program_id(0) == 0, col < blk)
    for kv in range(n_kv):
        q = jnp.concatenate([q_ref[:, (kv * group + g) * hd:(kv * group + g + 1) * hd] for g in range(group)],
                            axis=0)
        ksl = slice(kv * hd, (kv + 1) * hd)
        k = jnp.concatenate([kp_ref[:, ksl], kc_ref[:, ksl]], axis=0)
        v = jnp.concatenate([vp_ref[:, ksl], vc_ref[:, ksl]], axis=0)
        s = lax.dot_general(q, k, (((1,), (1,)), ((), ())), preferred_element_type=f32) * (hd ** -0.5)
        logits = jnp.where(no_prev, -jnp.inf, s + bias_ref[kv])
        sink = sink_ref[kv]
        m = jnp.maximum(jnp.max(logits, axis=-1, keepdims=True), sink)
        p = jnp.exp(logits - m)
        probs = p / (jnp.sum(p, axis=-1, keepdims=True) + jnp.exp(sink - m))
        o = jnp.dot(probs.astype(v.dtype), v, preferred_element_type=f32)
        for g in range(group):
            o_ref[:, (kv * group + g) * hd:(kv * group + g + 1) * hd] = (
                o[g * blk:(g + 1) * blk, :].astype(o_ref.dtype))


def _attention(qkv, sinks, n_q, n_kv):
    T = qkv.shape[0]
    group = n_q // n_kv
    blk = WINDOW
    nb = T // blk
    i = jnp.arange(blk)[:, None]
    j = jnp.arange(2 * blk)[None, :]
    dist = blk + i - j
    in_band = (dist >= 0) & (dist < WINDOW)
    slopes = jnp.exp2(-8.0 * jnp.arange(1, n_q + 1, dtype=f32) / n_q)
    bias = jnp.where(in_band[None], -slopes[:, None, None] * dist.astype(f32)[None], -jnp.inf)
    bias = bias.reshape(n_kv, group * blk, 2 * blk)
    sink = jnp.repeat(sinks.astype(f32).reshape(n_kv, group), blk, axis=1).reshape(n_kv, group * blk, 1)

    assert group * n_kv == n_q
    prev = lambda n: jnp.maximum(n - 1, 0)
    kvw = n_kv * HEAD_DIM
    return pl.pallas_call(
        functools.partial(_attn_kernel, group=group, n_kv=n_kv),
        grid=(nb,),
        in_specs=[pl.BlockSpec((blk, n_q * HEAD_DIM), lambda n: (n, 0)),
                  pl.BlockSpec((blk, kvw), lambda n: (prev(n), group)),
                  pl.BlockSpec((blk, kvw), lambda n: (n, group)),
                  pl.BlockSpec((blk, kvw), lambda n: (prev(n), group + 1)),
                  pl.BlockSpec((blk, kvw), lambda n: (n, group + 1)),
                  pl.BlockSpec((n_kv, group * blk, 2 * blk), lambda n: (0, 0, 0)),
                  pl.BlockSpec((n_kv, group * blk, 1), lambda n: (0, 0, 0))],
        out_specs=pl.BlockSpec((blk, n_q * HEAD_DIM), lambda n: (n, 0)),
        out_shape=jax.ShapeDtypeStruct((T, n_q * HEAD_DIM), bf16),
        compiler_params=_cparams("arbitrary"),
        name="swa_attention",
    )(qkv, qkv, qkv, qkv, qkv, bias, sink)


def _first_index_of_max(x, rows):
    m = jnp.max(x, axis=0, keepdims=True)
    idx = jnp.min(jnp.where(x == m, rows, x.shape[0]), axis=0, keepdims=True)
    return m, idx


def _router_kernel(h_ref, rwt_ref, rb_ref, idx_ref, gate_ref, cnt_ref, carry_sc, *, n_exp):
    tm = h_ref.shape[0]
    gsz = n_exp // N_GROUPS

    @pl.when(pl.program_id(0) == 0)
    def _():
        carry_sc[...] = jnp.zeros_like(carry_sc)

    logits = lax.dot_general(rwt_ref[...], h_ref[...], (((1,), (1,)), ((), ())),
                             precision=lax.Precision.HIGHEST, preferred_element_type=f32)
    scores = jax.nn.sigmoid(logits)
    sel = scores + rb_ref[...]

    grow = lax.broadcasted_iota(jnp.int32, (gsz, tm), 0)
    gscores = []
    for g in range(N_GROUPS):
        sg = sel[g * gsz:(g + 1) * gsz, :]
        m1, i1 = _first_index_of_max(sg, grow)
        m2 = jnp.max(jnp.where(grow == i1, -jnp.inf, sg), axis=0, keepdims=True)
        gscores.append(m1 + m2)
    gs = jnp.concatenate(gscores, axis=0)
    g_iota = lax.broadcasted_iota(jnp.int32, (N_GROUPS, tm), 0)
    gpick = jnp.zeros((N_GROUPS, tm), jnp.bool_)
    for _ in range(TOPK_GROUPS):
        _, gi = _first_index_of_max(jnp.where(gpick, -jnp.inf, gs), g_iota)
        gpick = jnp.logical_or(gpick, g_iota == gi)

    erow = lax.broadcasted_iota(jnp.int32, (n_exp, tm), 0)
    emask = jnp.concatenate(
        [jnp.broadcast_to(gpick[g:g + 1, :], (gsz, tm)) for g in range(N_GROUPS)], axis=0)
    cand = jnp.where(emask, sel, -jnp.inf)
    idxs, gates = [], []
    taken = jnp.zeros((n_exp, tm), jnp.bool_)
    for _ in range(TOP_K):
        _, ei = _first_index_of_max(jnp.where(taken, -jnp.inf, cand), erow)
        hit = erow == ei
        taken = jnp.logical_or(taken, hit)
        idxs.append(ei)
        gates.append(jnp.sum(jnp.where(hit, scores, 0.0), axis=0, keepdims=True))
    gate = jnp.concatenate(gates, axis=0)
    idx_ref[...] = jnp.concatenate(idxs, axis=0)
    gate_ref[...] = gate / jnp.sum(gate, axis=0, keepdims=True) * ROUTED_SCALE

    total = carry_sc[...] + jnp.sum(jnp.where(taken, 1.0, 0.0), axis=1, keepdims=True)
    carry_sc[...] = total
    cnt_ref[...] = total


def _router(h, router_w, router_b):
    N, D = h.shape
    E = router_w.shape[1]
    tm = _pick(N, 512)
    kspec = pl.BlockSpec((TOP_K, tm), lambda i: (0, i))
    idx, gate, cnt = pl.pallas_call(
        functools.partial(_router_kernel, n_exp=E),
        grid=(N // tm,),
        in_specs=[pl.BlockSpec((tm, D), lambda i: (i, 0)),
                  pl.BlockSpec((E, D), lambda i: (0, 0)),
                  pl.BlockSpec((E, 1), lambda i: (0, 0))],
        out_specs=[kspec, kspec, pl.BlockSpec((E, 128), lambda i: (0, 0))],
        out_shape=[jax.ShapeDtypeStruct((TOP_K, N), jnp.int32),
                   jax.ShapeDtypeStruct((TOP_K, N), f32),
                   jax.ShapeDtypeStruct((E, 128), f32)],
        scratch_shapes=[pltpu.VMEM((E, 128), f32)],
        compiler_params=_cparams("arbitrary"),
        name="moe_router",
    )(h, router_w.T, router_b.reshape(E, 1))
    return idx, gate, cnt[:, 0].astype(jnp.int32)


def _routing_tables(idx, counts, n_tok):
    B = EXPERT_ROWS
    E = counts.shape[0]
    A = TOP_K * n_tok
    assert A % B == 0
    n_blocks = A // B + E
    stride = A + B
    blocks = (counts + B - 1) // B
    bend = jnp.cumsum(blocks)
    n_used = bend[-1:].astype(jnp.int32)
    block_ids = jnp.arange(n_blocks, dtype=jnp.int32)
    block_expert = jnp.minimum(jnp.sum(bend[None, :] <= block_ids[:, None], axis=1), E - 1).astype(jnp.int32)

    assign = jnp.arange(A, dtype=jnp.int32)
    real = idx.reshape(-1) * stride + assign
    pad_count = blocks * B - counts
    j = jnp.arange(B, dtype=jnp.int32)[None, :]
    e = jnp.arange(E, dtype=jnp.int32)[:, None]
    pad = jnp.where(j < pad_count[:, None], e * stride + A + j, E * stride)
    keys = lax.sort(jnp.concatenate([real, pad.reshape(-1).astype(jnp.int32)]))
    low = keys % stride
    is_real = jnp.logical_and(low < A, keys < E * stride)
    pos = jnp.arange(n_blocks * B, dtype=jnp.int32)
    src = jnp.where(is_real, low % n_tok, 0)
    dst = jnp.where(is_real, low, A + (pos // B % 2) * B + pos % B)
    tables = jnp.stack([src.reshape(n_blocks, B), dst.reshape(n_blocks, B)], axis=1)
    return tables, block_expert, n_used


def _unpack_pairs(u):
    lo = lax.bitcast_convert_type(u << 16, f32)
    hi = lax.bitcast_convert_type(u & jnp.uint32(0xFFFF0000), f32)
    return lo, hi


def _silu(x):
    return x * jax.nn.sigmoid(x)


def _expert_kernel(be_ref, nu_ref, lay_ref, tab_hbm, hp_hbm, wu_ref, wd_ref, z_hbm,
                   tab_sm, xbuf, ybuf, wu_sc, wd_sc, sem_t, gsem, ssem):
    del lay_ref
    i = pl.program_id(0)
    last = pl.num_programs(0) - 1
    n_used = nu_ref[0]
    B, half = xbuf.shape[1], xbuf.shape[2]

    def table_copy(j):
        return pltpu.make_async_copy(tab_hbm.at[j], tab_sm.at[j % 3], sem_t.at[j % 3])

    def gather_copy(j, r):
        return pltpu.make_async_copy(hp_hbm.at[pl.ds(tab_sm[j % 3, 0, r], 1)],
                                     xbuf.at[j % 2, pl.ds(r, 1)], gsem.at[j % 2])

    def scatter_copy(j, r):
        return pltpu.make_async_copy(ybuf.at[j % 2, pl.ds(r, 1)],
                                     z_hbm.at[pl.ds(tab_sm[j % 3, 1, r], 1)], ssem.at[j % 2])

    def start_rows(copy, j):
        for r in range(B):
            copy(j, r).start()

    def wait_gather(j):
        pltpu.make_async_copy(hp_hbm.at[pl.ds(0, B)], xbuf.at[j % 2], gsem.at[j % 2]).wait()

    def wait_scatter(j):
        pltpu.make_async_copy(ybuf.at[j % 2], z_hbm.at[pl.ds(0, B)], ssem.at[j % 2]).wait()

    @pl.when(i == 0)
    def _():
        table_copy(0).start()
        table_copy(0).wait()

        @pl.when(1 < n_used)
        def _():
            table_copy(1).start()

        start_rows(gather_copy, 0)

    @pl.when(i + 1 < n_used)
    def _():
        table_copy(i + 1).wait()
        start_rows(gather_copy, i + 1)

    @pl.when(i + 2 < n_used)
    def _():
        table_copy(i + 2).start()

    @pl.when(jnp.logical_and(i >= 2, i - 2 < n_used))
    def _():
        wait_scatter(i - 2)

    @pl.when(i < n_used)
    def _():
        @pl.when(jnp.logical_or(i == 0, be_ref[i] != be_ref[jnp.maximum(i - 1, 0)]))
        def _():
            wu_sc[...] = wu_ref[...].astype(bf16)
            wd_sc[...] = wd_ref[...].astype(bf16)

        wait_gather(i)
        lo, hi = _unpack_pairs(xbuf[i % 2])
        hu = (jnp.dot(lo.astype(bf16), wu_sc[:half, :], preferred_element_type=f32)
              + jnp.dot(hi.astype(bf16), wu_sc[half:, :], preferred_element_type=f32))
        hid = wd_sc.shape[0]
        act = (_silu(hu[:, :hid]) * hu[:, hid:]).astype(bf16)
        ybuf[i % 2] = _pack_pairs(jnp.dot(act, wd_sc[...], preferred_element_type=f32))
        start_rows(scatter_copy, i)

    @pl.when(i == last)
    def _():
        @pl.when(jnp.logical_and(i >= 1, i - 1 < n_used))
        def _():
            wait_scatter(i - 1)

        @pl.when(i < n_used)
        def _():
            wait_scatter(i)


def _experts(hp, tables, block_expert, n_used, w_up, w_down, layer):
    N, D2 = hp.shape
    _, E, D, H2 = w_up.shape
    H = H2 // 2
    B = EXPERT_ROWS
    n_blocks = tables.shape[0]
    return pl.pallas_call(
        _expert_kernel,
        grid_spec=pltpu.PrefetchScalarGridSpec(
            num_scalar_prefetch=3,
            grid=(n_blocks,),
            in_specs=[pl.BlockSpec(memory_space=pl.ANY),
                      pl.BlockSpec(memory_space=pl.ANY),
                      pl.BlockSpec((None, None, D, H2), lambda i, be, nu, lay: (lay[0], be[i], 0, 0)),
                      pl.BlockSpec((None, None, H, D), lambda i, be, nu, lay: (lay[0], be[i], 0, 0))],
            out_specs=pl.BlockSpec(memory_space=pl.ANY),
            scratch_shapes=[pltpu.SMEM((3, 2, B), jnp.int32),
                            pltpu.VMEM((2, B, D2), jnp.uint32), pltpu.VMEM((2, B, D2), jnp.uint32),
                            pltpu.VMEM((D, H2), bf16), pltpu.VMEM((H, D), bf16),
                            pltpu.SemaphoreType.DMA((3,)), pltpu.SemaphoreType.DMA((2,)),
                            pltpu.SemaphoreType.DMA((2,))],
        ),
        out_shape=jax.ShapeDtypeStruct((TOP_K * N + 2 * B, D2), jnp.uint32),
        compiler_params=_cparams("arbitrary"),
        name="moe_experts",
    )(block_expert, n_used, jnp.full((1,), layer, jnp.int32), tables, hp, w_up, w_down)


def _combine_kernel(*refs, alpha):
    z_refs = refs[:TOP_K]
    gate_ref, h_ref, hb_ref, sg_ref, sv_ref, sd_ref, g_ref, b_ref, ho_ref, hbo_ref, hpo_ref = refs[TOP_K:]
    hb = hb_ref[...]
    act = (_silu(jnp.dot(hb, sg_ref[...], preferred_element_type=f32))
           * jnp.dot(hb, sv_ref[...], preferred_element_type=f32)).astype(bf16)
    z = alpha * h_ref[...] + jnp.dot(act, sd_ref[...], preferred_element_type=f32)
    half = z.shape[1] // 2
    z_lo, z_hi = z[:, :half], z[:, half:]
    gate = gate_ref[...]
    for k in range(TOP_K):
        lo, hi = _unpack_pairs(z_refs[k][...])
        g = gate[:, k:k + 1]
        z_lo = z_lo + lo * g
        z_hi = z_hi + hi * g
    z = jnp.concatenate([z_lo, z_hi], axis=1)
    _store_hidden(_layer_norm_rows(z, g_ref[...], b_ref[...]), ho_ref, hbo_ref, hpo_ref)


def _combine(zs, gate_t, h, hb, s_gate, s_val, s_down, ln_g, ln_b, alpha):
    N, D = h.shape
    H = s_gate.shape[1]
    tm = _pick(N, 128)
    nt = N // tm
    specs, shapes = _hidden_out(N, D, tm)
    row = pl.BlockSpec((tm, D), lambda i: (i, 0))
    full = lambda a, b: pl.BlockSpec((a, b), lambda i: (0, 0))
    z_specs = [pl.BlockSpec((tm, D // 2), lambda i, k=k: (k * nt + i, 0)) for k in range(TOP_K)]
    return pl.pallas_call(
        functools.partial(_combine_kernel, alpha=alpha),
        grid=(nt,),
        in_specs=z_specs + [pl.BlockSpec((tm, TOP_K), lambda i: (i, 0)),
                            row, row, full(D, H), full(D, H), full(H, D), full(1, D), full(1, D)],
        out_specs=specs,
        out_shape=shapes,
        compiler_params=_cparams("parallel"),
        name="moe_combine",
    )(*([zs] * TOP_K), gate_t, h, hb, s_gate, s_val, s_down, ln_g.reshape(1, D), ln_b.reshape(1, D))


def _moe_layer(h, hb, hp, router_w, router_b, w_up, w_down, layer, s_gate, s_val, s_down,
               ln_g, ln_b, alpha):
    idx, gate, counts = _router(h, router_w, router_b)
    tables, block_expert, n_used = _routing_tables(idx, counts, h.shape[0])
    zs = _experts(hp, tables, block_expert, n_used, w_up, w_down, layer)
    return _combine(zs, gate.T, h, hb, s_gate, s_val, s_down, ln_g, ln_b, alpha)


def kernel(x, lru_w_in, lru_b_in, lru_conv_w, lru_conv_b, lru_gx_w, lru_gx_b, lru_ga_w, lru_ga_b,
           lru_a_param, lru_w_out, attn_w_qkv, attn_sinks, attn_w_o, ln_g, ln_b,
           moe_router_w, moe_router_b, moe_w_up, moe_w_down, moe_shared_up, moe_shared_down):
    Bsz, T, D = x.shape
    depth = ln_g.shape[0]
    alpha = (2.0 * depth) ** 0.25
    W = lru_w_in.shape[2] // 2
    H = moe_w_down.shape[2]
    n_q = D // HEAD_DIM
    n_kv = (attn_w_qkv.shape[2] // HEAD_DIM - n_q) // 2
    N = Bsz * T
    assert Bsz == 1, "sequence mixers below assume one sequence"

    h = x.reshape(N, D)
    hb = h.astype(bf16)
    hp = None
    for layer in range(depth):
        j = layer // 2
        if layer % 2 == 0:
            w_in = lru_w_in[j].astype(bf16)
            xr, gb = _lru_in(hb, w_in[:, :W], w_in[:, W:],
                             lru_b_in[j, :W].reshape(1, W), lru_b_in[j, W:].reshape(1, W))
            y = _lru_core(xr, gb, lru_conv_w[j], lru_conv_b[j], lru_gx_w[j], lru_gx_b[j].reshape(-1),
                          lru_ga_w[j], lru_ga_b[j].reshape(-1), lru_a_param[j])
            z = _matmul_residual(y, lru_w_out[j].astype(bf16), h, alpha)
        else:
            qkv = _matmul(hb, attn_w_qkv[j].astype(bf16), bf16)
            o = _attention(qkv, attn_sinks[j], n_q, n_kv)
            z = _matmul_residual(o, attn_w_o[j].astype(bf16), h, alpha)
        h, hb, hp = _layer_norm(z, ln_g[layer, 0], ln_b[layer, 0])
        h, hb, hp = _moe_layer(
            h, hb, hp, moe_router_w[layer], moe_router_b[layer], moe_w_up, moe_w_down, layer,
            moe_shared_up[layer, :, :H].astype(bf16), moe_shared_up[layer, :, H:].astype(bf16),
            moe_shared_down[layer].astype(bf16), ln_g[layer, 1], ln_b[layer, 1], alpha)
    return h.reshape(Bsz, T, D)
```

```python
import functools

import jax
import jax.numpy as jnp
from jax import lax
from jax.experimental import pallas as pl
from jax.experimental.pallas import tpu as pltpu

HEAD_DIM = 128
WINDOW = 128
CONV_WIDTH = 4
LRU_C = 8.0
TOP_K = 8
N_GROUPS = 8
TOPK_GROUPS = 4
ROUTED_SCALE = 2.5
LN_EPS = 1e-5

V7X_VMEM_BYTES = 64 * 1024 * 1024
VMEM_LIMIT = V7X_VMEM_BYTES - 8 * 1024 * 1024
EXPERT_ROWS = 256

f32 = jnp.float32
bf16 = jnp.bfloat16


def _cparams(*sem):
    return pltpu.CompilerParams(dimension_semantics=sem, vmem_limit_bytes=VMEM_LIMIT)


def _pick(n, pref):
    t = min(pref, n)
    while n % t:
        t //= 2
    return t


def _mm_kernel(x_ref, w_ref, o_ref):
    o_ref[...] = jnp.dot(x_ref[...], w_ref[...], preferred_element_type=f32).astype(o_ref.dtype)


def _matmul(x, w, out_dtype):
    M, K = x.shape
    N = w.shape[1]
    tm, tn = _pick(M, 1024), _pick(N, 512)
    return pl.pallas_call(
        _mm_kernel,
        grid=(M // tm, N // tn),
        in_specs=[pl.BlockSpec((tm, K), lambda i, j: (i, 0)),
                  pl.BlockSpec((K, tn), lambda i, j: (0, j))],
        out_specs=pl.BlockSpec((tm, tn), lambda i, j: (i, j)),
        out_shape=jax.ShapeDtypeStruct((M, N), out_dtype),
        compiler_params=_cparams("parallel", "arbitrary"),
        name="matmul",
    )(x, w)


def _mm_res_kernel(x_ref, w_ref, r_ref, o_ref, *, alpha):
    o_ref[...] = alpha * r_ref[...] + jnp.dot(x_ref[...], w_ref[...], preferred_element_type=f32)


def _matmul_residual(x, w, res, alpha):
    M, K = x.shape
    N = w.shape[1]
    tm, tn = _pick(M, 1024), _pick(N, 512)
    return pl.pallas_call(
        functools.partial(_mm_res_kernel, alpha=alpha),
        grid=(M // tm, N // tn),
        in_specs=[pl.BlockSpec((tm, K), lambda i, j: (i, 0)),
                  pl.BlockSpec((K, tn), lambda i, j: (0, j)),
                  pl.BlockSpec((tm, tn), lambda i, j: (i, j))],
        out_specs=pl.BlockSpec((tm, tn), lambda i, j: (i, j)),
        out_shape=jax.ShapeDtypeStruct((M, N), f32),
        compiler_params=_cparams("parallel", "arbitrary"),
        name="matmul_residual",
    )(x, w, res)


def _gelu_tanh(x):
    return 0.5 * x * (1.0 + jnp.tanh(0.7978845608028654 * (x + 0.044715 * (x * x * x))))


def _lru_in_kernel(x_ref, wx_ref, wy_ref, bx_ref, by_ref, xr_ref, gb_ref):
    x = x_ref[...]
    xr_ref[...] = jnp.dot(x, wx_ref[...], preferred_element_type=f32) + bx_ref[...]
    gb_ref[...] = _gelu_tanh(jnp.dot(x, wy_ref[...], preferred_element_type=f32) + by_ref[...])


def _lru_in(x, wx, wy, bx, by):
    M, K = x.shape
    W = wx.shape[1]
    tm, tn = _pick(M, 1024), _pick(W, 256)
    wspec = pl.BlockSpec((K, tn), lambda i, j: (0, j))
    bspec = pl.BlockSpec((1, tn), lambda i, j: (0, j))
    ospec = pl.BlockSpec((tm, tn), lambda i, j: (i, j))
    return pl.pallas_call(
        _lru_in_kernel,
        grid=(M // tm, W // tn),
        in_specs=[pl.BlockSpec((tm, K), lambda i, j: (i, 0)), wspec, wspec, bspec, bspec],
        out_specs=[ospec, ospec],
        out_shape=[jax.ShapeDtypeStruct((M, W), f32)] * 2,
        compiler_params=_cparams("parallel", "arbitrary"),
        name="lru_in",
    )(x, wx, wy, bx, by)


def _layer_norm_rows(z, g, b):
    mu = jnp.mean(z, axis=-1, keepdims=True)
    zc = z - mu
    var = jnp.mean(zc * zc, axis=-1, keepdims=True)
    return zc * lax.rsqrt(var + LN_EPS) * g + b


def _pack_pairs(x):
    half = x.shape[1] // 2
    bits = lax.bitcast_convert_type(x.astype(bf16).astype(f32), jnp.uint32)
    return (bits[:, :half] >> 16) | bits[:, half:]


def _store_hidden(h, h_ref, hb_ref, hp_ref):
    h_ref[...] = h
    hb_ref[...] = h.astype(bf16)
    hp_ref[...] = _pack_pairs(h)


def _ln_kernel(z_ref, g_ref, b_ref, h_ref, hb_ref, hp_ref):
    _store_hidden(_layer_norm_rows(z_ref[...], g_ref[...], b_ref[...]), h_ref, hb_ref, hp_ref)


def _hidden_out(M, D, tm):
    specs = [pl.BlockSpec((tm, D), lambda i: (i, 0)),
             pl.BlockSpec((tm, D), lambda i: (i, 0)),
             pl.BlockSpec((tm, D // 2), lambda i: (i, 0))]
    shapes = [jax.ShapeDtypeStruct((M, D), f32),
              jax.ShapeDtypeStruct((M, D), bf16),
              jax.ShapeDtypeStruct((M, D // 2), jnp.uint32)]
    return specs, shapes


def _layer_norm(z, g, b):
    M, D = z.shape
    tm = _pick(M, 256)
    specs, shapes = _hidden_out(M, D, tm)
    vspec = pl.BlockSpec((1, D), lambda i: (0, 0))
    return pl.pallas_call(
        _ln_kernel,
        grid=(M // tm,),
        in_specs=[pl.BlockSpec((tm, D), lambda i: (i, 0)), vspec, vspec],
        out_specs=specs,
        out_shape=shapes,
        compiler_params=_cparams("parallel"),
        name="layer_norm",
    )(z, g.reshape(1, D), b.reshape(1, D))


def _sigmoid(x):
    return 0.5 + 0.5 * jnp.tanh(0.5 * x)


def _softplus(x):
    return jnp.maximum(x, 0.0) + jnp.log1p(jnp.exp(-jnp.abs(x)))


def _lru_core_kernel(xr_ref, gb_ref, cw_ref, cb_ref, gxw_ref, gxb_ref, gaw_ref, gab_ref, ap_ref,
                     y_ref, tail_sc, h_sc, a_sc, b_sc, hs_sc, *, heads, blk):
    tb = xr_ref.shape[0]

    @pl.when(pl.program_id(1) == 0)
    def _():
        tail_sc[...] = jnp.zeros_like(tail_sc)
        h_sc[...] = jnp.zeros_like(h_sc)

    xr = xr_ref[...]
    ext = jnp.concatenate([tail_sc[...], xr], axis=0)
    tail_sc[...] = xr[tb - 8:, :]
    cw = cw_ref[...]
    xc = cb_ref[...]
    for j in range(CONV_WIDTH):
        off = 8 - (CONV_WIDTH - 1) + j
        xc = xc + cw[j:j + 1, :] * ext[off:off + tb, :]
    xcb = xc.astype(bf16)

    for hd in range(heads):
        sl = slice(hd * blk, (hd + 1) * blk)
        xh = xcb[:, sl]
        gate_x = _sigmoid(jnp.dot(xh, gxw_ref[hd], preferred_element_type=f32) + gxb_ref[:, sl])
        gate_a = _sigmoid(jnp.dot(xh, gaw_ref[hd], preferred_element_type=f32) + gab_ref[:, sl])
        a = jnp.exp((-LRU_C * _softplus(-ap_ref[:, sl])) * gate_a)
        a_sc[:, sl] = a
        b_sc[:, sl] = xc[:, sl] * gate_x * jnp.sqrt(1.0 - a * a)

    def body(i, h):
        base = pl.multiple_of(i * 8, 8)
        a8 = a_sc[pl.ds(base, 8), :]
        b8 = b_sc[pl.ds(base, 8), :]
        rows = []
        for r in range(8):
            h = a8[r:r + 1, :] * h + b8[r:r + 1, :]
            rows.append(h)
        hs_sc[pl.ds(base, 8), :] = jnp.concatenate(rows, axis=0)
        return h

    h_sc[0:1, :] = lax.fori_loop(0, tb // 8, body, h_sc[0:1, :])
    y_ref[...] = (hs_sc[...] * gb_ref[...]).astype(y_ref.dtype)


def _lru_core(xr, gb, conv_w, conv_b, gx_w, gx_b, ga_w, ga_b, a_param):
    T, W = xr.shape
    nh, blk = gx_w.shape[0], gx_w.shape[1]
    wb = _pick(W, 1024)
    heads = wb // blk
    tb = _pick(T, 256)
    xspec = pl.BlockSpec((tb, wb), lambda c, t: (t, c))
    vspec = pl.BlockSpec((1, wb), lambda c, t: (0, c))
    gspec = pl.BlockSpec((heads, blk, blk), lambda c, t: (c, 0, 0))
    return pl.pallas_call(
        functools.partial(_lru_core_kernel, heads=heads, blk=blk),
        grid=(W // wb, T // tb),
        in_specs=[xspec, xspec, pl.BlockSpec((CONV_WIDTH, wb), lambda c, t: (0, c)), vspec,
                  gspec, vspec, gspec, vspec, vspec],
        out_specs=xspec,
        out_shape=jax.ShapeDtypeStruct((T, W), bf16),
        scratch_shapes=[pltpu.VMEM((8, wb), f32), pltpu.VMEM((8, wb), f32),
                        pltpu.VMEM((tb, wb), f32), pltpu.VMEM((tb, wb), f32), pltpu.VMEM((tb, wb), f32)],
        compiler_params=_cparams("parallel", "arbitrary"),
        name="lru_core",
    )(xr, gb, conv_w, conv_b.reshape(1, W), gx_w.astype(bf16), gx_b.reshape(1, W),
      ga_w.astype(bf16), ga_b.reshape(1, W), a_param.reshape(1, W))


def _attn_kernel(q_ref, kp_ref, kc_ref, vp_ref, vc_ref, bias_ref, sink_ref, o_ref, *, group, n_kv):
    blk = q_ref.shape[0]
    hd = HEAD_DIM
    col = lax.broadcasted_iota(jnp.int32, (group * blk, 2 * blk), 1)
    no_prev = jnp.logical_and(pl.program_id(0) == 0, col < blk)
    for kv in range(n_kv):
        q = jnp.concatenate([q_ref[:, (kv * group + g) * hd:(kv * group + g + 1) * hd] for g in range(group)],
                            axis=0)
        ksl = slice(kv * hd, (kv + 1) * hd)
        k = jnp.concatenate([kp_ref[:, ksl], kc_ref[:, ksl]], axis=0)
        v = jnp.concatenate([vp_ref[:, ksl], vc_ref[:, ksl]], axis=0)
        s = lax.dot_general(q, k, (((1,), (1,)), ((), ())), preferred_element_type=f32) * (hd ** -0.5)
        logits = jnp.where(no_prev, -jnp.inf, s + bias_ref[kv])
        sink = sink_ref[kv]
        m = jnp.maximum(jnp.max(logits, axis=-1, keepdims=True), sink)
        p = jnp.exp(logits - m)
        probs = p / (jnp.sum(p, axis=-1, keepdims=True) + jnp.exp(sink - m))
        o = jnp.dot(probs.astype(v.dtype), v, preferred_element_type=f32)
        for g in range(group):
            o_ref[:, (kv * group + g) * hd:(kv * group + g + 1) * hd] = (
                o[g * blk:(g + 1) * blk, :].astype(o_ref.dtype))


def _attention(qkv, sinks, n_q, n_kv):
    T = qkv.shape[0]
    group = n_q // n_kv
    blk = WINDOW
    nb = T // blk
    i = jnp.arange(blk)[:, None]
    j = jnp.arange(2 * blk)[None, :]
    dist = blk + i - j
    in_band = (dist >= 0) & (dist < WINDOW)
    slopes = jnp.exp2(-8.0 * jnp.arange(1, n_q + 1, dtype=f32) / n_q)
    bias = jnp.where(in_band[None], -slopes[:, None, None] * dist.astype(f32)[None], -jnp.inf)
    bias = bias.reshape(n_kv, group * blk, 2 * blk)
    sink = jnp.repeat(sinks.astype(f32).reshape(n_kv, group), blk, axis=1).reshape(n_kv, group * blk, 1)

    assert group * n_kv == n_q
    prev = lambda n: jnp.maximum(n - 1, 0)
    kvw = n_kv * HEAD_DIM
    return pl.pallas_call(
        functools.partial(_attn_kernel, group=group, n_kv=n_kv),
        grid=(nb,),
        in_specs=[pl.BlockSpec((blk, n_q * HEAD_DIM), lambda n: (n, 0)),
                  pl.BlockSpec((blk, kvw), lambda n: (prev(n), group)),
                  pl.BlockSpec((blk, kvw), lambda n: (n, group)),
                  pl.BlockSpec((blk, kvw), lambda n: (prev(n), group + 1)),
                  pl.BlockSpec((blk, kvw), lambda n: (n, group + 1)),
                  pl.BlockSpec((n_kv, group * blk, 2 * blk), lambda n: (0, 0, 0)),
                  pl.BlockSpec((n_kv, group * blk, 1), lambda n: (0, 0, 0))],
        out_specs=pl.BlockSpec((blk, n_q * HEAD_DIM), lambda n: (n, 0)),
        out_shape=jax.ShapeDtypeStruct((T, n_q * HEAD_DIM), bf16),
        compiler_params=_cparams("arbitrary"),
        name="swa_attention",
    )(qkv, qkv, qkv, qkv, qkv, bias, sink)


def _first_index_of_max(x, rows):
    m = jnp.max(x, axis=0, keepdims=True)
    idx = jnp.min(jnp.where(x == m, rows, x.shape[0]), axis=0, keepdims=True)
    return m, idx


def _router_kernel(h_ref, rwt_ref, rb_ref, idx_ref, gate_ref, cnt_ref, carry_sc, *, n_exp):
    tm = h_ref.shape[0]
    gsz = n_exp // N_GROUPS

    @pl.when(pl.program_id(0) == 0)
    def _():
        carry_sc[...] = jnp.zeros_like(carry_sc)

    h = h_ref[...]
    h_hi = h.astype(bf16)
    h_lo = (h - h_hi.astype(f32)).astype(bf16)
    nt = (((1,), (1,)), ((), ()))
    p = lax.dot_general(rwt_ref[...], h_hi, nt, preferred_element_type=f32)
    q = lax.dot_general(rwt_ref[:n_exp, :], h_lo, nt, preferred_element_type=f32)
    logits = p[:n_exp, :] + p[n_exp:, :] + q
    scores = jax.nn.sigmoid(logits)
    sel = scores + rb_ref[...]

    grow = lax.broadcasted_iota(jnp.int32, (gsz, tm), 0)
    gscores = []
    for g in range(N_GROUPS):
        sg = sel[g * gsz:(g + 1) * gsz, :]
        m1, i1 = _first_index_of_max(sg, grow)
        m2 = jnp.max(jnp.where(grow == i1, -jnp.inf, sg), axis=0, keepdims=True)
        gscores.append(m1 + m2)
    gs = jnp.concatenate(gscores, axis=0)
    g_iota = lax.broadcasted_iota(jnp.int32, (N_GROUPS, tm), 0)
    gpick = jnp.zeros((N_GROUPS, tm), jnp.bool_)
    for _ in range(TOPK_GROUPS):
        _, gi = _first_index_of_max(jnp.where(gpick, -jnp.inf, gs), g_iota)
        gpick = jnp.logical_or(gpick, g_iota == gi)

    erow = lax.broadcasted_iota(jnp.int32, (n_exp, tm), 0)
    emask = jnp.concatenate(
        [jnp.broadcast_to(gpick[g:g + 1, :], (gsz, tm)) for g in range(N_GROUPS)], axis=0)
    cand = jnp.where(emask, sel, -jnp.inf)
    idxs, gates = [], []
    taken = jnp.zeros((n_exp, tm), jnp.bool_)
    for _ in range(TOP_K):
        _, ei = _first_index_of_max(jnp.where(taken, -jnp.inf, cand), erow)
        hit = erow == ei
        taken = jnp.logical_or(taken, hit)
        idxs.append(ei)
        gates.append(jnp.sum(jnp.where(hit, scores, 0.0), axis=0, keepdims=True))
    gate = jnp.concatenate(gates, axis=0)
    idx_ref[...] = jnp.concatenate(idxs, axis=0)
    gate_ref[...] = gate / jnp.sum(gate, axis=0, keepdims=True) * ROUTED_SCALE

    total = carry_sc[...] + jnp.sum(jnp.where(taken, 1.0, 0.0), axis=1, keepdims=True)
    carry_sc[...] = total
    cnt_ref[...] = total


def _router(h, router_w, router_b):
    N, D = h.shape
    E = router_w.shape[1]
    tm = _pick(N, 512)
    wt = router_w.T
    w_hi = wt.astype(bf16)
    w_split = jnp.concatenate([w_hi, (wt - w_hi.astype(f32)).astype(bf16)], axis=0)
    kspec = pl.BlockSpec((TOP_K, tm), lambda i: (0, i))
    idx, gate, cnt = pl.pallas_call(
        functools.partial(_router_kernel, n_exp=E),
        grid=(N // tm,),
        in_specs=[pl.BlockSpec((tm, D), lambda i: (i, 0)),
                  pl.BlockSpec((2 * E, D), lambda i: (0, 0)),
                  pl.BlockSpec((E, 1), lambda i: (0, 0))],
        out_specs=[kspec, kspec, pl.BlockSpec((E, 128), lambda i: (0, 0))],
        out_shape=[jax.ShapeDtypeStruct((TOP_K, N), jnp.int32),
                   jax.ShapeDtypeStruct((TOP_K, N), f32),
                   jax.ShapeDtypeStruct((E, 128), f32)],
        scratch_shapes=[pltpu.VMEM((E, 128), f32)],
        compiler_params=_cparams("arbitrary"),
        name="moe_router",
    )(h, w_split, router_b.reshape(E, 1))
    return idx, gate, cnt[:, 0].astype(jnp.int32)


def _routing_tables(idx, counts, n_tok):
    B = EXPERT_ROWS
    E = counts.shape[0]
    A = TOP_K * n_tok
    assert A % B == 0
    n_blocks = A // B + E
    stride = A + B
    blocks = (counts + B - 1) // B
    bend = jnp.cumsum(blocks)
    n_used = bend[-1:].astype(jnp.int32)
    block_ids = jnp.arange(n_blocks, dtype=jnp.int32)
    block_expert = jnp.minimum(jnp.sum(bend[None, :] <= block_ids[:, None], axis=1), E - 1).astype(jnp.int32)

    assign = jnp.arange(A, dtype=jnp.int32)
    real = idx.reshape(-1) * stride + assign
    pad_count = blocks * B - counts
    j = jnp.arange(B, dtype=jnp.int32)[None, :]
    e = jnp.arange(E, dtype=jnp.int32)[:, None]
    pad = jnp.where(j < pad_count[:, None], e * stride + A + j, E * stride)
    keys = lax.sort(jnp.concatenate([real, pad.reshape(-1).astype(jnp.int32)]))
    low = keys % stride
    is_real = jnp.logical_and(low < A, keys < E * stride)
    pos = jnp.arange(n_blocks * B, dtype=jnp.int32)
    src = jnp.where(is_real, low % n_tok, 0)
    dst = jnp.where(is_real, low, A + (pos // B % 2) * B + pos % B)
    tables = jnp.stack([src.reshape(n_blocks, B), dst.reshape(n_blocks, B)], axis=1)
    return tables, block_expert, n_used


def _unpack_pairs(u):
    lo = lax.bitcast_convert_type(u << 16, f32)
    hi = lax.bitcast_convert_type(u & jnp.uint32(0xFFFF0000), f32)
    return lo, hi


def _silu(x):
    return x * jax.nn.sigmoid(x)


def _expert_kernel(be_ref, nu_ref, lay_ref, tab_hbm, hp_hbm, wu_ref, wd_ref, z_hbm,
                   tab_sm, xbuf, ybuf, wu_sc, wd_sc, sem_t, gsem, ssem):
    del lay_ref
    i = pl.program_id(0)
    last = pl.num_programs(0) - 1
    n_used = nu_ref[0]
    B, half = xbuf.shape[1], xbuf.shape[2]

    def table_copy(j):
        return pltpu.make_async_copy(tab_hbm.at[j], tab_sm.at[j % 3], sem_t.at[j % 3])

    def gather_copy(j, r):
        return pltpu.make_async_copy(hp_hbm.at[pl.ds(tab_sm[j % 3, 0, r], 1)],
                                     xbuf.at[j % 2, pl.ds(r, 1)], gsem.at[j % 2])

    def scatter_copy(j, r):
        return pltpu.make_async_copy(ybuf.at[j % 2, pl.ds(r, 1)],
                                     z_hbm.at[pl.ds(tab_sm[j % 3, 1, r], 1)], ssem.at[j % 2])

    def start_rows(copy, j, priority=0):
        for r in range(B):
            copy(j, r).start(priority=priority)

    def wait_gather(j):
        pltpu.make_async_copy(hp_hbm.at[pl.ds(0, B)], xbuf.at[j % 2], gsem.at[j % 2]).wait()

    def wait_scatter(j):
        pltpu.make_async_copy(ybuf.at[j % 2], z_hbm.at[pl.ds(0, B)], ssem.at[j % 2]).wait()

    @pl.when(i == 0)
    def _():
        table_copy(0).start()
        table_copy(0).wait()

        @pl.when(1 < n_used)
        def _():
            table_copy(1).start()

        start_rows(gather_copy, 0)

    @pl.when(i + 1 < n_used)
    def _():
        table_copy(i + 1).wait()
        start_rows(gather_copy, i + 1)

    @pl.when(i + 2 < n_used)
    def _():
        table_copy(i + 2).start()

    @pl.when(jnp.logical_and(i >= 2, i - 2 < n_used))
    def _():
        wait_scatter(i - 2)

    @pl.when(i < n_used)
    def _():
        @pl.when(jnp.logical_or(i == 0, be_ref[i] != be_ref[jnp.maximum(i - 1, 0)]))
        def _():
            wu_sc[...] = wu_ref[...].astype(bf16)
            wd_sc[...] = wd_ref[...].astype(bf16)

        wait_gather(i)
        lo, hi = _unpack_pairs(xbuf[i % 2])
        hu = (jnp.dot(lo.astype(bf16), wu_sc[:half, :], preferred_element_type=f32)
              + jnp.dot(hi.astype(bf16), wu_sc[half:, :], preferred_element_type=f32))
        hid = wd_sc.shape[0]
        act = (_silu(hu[:, :hid]) * hu[:, hid:]).astype(bf16)
        ybuf[i % 2] = _pack_pairs(jnp.dot(act, wd_sc[...], preferred_element_type=f32))
        start_rows(scatter_copy, i, priority=1)

    @pl.when(i == last)
    def _():
        @pl.when(jnp.logical_and(i >= 1, i - 1 < n_used))
        def _():
            wait_scatter(i - 1)

        @pl.when(i < n_used)
        def _():
            wait_scatter(i)


def _experts(hp, tables, block_expert, n_used, w_up, w_down, layer):
    N, D2 = hp.shape
    _, E, D, H2 = w_up.shape
    H = H2 // 2
    B = EXPERT_ROWS
    n_blocks = tables.shape[0]
    return pl.pallas_call(
        _expert_kernel,
        grid_spec=pltpu.PrefetchScalarGridSpec(
            num_scalar_prefetch=3,
            grid=(n_blocks,),
            in_specs=[pl.BlockSpec(memory_space=pl.ANY),
                      pl.BlockSpec(memory_space=pl.ANY),
                      pl.BlockSpec((None, None, D, H2), lambda i, be, nu, lay: (lay[0], be[i], 0, 0)),
                      pl.BlockSpec((None, None, H, D), lambda i, be, nu, lay: (lay[0], be[i], 0, 0))],
            out_specs=pl.BlockSpec(memory_space=pl.ANY),
            scratch_shapes=[pltpu.SMEM((3, 2, B), jnp.int32),
                            pltpu.VMEM((2, B, D2), jnp.uint32), pltpu.VMEM((2, B, D2), jnp.uint32),
                            pltpu.VMEM((D, H2), bf16), pltpu.VMEM((H, D), bf16),
                            pltpu.SemaphoreType.DMA((3,)), pltpu.SemaphoreType.DMA((2,)),
                            pltpu.SemaphoreType.DMA((2,))],
        ),
        out_shape=jax.ShapeDtypeStruct((TOP_K * N + 2 * B, D2), jnp.uint32),
        compiler_params=_cparams("arbitrary"),
        name="moe_experts",
    )(block_expert, n_used, jnp.full((1,), layer, jnp.int32), tables, hp, w_up, w_down)


def _combine_kernel(*refs, alpha):
    z_refs = refs[:TOP_K]
    (gate_ref, h_ref, hb_ref, sg_ref, sv_ref, sd_ref, g_ref, b_ref,
     ho_ref, hbo_ref, hpo_ref, z_sc, gate_sc) = refs[TOP_K:]
    tm, d = h_ref.shape
    half = d // 2
    lanes = gate_sc.shape[2]
    hb = hb_ref[...]
    act = (_silu(jnp.dot(hb, sg_ref[...], preferred_element_type=f32))
           * jnp.dot(hb, sv_ref[...], preferred_element_type=f32)).astype(bf16)
    z_sc[...] = alpha * h_ref[...] + jnp.dot(act, sd_ref[...], preferred_element_type=f32)
    gate = gate_ref[...]
    for k in range(TOP_K):
        gate_sc[k] = jnp.broadcast_to(gate[:, k:k + 1], (tm, lanes))
    for c in range(half // lanes):
        lo_sl = slice(c * lanes, (c + 1) * lanes)
        hi_sl = slice(half + c * lanes, half + (c + 1) * lanes)
        z_lo, z_hi = z_sc[:, lo_sl], z_sc[:, hi_sl]
        for k in range(TOP_K):
            lo, hi = _unpack_pairs(z_refs[k][:, lo_sl])
            g = gate_sc[k]
            z_lo = z_lo + lo * g
            z_hi = z_hi + hi * g
        z_sc[:, lo_sl] = z_lo
        z_sc[:, hi_sl] = z_hi
    _store_hidden(_layer_norm_rows(z_sc[...], g_ref[...], b_ref[...]), ho_ref, hbo_ref, hpo_ref)


def _combine(zs, gate_t, h, hb, s_gate, s_val, s_down, ln_g, ln_b, alpha):
    N, D = h.shape
    H = s_gate.shape[1]
    tm = _pick(N, 128)
    nt = N // tm
    specs, shapes = _hidden_out(N, D, tm)
    row = pl.BlockSpec((tm, D), lambda i: (i, 0))
    full = lambda a, b: pl.BlockSpec((a, b), lambda i: (0, 0))
    z_specs = [pl.BlockSpec((tm, D // 2), lambda i, k=k: (k * nt + i, 0)) for k in range(TOP_K)]
    return pl.pallas_call(
        functools.partial(_combine_kernel, alpha=alpha),
        grid=(nt,),
        in_specs=z_specs + [pl.BlockSpec((tm, TOP_K), lambda i: (i, 0)),
                            row, row, full(D, H), full(D, H), full(H, D), full(1, D), full(1, D)],
        out_specs=specs,
        out_shape=shapes,
        scratch_shapes=[pltpu.VMEM((tm, D), f32), pltpu.VMEM((TOP_K, tm, 128), f32)],
        compiler_params=_cparams("parallel"),
        name="moe_combine",
    )(*([zs] * TOP_K), gate_t, h, hb, s_gate, s_val, s_down, ln_g.reshape(1, D), ln_b.reshape(1, D))


def _moe_layer(h, hb, hp, router_w, router_b, w_up, w_down, layer, s_gate, s_val, s_down,
               ln_g, ln_b, alpha):
    idx, gate, counts = _router(h, router_w, router_b)
    tables, block_expert, n_used = _routing_tables(idx, counts, h.shape[0])
    zs = _experts(hp, tables, block_expert, n_used, w_up, w_down, layer)
    return _combine(zs, gate.T, h, hb, s_gate, s_val, s_down, ln_g, ln_b, alpha)


def kernel(x, lru_w_in, lru_b_in, lru_conv_w, lru_conv_b, lru_gx_w, lru_gx_b, lru_ga_w, lru_ga_b,
           lru_a_param, lru_w_out, attn_w_qkv, attn_sinks, attn_w_o, ln_g, ln_b,
           moe_router_w, moe_router_b, moe_w_up, moe_w_down, moe_shared_up, moe_shared_down):
    Bsz, T, D = x.shape
    depth = ln_g.shape[0]
    alpha = (2.0 * depth) ** 0.25
    W = lru_w_in.shape[2] // 2
    H = moe_w_down.shape[2]
    n_q = D // HEAD_DIM
    n_kv = (attn_w_qkv.shape[2] // HEAD_DIM - n_q) // 2
    N = Bsz * T
    assert Bsz == 1, "sequence mixers below assume one sequence"

    h = x.reshape(N, D)
    hb = h.astype(bf16)
    hp = None
    for layer in range(depth):
        j = layer // 2
        if layer % 2 == 0:
            w_in = lru_w_in[j].astype(bf16)
            xr, gb = _lru_in(hb, w_in[:, :W], w_in[:, W:],
                             lru_b_in[j, :W].reshape(1, W), lru_b_in[j, W:].reshape(1, W))
            y = _lru_core(xr, gb, lru_conv_w[j], lru_conv_b[j], lru_gx_w[j], lru_gx_b[j].reshape(-1),
                          lru_ga_w[j], lru_ga_b[j].reshape(-1), lru_a_param[j])
            z = _matmul_residual(y, lru_w_out[j].astype(bf16), h, alpha)
        else:
            qkv = _matmul(hb, attn_w_qkv[j].astype(bf16), bf16)
            o = _attention(qkv, attn_sinks[j], n_q, n_kv)
            z = _matmul_residual(o, attn_w_o[j].astype(bf16), h, alpha)
        h, hb, hp = _layer_norm(z, ln_g[layer, 0], ln_b[layer, 0])
        h, hb, hp = _moe_layer(
            h, hb, hp, moe_router_w[layer], moe_router_b[layer], moe_w_up, moe_w_down, layer,
            moe_shared_up[layer, :, :H].astype(bf16), moe_shared_up[layer, :, H:].astype(bf16),
            moe_shared_down[layer].astype(bf16), ln_g[layer, 1], ln_b[layer, 1], alpha)
    return h.reshape(Bsz, T, D)
```

```python
import functools

import jax
import jax.numpy as jnp
from jax import lax
from jax.experimental import pallas as pl
from jax.experimental.pallas import tpu as pltpu

HEAD_DIM = 128
WINDOW = 128
CONV_WIDTH = 4
LRU_C = 8.0
TOP_K = 8
N_GROUPS = 8
TOPK_GROUPS = 4
ROUTED_SCALE = 2.5
LN_EPS = 1e-5

V7X_VMEM_BYTES = 64 * 1024 * 1024
VMEM_LIMIT = V7X_VMEM_BYTES - 8 * 1024 * 1024
EXPERT_ROWS = 256

f32 = jnp.float32
bf16 = jnp.bfloat16


def _cparams(*sem):
    return pltpu.CompilerParams(dimension_semantics=sem, vmem_limit_bytes=VMEM_LIMIT)


def _pick(n, pref):
    t = min(pref, n)
    while n % t:
        t //= 2
    return t


def _mm_kernel(x_ref, w_ref, o_ref):
    o_ref[...] = jnp.dot(x_ref[...], w_ref[...], preferred_element_type=f32).astype(o_ref.dtype)


def _matmul(x, w, out_dtype):
    M, K = x.shape
    N = w.shape[1]
    tm, tn = _pick(M, 1024), _pick(N, 512)
    return pl.pallas_call(
        _mm_kernel,
        grid=(M // tm, N // tn),
        in_specs=[pl.BlockSpec((tm, K), lambda i, j: (i, 0)),
                  pl.BlockSpec((K, tn), lambda i, j: (0, j))],
        out_specs=pl.BlockSpec((tm, tn), lambda i, j: (i, j)),
        out_shape=jax.ShapeDtypeStruct((M, N), out_dtype),
        compiler_params=_cparams("parallel", "arbitrary"),
        name="matmul",
    )(x, w)


def _mm_res_kernel(x_ref, w_ref, r_ref, o_ref, *, alpha):
    o_ref[...] = alpha * r_ref[...] + jnp.dot(x_ref[...], w_ref[...], preferred_element_type=f32)


def _matmul_residual(x, w, res, alpha):
    M, K = x.shape
    N = w.shape[1]
    tm, tn = _pick(M, 1024), _pick(N, 512)
    return pl.pallas_call(
        functools.partial(_mm_res_kernel, alpha=alpha),
        grid=(M // tm, N // tn),
        in_specs=[pl.BlockSpec((tm, K), lambda i, j: (i, 0)),
                  pl.BlockSpec((K, tn), lambda i, j: (0, j)),
                  pl.BlockSpec((tm, tn), lambda i, j: (i, j))],
        out_specs=pl.BlockSpec((tm, tn), lambda i, j: (i, j)),
        out_shape=jax.ShapeDtypeStruct((M, N), f32),
        compiler_params=_cparams("parallel", "arbitrary"),
        name="matmul_residual",
    )(x, w, res)


def _gelu_tanh(x):
    return 0.5 * x * (1.0 + jnp.tanh(0.7978845608028654 * (x + 0.044715 * (x * x * x))))


def _lru_in_kernel(x_ref, wx_ref, wy_ref, bx_ref, by_ref, xr_ref, gb_ref):
    x = x_ref[...]
    xr_ref[...] = jnp.dot(x, wx_ref[...], preferred_element_type=f32) + bx_ref[...]
    gb_ref[...] = _gelu_tanh(jnp.dot(x, wy_ref[...], preferred_element_type=f32) + by_ref[...])


def _lru_in(x, wx, wy, bx, by):
    M, K = x.shape
    W = wx.shape[1]
    tm, tn = _pick(M, 1024), _pick(W, 256)
    wspec = pl.BlockSpec((K, tn), lambda i, j: (0, j))
    bspec = pl.BlockSpec((1, tn), lambda i, j: (0, j))
    ospec = pl.BlockSpec((tm, tn), lambda i, j: (i, j))
    return pl.pallas_call(
        _lru_in_kernel,
        grid=(M // tm, W // tn),
        in_specs=[pl.BlockSpec((tm, K), lambda i, j: (i, 0)), wspec, wspec, bspec, bspec],
        out_specs=[ospec, ospec],
        out_shape=[jax.ShapeDtypeStruct((M, W), f32)] * 2,
        compiler_params=_cparams("parallel", "arbitrary"),
        name="lru_in",
    )(x, wx, wy, bx, by)


def _layer_norm_rows(z, g, b):
    mu = jnp.mean(z, axis=-1, keepdims=True)
    zc = z - mu
    var = jnp.mean(zc * zc, axis=-1, keepdims=True)
    return zc * lax.rsqrt(var + LN_EPS) * g + b


def _pack_pairs(x):
    half = x.shape[1] // 2
    bits = lax.bitcast_convert_type(x.astype(bf16).astype(f32), jnp.uint32)
    return (bits[:, :half] >> 16) | bits[:, half:]


def _store_hidden(h, h_ref, hb_ref, hp_ref):
    h_ref[...] = h
    hb_ref[...] = h.astype(bf16)
    hp_ref[...] = _pack_pairs(h)


def _ln_kernel(z_ref, g_ref, b_ref, h_ref, hb_ref, hp_ref):
    _store_hidden(_layer_norm_rows(z_ref[...], g_ref[...], b_ref[...]), h_ref, hb_ref, hp_ref)


def _hidden_out(M, D, tm):
    specs = [pl.BlockSpec((tm, D), lambda i: (i, 0)),
             pl.BlockSpec((tm, D), lambda i: (i, 0)),
             pl.BlockSpec((tm, D // 2), lambda i: (i, 0))]
    shapes = [jax.ShapeDtypeStruct((M, D), f32),
              jax.ShapeDtypeStruct((M, D), bf16),
              jax.ShapeDtypeStruct((M, D // 2), jnp.uint32)]
    return specs, shapes


def _layer_norm(z, g, b):
    M, D = z.shape
    tm = _pick(M, 256)
    specs, shapes = _hidden_out(M, D, tm)
    vspec = pl.BlockSpec((1, D), lambda i: (0, 0))
    return pl.pallas_call(
        _ln_kernel,
        grid=(M // tm,),
        in_specs=[pl.BlockSpec((tm, D), lambda i: (i, 0)), vspec, vspec],
        out_specs=specs,
        out_shape=shapes,
        compiler_params=_cparams("parallel"),
        name="layer_norm",
    )(z, g.reshape(1, D), b.reshape(1, D))


def _sigmoid(x):
    return 0.5 + 0.5 * jnp.tanh(0.5 * x)


def _softplus(x):
    return jnp.maximum(x, 0.0) + jnp.log1p(jnp.exp(-jnp.abs(x)))


def _lru_core_kernel(xr_ref, gb_ref, cw_ref, cb_ref, gxw_ref, gxb_ref, gaw_ref, gab_ref, ap_ref,
                     y_ref, tail_sc, h_sc, a_sc, b_sc, hs_sc, *, heads, blk):
    tb = xr_ref.shape[0]

    @pl.when(pl.program_id(1) == 0)
    def _():
        tail_sc[...] = jnp.zeros_like(tail_sc)
        h_sc[...] = jnp.zeros_like(h_sc)

    xr = xr_ref[...]
    ext = jnp.concatenate([tail_sc[...], xr], axis=0)
    tail_sc[...] = xr[tb - 8:, :]
    cw = cw_ref[...]
    xc = cb_ref[...]
    for j in range(CONV_WIDTH):
        off = 8 - (CONV_WIDTH - 1) + j
        xc = xc + cw[j:j + 1, :] * ext[off:off + tb, :]
    xcb = xc.astype(bf16)

    for hd in range(heads):
        sl = slice(hd * blk, (hd + 1) * blk)
        xh = xcb[:, sl]
        gate_x = _sigmoid(jnp.dot(xh, gxw_ref[hd], preferred_element_type=f32) + gxb_ref[:, sl])
        gate_a = _sigmoid(jnp.dot(xh, gaw_ref[hd], preferred_element_type=f32) + gab_ref[:, sl])
        a = jnp.exp((-LRU_C * _softplus(-ap_ref[:, sl])) * gate_a)
        a_sc[:, sl] = a
        b_sc[:, sl] = xc[:, sl] * gate_x * jnp.sqrt(1.0 - a * a)

    def body(i, h):
        base = pl.multiple_of(i * 8, 8)
        a8 = a_sc[pl.ds(base, 8), :]
        b8 = b_sc[pl.ds(base, 8), :]
        rows = []
        for r in range(8):
            h = a8[r:r + 1, :] * h + b8[r:r + 1, :]
            rows.append(h)
        hs_sc[pl.ds(base, 8), :] = jnp.concatenate(rows, axis=0)
        return h

    h_sc[0:1, :] = lax.fori_loop(0, tb // 8, body, h_sc[0:1, :])
    y_ref[...] = (hs_sc[...] * gb_ref[...]).astype(y_ref.dtype)


def _lru_core(xr, gb, conv_w, conv_b, gx_w, gx_b, ga_w, ga_b, a_param):
    T, W = xr.shape
    nh, blk = gx_w.shape[0], gx_w.shape[1]
    wb = _pick(W, 1024)
    heads = wb // blk
    tb = _pick(T, 256)
    xspec = pl.BlockSpec((tb, wb), lambda c, t: (t, c))
    vspec = pl.BlockSpec((1, wb), lambda c, t: (0, c))
    gspec = pl.BlockSpec((heads, blk, blk), lambda c, t: (c, 0, 0))
    return pl.pallas_call(
        functools.partial(_lru_core_kernel, heads=heads, blk=blk),
        grid=(W // wb, T // tb),
        in_specs=[xspec, xspec, pl.BlockSpec((CONV_WIDTH, wb), lambda c, t: (0, c)), vspec,
                  gspec, vspec, gspec, vspec, vspec],
        out_specs=xspec,
        out_shape=jax.ShapeDtypeStruct((T, W), bf16),
        scratch_shapes=[pltpu.VMEM((8, wb), f32), pltpu.VMEM((8, wb), f32),
                        pltpu.VMEM((tb, wb), f32), pltpu.VMEM((tb, wb), f32), pltpu.VMEM((tb, wb), f32)],
        compiler_params=_cparams("parallel", "arbitrary"),
        name="lru_core",
    )(xr, gb, conv_w, conv_b.reshape(1, W), gx_w.astype(bf16), gx_b.reshape(1, W),
      ga_w.astype(bf16), ga_b.reshape(1, W), a_param.reshape(1, W))


def _attn_kernel(q_ref, kp_ref, kc_ref, vp_ref, vc_ref, bias_ref, sink_ref, o_ref, *, group, n_kv):
    blk = q_ref.shape[0]
    hd = HEAD_DIM
    col = lax.broadcasted_iota(jnp.int32, (group * blk, 2 * blk), 1)
    no_prev = jnp.logical_and(pl.program_id(0) == 0, col < blk)
    for kv in range(n_kv):
        q = jnp.concatenate([q_ref[:, (kv * group + g) * hd:(kv * group + g + 1) * hd] for g in range(group)],
                            axis=0)
        ksl = slice(kv * hd, (kv + 1) * hd)
        k = jnp.concatenate([kp_ref[:, ksl], kc_ref[:, ksl]], axis=0)
        v = jnp.concatenate([vp_ref[:, ksl], vc_ref[:, ksl]], axis=0)
        s = lax.dot_general(q, k, (((1,), (1,)), ((), ())), preferred_element_type=f32) * (hd ** -0.5)
        logits = jnp.where(no_prev, -jnp.inf, s + bias_ref[kv])
        sink = sink_ref[kv]
        m = jnp.maximum(jnp.max(logits, axis=-1, keepdims=True), sink)
        p = jnp.exp(logits - m)
        probs = p / (jnp.sum(p, axis=-1, keepdims=True) + jnp.exp(sink - m))
        o = jnp.dot(probs.astype(v.dtype), v, preferred_element_type=f32)
        for g in range(group):
            o_ref[:, (kv * group + g) * hd:(kv * group + g + 1) * hd] = (
                o[g * blk:(g + 1) * blk, :].astype(o_ref.dtype))


def _attention(qkv, sinks, n_q, n_kv):
    T = qkv.shape[0]
    group = n_q // n_kv
    blk = WINDOW
    nb = T // blk
    i = jnp.arange(blk)[:, None]
    j = jnp.arange(2 * blk)[None, :]
    dist = blk + i - j
    in_band = (dist >= 0) & (dist < WINDOW)
    slopes = jnp.exp2(-8.0 * jnp.arange(1, n_q + 1, dtype=f32) / n_q)
    bias = jnp.where(in_band[None], -slopes[:, None, None] * dist.astype(f32)[None], -jnp.inf)
    bias = bias.reshape(n_kv, group * blk, 2 * blk)
    sink = jnp.repeat(sinks.astype(f32).reshape(n_kv, group), blk, axis=1).reshape(n_kv, group * blk, 1)

    assert group * n_kv == n_q
    prev = lambda n: jnp.maximum(n - 1, 0)
    kvw = n_kv * HEAD_DIM
    return pl.pallas_call(
        functools.partial(_attn_kernel, group=group, n_kv=n_kv),
        grid=(nb,),
        in_specs=[pl.BlockSpec((blk, n_q * HEAD_DIM), lambda n: (n, 0)),
                  pl.BlockSpec((blk, kvw), lambda n: (prev(n), group)),
                  pl.BlockSpec((blk, kvw), lambda n: (n, group)),
                  pl.BlockSpec((blk, kvw), lambda n: (prev(n), group + 1)),
                  pl.BlockSpec((blk, kvw), lambda n: (n, group + 1)),
                  pl.BlockSpec((n_kv, group * blk, 2 * blk), lambda n: (0, 0, 0)),
                  pl.BlockSpec((n_kv, group * blk, 1), lambda n: (0, 0, 0))],
        out_specs=pl.BlockSpec((blk, n_q * HEAD_DIM), lambda n: (n, 0)),
        out_shape=jax.ShapeDtypeStruct((T, n_q * HEAD_DIM), bf16),
        compiler_params=_cparams("arbitrary"),
        name="swa_attention",
    )(qkv, qkv, qkv, qkv, qkv, bias, sink)


def _first_index_of_max(x, rows):
    m = jnp.max(x, axis=0, keepdims=True)
    idx = jnp.min(jnp.where(x == m, rows, x.shape[0]), axis=0, keepdims=True)
    return m, idx


def _router_kernel(h_ref, rwt_ref, rb_ref, idx_ref, gate_ref, rank_ref, cnt_ref, carry_sc, *, n_exp):
    tm = h_ref.shape[0]
    gsz = n_exp // N_GROUPS

    @pl.when(pl.program_id(0) == 0)
    def _():
        carry_sc[...] = jnp.zeros_like(carry_sc)

    h = h_ref[...]
    h_hi = h.astype(bf16)
    h_lo = (h - h_hi.astype(f32)).astype(bf16)
    nt = (((1,), (1,)), ((), ()))
    p = lax.dot_general(rwt_ref[...], h_hi, nt, preferred_element_type=f32)
    q = lax.dot_general(rwt_ref[:n_exp, :], h_lo, nt, preferred_element_type=f32)
    logits = p[:n_exp, :] + p[n_exp:, :] + q
    scores = jax.nn.sigmoid(logits)
    sel = scores + rb_ref[...]

    grow = lax.broadcasted_iota(jnp.int32, (gsz, tm), 0)
    gscores = []
    for g in range(N_GROUPS):
        sg = sel[g * gsz:(g + 1) * gsz, :]
        m1, i1 = _first_index_of_max(sg, grow)
        m2 = jnp.max(jnp.where(grow == i1, -jnp.inf, sg), axis=0, keepdims=True)
        gscores.append(m1 + m2)
    gs = jnp.concatenate(gscores, axis=0)
    g_iota = lax.broadcasted_iota(jnp.int32, (N_GROUPS, tm), 0)
    gpick = jnp.zeros((N_GROUPS, tm), jnp.bool_)
    for _ in range(TOPK_GROUPS):
        _, gi = _first_index_of_max(jnp.where(gpick, -jnp.inf, gs), g_iota)
        gpick = jnp.logical_or(gpick, g_iota == gi)

    erow = lax.broadcasted_iota(jnp.int32, (n_exp, tm), 0)
    emask = jnp.concatenate(
        [jnp.broadcast_to(gpick[g:g + 1, :], (gsz, tm)) for g in range(N_GROUPS)], axis=0)
    cand = jnp.where(emask, sel, -jnp.inf)
    idxs, gates = [], []
    taken = jnp.zeros((n_exp, tm), jnp.bool_)
    for _ in range(TOP_K):
        _, ei = _first_index_of_max(jnp.where(taken, -jnp.inf, cand), erow)
        hit = erow == ei
        taken = jnp.logical_or(taken, hit)
        idxs.append(ei)
        gates.append(jnp.sum(jnp.where(hit, scores, 0.0), axis=0, keepdims=True))
    gate = jnp.concatenate(gates, axis=0)
    idx_ref[...] = jnp.concatenate(idxs, axis=0)
    gate_ref[...] = gate / jnp.sum(gate, axis=0, keepdims=True) * ROUTED_SCALE

    onehot = jnp.where(taken, 1.0, 0.0).astype(bf16)
    r_i = lax.broadcasted_iota(jnp.int32, (tm, tm), 0)
    c_i = lax.broadcasted_iota(jnp.int32, (tm, tm), 1)
    before = jnp.where(r_i < c_i, 1.0, 0.0).astype(bf16)
    prefix = jnp.dot(onehot, before, preferred_element_type=f32) + carry_sc[:, 0:1]
    ranks = [jnp.sum(jnp.where(erow == idxs[k], prefix, 0.0), axis=0, keepdims=True) for k in range(TOP_K)]
    rank_ref[...] = jnp.concatenate(ranks, axis=0).astype(jnp.int32)

    total = carry_sc[...] + jnp.sum(jnp.where(taken, 1.0, 0.0), axis=1, keepdims=True)
    carry_sc[...] = total
    cnt_ref[...] = total


def _router(h, router_w, router_b):
    N, D = h.shape
    E = router_w.shape[1]
    tm = _pick(N, 512)
    wt = router_w.T
    w_hi = wt.astype(bf16)
    w_split = jnp.concatenate([w_hi, (wt - w_hi.astype(f32)).astype(bf16)], axis=0)
    kspec = pl.BlockSpec((TOP_K, tm), lambda i: (0, i))
    idx, gate, rank, cnt = pl.pallas_call(
        functools.partial(_router_kernel, n_exp=E),
        grid=(N // tm,),
        in_specs=[pl.BlockSpec((tm, D), lambda i: (i, 0)),
                  pl.BlockSpec((2 * E, D), lambda i: (0, 0)),
                  pl.BlockSpec((E, 1), lambda i: (0, 0))],
        out_specs=[kspec, kspec, kspec, pl.BlockSpec((E, 128), lambda i: (0, 0))],
        out_shape=[jax.ShapeDtypeStruct((TOP_K, N), jnp.int32),
                   jax.ShapeDtypeStruct((TOP_K, N), f32),
                   jax.ShapeDtypeStruct((TOP_K, N), jnp.int32),
                   jax.ShapeDtypeStruct((E, 128), f32)],
        scratch_shapes=[pltpu.VMEM((E, 128), f32)],
        compiler_params=_cparams("arbitrary"),
        name="moe_router",
    )(h, w_split, router_b.reshape(E, 1))
    return idx, gate, rank, cnt[:, 0].astype(jnp.int32)


def _routing_tables(idx, rank, counts, n_tok, tile):
    B = EXPERT_ROWS
    E = counts.shape[0]
    A = TOP_K * n_tok
    assert A % B == 0
    n_blocks = A // B + E
    stride = A + B
    blocks = (counts + B - 1) // B
    bend = jnp.cumsum(blocks)
    n_used = bend[-1:].astype(jnp.int32)
    block_ids = jnp.arange(n_blocks, dtype=jnp.int32)
    block_expert = jnp.minimum(jnp.sum(bend[None, :] <= block_ids[:, None], axis=1), E - 1).astype(jnp.int32)

    first_row = ((bend - blocks) * B).astype(jnp.int32)
    onehot = idx[:, :, None] == jnp.arange(E, dtype=jnp.int32)
    dest = rank + jnp.sum(jnp.where(onehot, first_row, 0), axis=-1)
    dest = dest.reshape(TOP_K, n_tok // tile, tile).transpose(1, 0, 2)

    token = jnp.arange(n_tok, dtype=jnp.int32)[None, :]
    real = (idx * stride + token).reshape(-1)
    pad_count = blocks * B - counts
    j = jnp.arange(B, dtype=jnp.int32)[None, :]
    e = jnp.arange(E, dtype=jnp.int32)[:, None]
    pad = jnp.where(j < pad_count[:, None], e * stride + A + j, E * stride)
    keys = lax.sort(jnp.concatenate([real, pad.reshape(-1).astype(jnp.int32)]))
    low = keys % stride
    is_real = jnp.logical_and(low < A, keys < E * stride)
    src = jnp.where(is_real, low, 0).reshape(n_blocks, 1, B)
    return src, dest, block_expert, n_used


def _unpack_pairs(u):
    lo = lax.bitcast_convert_type(u << 16, f32)
    hi = lax.bitcast_convert_type(u & jnp.uint32(0xFFFF0000), f32)
    return lo, hi


def _silu(x):
    return x * jax.nn.sigmoid(x)


def _expert_kernel(be_ref, nu_ref, lay_ref, tab_hbm, hp_hbm, wu_ref, wd_ref, ys_ref,
                   tab_sm, xbuf, wu_sc, wd_sc, sem_t, gsem):
    del lay_ref
    i = pl.program_id(0)
    n_used = nu_ref[0]
    B, half = xbuf.shape[1], xbuf.shape[2]

    def table_copy(j):
        return pltpu.make_async_copy(tab_hbm.at[j], tab_sm.at[j % 3], sem_t.at[j % 3])

    def gather_copy(j, r):
        return pltpu.make_async_copy(hp_hbm.at[pl.ds(tab_sm[j % 3, 0, r], 1)],
                                     xbuf.at[j % 2, pl.ds(r, 1)], gsem.at[j % 2])

    def start_rows(j):
        for r in range(B):
            gather_copy(j, r).start()

    def wait_rows(j):
        pltpu.make_async_copy(hp_hbm.at[pl.ds(0, B)], xbuf.at[j % 2], gsem.at[j % 2]).wait()

    @pl.when(i == 0)
    def _():
        table_copy(0).start()
        table_copy(0).wait()

        @pl.when(1 < n_used)
        def _():
            table_copy(1).start()

        start_rows(0)

    @pl.when(i + 1 < n_used)
    def _():
        table_copy(i + 1).wait()
        start_rows(i + 1)

    @pl.when(i + 2 < n_used)
    def _():
        table_copy(i + 2).start()

    @pl.when(i < n_used)
    def _():
        @pl.when(jnp.logical_or(i == 0, be_ref[i] != be_ref[jnp.maximum(i - 1, 0)]))
        def _():
            wu_sc[...] = wu_ref[...].astype(bf16)
            wd_sc[...] = wd_ref[...].astype(bf16)

        wait_rows(i)
        lo, hi = _unpack_pairs(xbuf[i % 2])
        hu = (jnp.dot(lo.astype(bf16), wu_sc[:half, :], preferred_element_type=f32)
              + jnp.dot(hi.astype(bf16), wu_sc[half:, :], preferred_element_type=f32))
        hid = wd_sc.shape[0]
        act = (_silu(hu[:, :hid]) * hu[:, hid:]).astype(bf16)
        ys_ref[...] = _pack_pairs(jnp.dot(act, wd_sc[...], preferred_element_type=f32))


def _experts(hp, src, block_expert, n_used, w_up, w_down, layer):
    N, D2 = hp.shape
    _, E, D, H2 = w_up.shape
    H = H2 // 2
    B = EXPERT_ROWS
    n_blocks = src.shape[0]
    live = lambda i, be, nu, lay: (jnp.minimum(i, nu[0] - 1), 0)
    return pl.pallas_call(
        _expert_kernel,
        grid_spec=pltpu.PrefetchScalarGridSpec(
            num_scalar_prefetch=3,
            grid=(n_blocks,),
            in_specs=[pl.BlockSpec(memory_space=pl.ANY),
                      pl.BlockSpec(memory_space=pl.ANY),
                      pl.BlockSpec((None, None, D, H2), lambda i, be, nu, lay: (lay[0], be[i], 0, 0)),
                      pl.BlockSpec((None, None, H, D), lambda i, be, nu, lay: (lay[0], be[i], 0, 0))],
            out_specs=pl.BlockSpec((B, D2), live),
            scratch_shapes=[pltpu.SMEM((3, 1, B), jnp.int32),
                            pltpu.VMEM((2, B, D2), jnp.uint32),
                            pltpu.VMEM((D, H2), bf16), pltpu.VMEM((H, D), bf16),
                            pltpu.SemaphoreType.DMA((3,)), pltpu.SemaphoreType.DMA((2,))],
        ),
        out_shape=jax.ShapeDtypeStruct((n_blocks * B, D2), jnp.uint32),
        compiler_params=_cparams("arbitrary"),
        name="moe_experts",
    )(block_expert, n_used, jnp.full((1,), layer, jnp.int32), src, hp, w_up, w_down)


COMBINE_TOKENS = 128


def _combine_kernel(dest_hbm, ys_hbm, gate_ref, h_ref, hb_ref, sg_ref, sv_ref, sd_ref, g_ref, b_ref,
                    ho_ref, hbo_ref, hpo_ref, tab_sm, ybuf, z_sc, gate_sc, sem_t, gsem, *, alpha):
    i = pl.program_id(0)
    n_tiles = pl.num_programs(0)
    tm, d = h_ref.shape
    half = d // 2
    lanes = gate_sc.shape[2]

    def table_copy(j):
        return pltpu.make_async_copy(dest_hbm.at[j], tab_sm.at[j % 3], sem_t.at[j % 3])

    def start_rows(j):
        for k in range(TOP_K):
            for t in range(tm):
                pltpu.make_async_copy(ys_hbm.at[pl.ds(tab_sm[j % 3, k, t], 1)],
                                      ybuf.at[j % 2, k, pl.ds(t, 1)], gsem.at[j % 2]).start()

    def wait_rows(j):
        for k in range(TOP_K):
            pltpu.make_async_copy(ys_hbm.at[pl.ds(0, tm)], ybuf.at[j % 2, k], gsem.at[j % 2]).wait()

    @pl.when(i == 0)
    def _():
        table_copy(0).start()
        table_copy(0).wait()

        @pl.when(1 < n_tiles)
        def _():
            table_copy(1).start()

        start_rows(0)

    @pl.when(i + 1 < n_tiles)
    def _():
        table_copy(i + 1).wait()
        start_rows(i + 1)

    @pl.when(i + 2 < n_tiles)
    def _():
        table_copy(i + 2).start()

    hb = hb_ref[...]
    act = (_silu(jnp.dot(hb, sg_ref[...], preferred_element_type=f32))
           * jnp.dot(hb, sv_ref[...], preferred_element_type=f32)).astype(bf16)
    z_sc[...] = alpha * h_ref[...] + jnp.dot(act, sd_ref[...], preferred_element_type=f32)
    gate = gate_ref[...]
    for k in range(TOP_K):
        gate_sc[k] = jnp.broadcast_to(gate[:, k:k + 1], (tm, lanes))
    wait_rows(i)
    slot = i % 2
    for c in range(half // lanes):
        lo_sl = slice(c * lanes, (c + 1) * lanes)
        hi_sl = slice(half + c * lanes, half + (c + 1) * lanes)
        z_lo, z_hi = z_sc[:, lo_sl], z_sc[:, hi_sl]
        for k in range(TOP_K):
            lo, hi = _unpack_pairs(ybuf[slot, k, :, lo_sl])
            g = gate_sc[k]
            z_lo = z_lo + lo * g
            z_hi = z_hi + hi * g
        z_sc[:, lo_sl] = z_lo
        z_sc[:, hi_sl] = z_hi
    _store_hidden(_layer_norm_rows(z_sc[...], g_ref[...], b_ref[...]), ho_ref, hbo_ref, hpo_ref)


def _combine(ys, dest, gate_t, h, hb, s_gate, s_val, s_down, ln_g, ln_b, alpha):
    N, D = h.shape
    H = s_gate.shape[1]
    nt, _, tm = dest.shape
    specs, shapes = _hidden_out(N, D, tm)
    row = pl.BlockSpec((tm, D), lambda i: (i, 0))
    full = lambda a, b: pl.BlockSpec((a, b), lambda i: (0, 0))
    hbm = pl.BlockSpec(memory_space=pl.ANY)
    return pl.pallas_call(
        functools.partial(_combine_kernel, alpha=alpha),
        grid=(nt,),
        in_specs=[hbm, hbm, pl.BlockSpec((tm, TOP_K), lambda i: (i, 0)),
                  row, row, full(D, H), full(D, H), full(H, D), full(1, D), full(1, D)],
        out_specs=specs,
        out_shape=shapes,
        scratch_shapes=[pltpu.SMEM((3, TOP_K, tm), jnp.int32),
                        pltpu.VMEM((2, TOP_K, tm, D // 2), jnp.uint32),
                        pltpu.VMEM((tm, D), f32), pltpu.VMEM((TOP_K, tm, 128), f32),
                        pltpu.SemaphoreType.DMA((3,)), pltpu.SemaphoreType.DMA((2,))],
        compiler_params=_cparams("arbitrary"),
        name="moe_combine",
    )(dest, ys, gate_t, h, hb, s_gate, s_val, s_down, ln_g.reshape(1, D), ln_b.reshape(1, D))


def _moe_layer(h, hb, hp, router_w, router_b, w_up, w_down, layer, s_gate, s_val, s_down,
               ln_g, ln_b, alpha):
    n_tok = h.shape[0]
    idx, gate, rank, counts = _router(h, router_w, router_b)
    src, dest, block_expert, n_used = _routing_tables(idx, rank, counts, n_tok, _pick(n_tok, COMBINE_TOKENS))
    ys = _experts(hp, src, block_expert, n_used, w_up, w_down, layer)
    return _combine(ys, dest, gate.T, h, hb, s_gate, s_val, s_down, ln_g, ln_b, alpha)


def kernel(x, lru_w_in, lru_b_in, lru_conv_w, lru_conv_b, lru_gx_w, lru_gx_b, lru_ga_w, lru_ga_b,
           lru_a_param, lru_w_out, attn_w_qkv, attn_sinks, attn_w_o, ln_g, ln_b,
           moe_router_w, moe_router_b, moe_w_up, moe_w_down, moe_shared_up, moe_shared_down):
    Bsz, T, D = x.shape
    depth = ln_g.shape[0]
    alpha = (2.0 * depth) ** 0.25
    W = lru_w_in.shape[2] // 2
    H = moe_w_down.shape[2]
    n_q = D // HEAD_DIM
    n_kv = (attn_w_qkv.shape[2] // HEAD_DIM - n_q) // 2
    N = Bsz * T
    assert Bsz == 1, "sequence mixers below assume one sequence"

    h = x.reshape(N, D)
    hb = h.astype(bf16)
    hp = None
    for layer in range(depth):
        j = layer // 2
        if layer % 2 == 0:
            w_in = lru_w_in[j].astype(bf16)
            xr, gb = _lru_in(hb, w_in[:, :W], w_in[:, W:],
                             lru_b_in[j, :W].reshape(1, W), lru_b_in[j, W:].reshape(1, W))
            y = _lru_core(xr, gb, lru_conv_w[j], lru_conv_b[j], lru_gx_w[j], lru_gx_b[j].reshape(-1),
                          lru_ga_w[j], lru_ga_b[j].reshape(-1), lru_a_param[j])
            z = _matmul_residual(y, lru_w_out[j].astype(bf16), h, alpha)
        else:
            qkv = _matmul(hb, attn_w_qkv[j].astype(bf16), bf16)
            o = _attention(qkv, attn_sinks[j], n_q, n_kv)
            z = _matmul_residual(o, attn_w_o[j].astype(bf16), h, alpha)
        h, hb, hp = _layer_norm(z, ln_g[layer, 0], ln_b[layer, 0])
        h, hb, hp = _moe_layer(
            h, hb, hp, moe_router_w[layer], moe_router_b[layer], moe_w_up, moe_w_down, layer,
            moe_shared_up[layer, :, :H].astype(bf16), moe_shared_up[layer, :, H:].astype(bf16),
            moe_shared_down[layer].astype(bf16), ln_g[layer, 1], ln_b[layer, 1], alpha)
    return h.reshape(Bsz, T, D)
```

```python
import functools

import jax
import jax.numpy as jnp
from jax import lax
from jax.experimental import pallas as pl
from jax.experimental.pallas import tpu as pltpu

LANES = 128
HEAD_DIM = 128
WINDOW = 128
CONV_WIDTH = 4
LRU_C = 8.0
TOP_K = 8
N_GROUPS = 8
TOPK_GROUPS = 4
ROUTED_SCALE = 2.5
LN_EPS = 1e-5

V7X_VMEM_BYTES = 64 * 1024 * 1024
VMEM_LIMIT = V7X_VMEM_BYTES - 8 * 1024 * 1024
EXPERT_ROWS = 256

f32 = jnp.float32
bf16 = jnp.bfloat16


def _cparams(*sem):
    return pltpu.CompilerParams(dimension_semantics=sem, vmem_limit_bytes=VMEM_LIMIT)


def _pick(n, pref):
    t = min(pref, n)
    while n % t:
        t //= 2
    return t


def _mm_kernel(x_ref, w_ref, o_ref):
    o_ref[...] = jnp.dot(x_ref[...], w_ref[...], preferred_element_type=f32).astype(o_ref.dtype)


def _matmul(x, w, out_dtype):
    M, K = x.shape
    N = w.shape[1]
    tm, tn = _pick(M, 1024), _pick(N, 512)
    return pl.pallas_call(
        _mm_kernel,
        grid=(M // tm, N // tn),
        in_specs=[pl.BlockSpec((tm, K), lambda i, j: (i, 0)),
                  pl.BlockSpec((K, tn), lambda i, j: (0, j))],
        out_specs=pl.BlockSpec((tm, tn), lambda i, j: (i, j)),
        out_shape=jax.ShapeDtypeStruct((M, N), out_dtype),
        compiler_params=_cparams("parallel", "arbitrary"),
        name="matmul",
    )(x, w)


def _mm_res_kernel(x_ref, w_ref, r_ref, o_ref, *, alpha):
    o_ref[...] = alpha * r_ref[...] + jnp.dot(x_ref[...], w_ref[...], preferred_element_type=f32)


def _matmul_residual(x, w, res, alpha):
    M, K = x.shape
    N = w.shape[1]
    tm, tn = _pick(M, 1024), _pick(N, 512)
    return pl.pallas_call(
        functools.partial(_mm_res_kernel, alpha=alpha),
        grid=(M // tm, N // tn),
        in_specs=[pl.BlockSpec((tm, K), lambda i, j: (i, 0)),
                  pl.BlockSpec((K, tn), lambda i, j: (0, j)),
                  pl.BlockSpec((tm, tn), lambda i, j: (i, j))],
        out_specs=pl.BlockSpec((tm, tn), lambda i, j: (i, j)),
        out_shape=jax.ShapeDtypeStruct((M, N), f32),
        compiler_params=_cparams("parallel", "arbitrary"),
        name="matmul_residual",
    )(x, w, res)


def _gelu_tanh(x):
    return 0.5 * x * (1.0 + jnp.tanh(0.7978845608028654 * (x + 0.044715 * (x * x * x))))


def _lru_in_kernel(x_ref, wx_ref, wy_ref, bx_ref, by_ref, xr_ref, gb_ref):
    x = x_ref[...]
    xr_ref[...] = jnp.dot(x, wx_ref[...], preferred_element_type=f32) + bx_ref[...]
    gb_ref[...] = _gelu_tanh(jnp.dot(x, wy_ref[...], preferred_element_type=f32) + by_ref[...])


def _lru_in(x, wx, wy, bx, by):
    M, K = x.shape
    W = wx.shape[1]
    tm, tn = _pick(M, 1024), _pick(W, 256)
    wspec = pl.BlockSpec((K, tn), lambda i, j: (0, j))
    bspec = pl.BlockSpec((1, tn), lambda i, j: (0, j))
    ospec = pl.BlockSpec((tm, tn), lambda i, j: (i, j))
    return pl.pallas_call(
        _lru_in_kernel,
        grid=(M // tm, W // tn),
        in_specs=[pl.BlockSpec((tm, K), lambda i, j: (i, 0)), wspec, wspec, bspec, bspec],
        out_specs=[ospec, ospec],
        out_shape=[jax.ShapeDtypeStruct((M, W), f32)] * 2,
        compiler_params=_cparams("parallel", "arbitrary"),
        name="lru_in",
    )(x, wx, wy, bx, by)


def _layer_norm_rows(z, g, b):
    mu = jnp.mean(z, axis=-1, keepdims=True)
    zc = z - mu
    var = jnp.mean(zc * zc, axis=-1, keepdims=True)
    return zc * lax.rsqrt(var + LN_EPS) * g + b


def _pack_pairs(x):
    half = x.shape[1] // 2
    bits = lax.bitcast_convert_type(x.astype(bf16).astype(f32), jnp.uint32)
    return (bits[:, :half] >> 16) | bits[:, half:]


def _store_packed(x, ref):
    rows = x.shape[0]
    p = _pack_pairs(x)
    s_tiles = p.shape[1] // LANES
    for s in range(s_tiles):
        ref[pl.ds(s, rows, stride=s_tiles), :] = p[:, s * LANES:(s + 1) * LANES]


def _load_packed_tile(ref, lead, s, rows, s_tiles):
    return ref[(*lead, pl.ds(s, rows, stride=s_tiles), slice(None))]


def _store_hidden(h, h_ref, hb_ref, hp_ref):
    h_ref[...] = h
    hb_ref[...] = h.astype(bf16)
    _store_packed(h, hp_ref)


def _ln_kernel(z_ref, g_ref, b_ref, h_ref, hb_ref, hp_ref):
    _store_hidden(_layer_norm_rows(z_ref[...], g_ref[...], b_ref[...]), h_ref, hb_ref, hp_ref)


def _hidden_out(M, D, tm):
    specs = [pl.BlockSpec((tm, D), lambda i: (i, 0)),
             pl.BlockSpec((tm, D), lambda i: (i, 0)),
             pl.BlockSpec((tm * (D // 2 // LANES), LANES), lambda i: (i, 0))]
    shapes = [jax.ShapeDtypeStruct((M, D), f32),
              jax.ShapeDtypeStruct((M, D), bf16),
              jax.ShapeDtypeStruct((M * (D // 2 // LANES), LANES), jnp.uint32)]
    return specs, shapes


def _layer_norm(z, g, b):
    M, D = z.shape
    tm = _pick(M, 256)
    specs, shapes = _hidden_out(M, D, tm)
    vspec = pl.BlockSpec((1, D), lambda i: (0, 0))
    return pl.pallas_call(
        _ln_kernel,
        grid=(M // tm,),
        in_specs=[pl.BlockSpec((tm, D), lambda i: (i, 0)), vspec, vspec],
        out_specs=specs,
        out_shape=shapes,
        compiler_params=_cparams("parallel"),
        name="layer_norm",
    )(z, g.reshape(1, D), b.reshape(1, D))


def _sigmoid(x):
    return 0.5 + 0.5 * jnp.tanh(0.5 * x)


def _softplus(x):
    return jnp.maximum(x, 0.0) + jnp.log1p(jnp.exp(-jnp.abs(x)))


def _lru_core_kernel(xr_ref, gb_ref, cw_ref, cb_ref, gxw_ref, gxb_ref, gaw_ref, gab_ref, ap_ref,
                     y_ref, tail_sc, h_sc, a_sc, b_sc, hs_sc, *, heads, blk):
    tb = xr_ref.shape[0]

    @pl.when(pl.program_id(1) == 0)
    def _():
        tail_sc[...] = jnp.zeros_like(tail_sc)
        h_sc[...] = jnp.zeros_like(h_sc)

    xr = xr_ref[...]
    ext = jnp.concatenate([tail_sc[...], xr], axis=0)
    tail_sc[...] = xr[tb - 8:, :]
    cw = cw_ref[...]
    xc = cb_ref[...]
    for j in range(CONV_WIDTH):
        off = 8 - (CONV_WIDTH - 1) + j
        xc = xc + cw[j:j + 1, :] * ext[off:off + tb, :]
    xcb = xc.astype(bf16)

    for hd in range(heads):
        sl = slice(hd * blk, (hd + 1) * blk)
        xh = xcb[:, sl]
        gate_x = _sigmoid(jnp.dot(xh, gxw_ref[hd], preferred_element_type=f32) + gxb_ref[:, sl])
        gate_a = _sigmoid(jnp.dot(xh, gaw_ref[hd], preferred_element_type=f32) + gab_ref[:, sl])
        a = jnp.exp((-LRU_C * _softplus(-ap_ref[:, sl])) * gate_a)
        a_sc[:, sl] = a
        b_sc[:, sl] = xc[:, sl] * gate_x * jnp.sqrt(1.0 - a * a)

    def body(i, h):
        base = pl.multiple_of(i * 8, 8)
        a8 = a_sc[pl.ds(base, 8), :]
        b8 = b_sc[pl.ds(base, 8), :]
        rows = []
        for r in range(8):
            h = a8[r:r + 1, :] * h + b8[r:r + 1, :]
            rows.append(h)
        hs_sc[pl.ds(base, 8), :] = jnp.concatenate(rows, axis=0)
        return h

    h_sc[0:1, :] = lax.fori_loop(0, tb // 8, body, h_sc[0:1, :])
    y_ref[...] = (hs_sc[...] * gb_ref[...]).astype(y_ref.dtype)


def _lru_core(xr, gb, conv_w, conv_b, gx_w, gx_b, ga_w, ga_b, a_param):
    T, W = xr.shape
    nh, blk = gx_w.shape[0], gx_w.shape[1]
    wb = _pick(W, 1024)
    heads = wb // blk
    tb = _pick(T, 256)
    xspec = pl.BlockSpec((tb, wb), lambda c, t: (t, c))
    vspec = pl.BlockSpec((1, wb), lambda c, t: (0, c))
    gspec = pl.BlockSpec((heads, blk, blk), lambda c, t: (c, 0, 0))
    return pl.pallas_call(
        functools.partial(_lru_core_kernel, heads=heads, blk=blk),
        grid=(W // wb, T // tb),
        in_specs=[xspec, xspec, pl.BlockSpec((CONV_WIDTH, wb), lambda c, t: (0, c)), vspec,
                  gspec, vspec, gspec, vspec, vspec],
        out_specs=xspec,
        out_shape=jax.ShapeDtypeStruct((T, W), bf16),
        scratch_shapes=[pltpu.VMEM((8, wb), f32), pltpu.VMEM((8, wb), f32),
                        pltpu.VMEM((tb, wb), f32), pltpu.VMEM((tb, wb), f32), pltpu.VMEM((tb, wb), f32)],
        compiler_params=_cparams("parallel", "arbitrary"),
        name="lru_core",
    )(xr, gb, conv_w, conv_b.reshape(1, W), gx_w.astype(bf16), gx_b.reshape(1, W),
      ga_w.astype(bf16), ga_b.reshape(1, W), a_param.reshape(1, W))


def _attn_kernel(q_ref, kp_ref, kc_ref, vp_ref, vc_ref, bias_ref, sink_ref, o_ref, *, group, n_kv):
    blk = q_ref.shape[0]
    hd = HEAD_DIM
    col = lax.broadcasted_iota(jnp.int32, (group * blk, 2 * blk), 1)
    no_prev = jnp.logical_and(pl.program_id(0) == 0, col < blk)
    for kv in range(n_kv):
        q = jnp.concatenate([q_ref[:, (kv * group + g) * hd:(kv * group + g + 1) * hd] for g in range(group)],
                            axis=0)
        ksl = slice(kv * hd, (kv + 1) * hd)
        k = jnp.concatenate([kp_ref[:, ksl], kc_ref[:, ksl]], axis=0)
        v = jnp.concatenate([vp_ref[:, ksl], vc_ref[:, ksl]], axis=0)
        s = lax.dot_general(q, k, (((1,), (1,)), ((), ())), preferred_element_type=f32) * (hd ** -0.5)
        logits = jnp.where(no_prev, -jnp.inf, s + bias_ref[kv])
        sink = sink_ref[kv]
        m = jnp.maximum(jnp.max(logits, axis=-1, keepdims=True), sink)
        p = jnp.exp(logits - m)
        probs = p / (jnp.sum(p, axis=-1, keepdims=True) + jnp.exp(sink - m))
        o = jnp.dot(probs.astype(v.dtype), v, preferred_element_type=f32)
        for g in range(group):
            o_ref[:, (kv * group + g) * hd:(kv * group + g + 1) * hd] = (
                o[g * blk:(g + 1) * blk, :].astype(o_ref.dtype))


def _attention(qkv, sinks, n_q, n_kv):
    T = qkv.shape[0]
    group = n_q // n_kv
    blk = WINDOW
    nb = T // blk
    i = jnp.arange(blk)[:, None]
    j = jnp.arange(2 * blk)[None, :]
    dist = blk + i - j
    in_band = (dist >= 0) & (dist < WINDOW)
    slopes = jnp.exp2(-8.0 * jnp.arange(1, n_q + 1, dtype=f32) / n_q)
    bias = jnp.where(in_band[None], -slopes[:, None, None] * dist.astype(f32)[None], -jnp.inf)
    bias = bias.reshape(n_kv, group * blk, 2 * blk)
    sink = jnp.repeat(sinks.astype(f32).reshape(n_kv, group), blk, axis=1).reshape(n_kv, group * blk, 1)

    assert group * n_kv == n_q
    prev = lambda n: jnp.maximum(n - 1, 0)
    kvw = n_kv * HEAD_DIM
    return pl.pallas_call(
        functools.partial(_attn_kernel, group=group, n_kv=n_kv),
        grid=(nb,),
        in_specs=[pl.BlockSpec((blk, n_q * HEAD_DIM), lambda n: (n, 0)),
                  pl.BlockSpec((blk, kvw), lambda n: (prev(n), group)),
                  pl.BlockSpec((blk, kvw), lambda n: (n, group)),
                  pl.BlockSpec((blk, kvw), lambda n: (prev(n), group + 1)),
                  pl.BlockSpec((blk, kvw), lambda n: (n, group + 1)),
                  pl.BlockSpec((n_kv, group * blk, 2 * blk), lambda n: (0, 0, 0)),
                  pl.BlockSpec((n_kv, group * blk, 1), lambda n: (0, 0, 0))],
        out_specs=pl.BlockSpec((blk, n_q * HEAD_DIM), lambda n: (n, 0)),
        out_shape=jax.ShapeDtypeStruct((T, n_q * HEAD_DIM), bf16),
        compiler_params=_cparams("arbitrary"),
        name="swa_attention",
    )(qkv, qkv, qkv, qkv, qkv, bias, sink)


def _first_index_of_max(x, rows):
    m = jnp.max(x, axis=0, keepdims=True)
    idx = jnp.min(jnp.where(x == m, rows, x.shape[0]), axis=0, keepdims=True)
    return m, idx


def _router_kernel(h_ref, rwt_ref, rb_ref, idx_ref, gate_ref, rank_ref, cnt_ref, carry_sc, *, n_exp):
    tm = h_ref.shape[0]
    gsz = n_exp // N_GROUPS

    @pl.when(pl.program_id(0) == 0)
    def _():
        carry_sc[...] = jnp.zeros_like(carry_sc)

    h = h_ref[...]
    h_hi = h.astype(bf16)
    h_lo = (h - h_hi.astype(f32)).astype(bf16)
    nt = (((1,), (1,)), ((), ()))
    p = lax.dot_general(rwt_ref[...], h_hi, nt, preferred_element_type=f32)
    q = lax.dot_general(rwt_ref[:n_exp, :], h_lo, nt, preferred_element_type=f32)
    logits = p[:n_exp, :] + p[n_exp:, :] + q
    scores = jax.nn.sigmoid(logits)
    sel = scores + rb_ref[...]

    grow = lax.broadcasted_iota(jnp.int32, (gsz, tm), 0)
    gscores = []
    for g in range(N_GROUPS):
        sg = sel[g * gsz:(g + 1) * gsz, :]
        m1, i1 = _first_index_of_max(sg, grow)
        m2 = jnp.max(jnp.where(grow == i1, -jnp.inf, sg), axis=0, keepdims=True)
        gscores.append(m1 + m2)
    gs = jnp.concatenate(gscores, axis=0)
    g_iota = lax.broadcasted_iota(jnp.int32, (N_GROUPS, tm), 0)
    gpick = jnp.zeros((N_GROUPS, tm), jnp.bool_)
    for _ in range(TOPK_GROUPS):
        _, gi = _first_index_of_max(jnp.where(gpick, -jnp.inf, gs), g_iota)
        gpick = jnp.logical_or(gpick, g_iota == gi)

    erow = lax.broadcasted_iota(jnp.int32, (n_exp, tm), 0)
    emask = jnp.concatenate(
        [jnp.broadcast_to(gpick[g:g + 1, :], (gsz, tm)) for g in range(N_GROUPS)], axis=0)
    cand = jnp.where(emask, sel, -jnp.inf)
    idxs, gates = [], []
    taken = jnp.zeros((n_exp, tm), jnp.bool_)
    for _ in range(TOP_K):
        _, ei = _first_index_of_max(jnp.where(taken, -jnp.inf, cand), erow)
        hit = erow == ei
        taken = jnp.logical_or(taken, hit)
        idxs.append(ei)
        gates.append(jnp.sum(jnp.where(hit, scores, 0.0), axis=0, keepdims=True))
    gate = jnp.concatenate(gates, axis=0)
    idx_ref[...] = jnp.concatenate(idxs, axis=0)
    gate_ref[...] = gate / jnp.sum(gate, axis=0, keepdims=True) * ROUTED_SCALE

    onehot = jnp.where(taken, 1.0, 0.0).astype(bf16)
    r_i = lax.broadcasted_iota(jnp.int32, (tm, tm), 0)
    c_i = lax.broadcasted_iota(jnp.int32, (tm, tm), 1)
    before = jnp.where(r_i < c_i, 1.0, 0.0).astype(bf16)
    prefix = jnp.dot(onehot, before, preferred_element_type=f32) + carry_sc[:, 0:1]
    ranks = [jnp.sum(jnp.where(erow == idxs[k], prefix, 0.0), axis=0, keepdims=True) for k in range(TOP_K)]
    rank_ref[...] = jnp.concatenate(ranks, axis=0).astype(jnp.int32)

    total = carry_sc[...] + jnp.sum(jnp.where(taken, 1.0, 0.0), axis=1, keepdims=True)
    carry_sc[...] = total
    cnt_ref[...] = total


def _router(h, router_w, router_b):
    N, D = h.shape
    E = router_w.shape[1]
    tm = _pick(N, 512)
    wt = router_w.T
    w_hi = wt.astype(bf16)
    w_split = jnp.concatenate([w_hi, (wt - w_hi.astype(f32)).astype(bf16)], axis=0)
    kspec = pl.BlockSpec((TOP_K, tm), lambda i: (0, i))
    idx, gate, rank, cnt = pl.pallas_call(
        functools.partial(_router_kernel, n_exp=E),
        grid=(N // tm,),
        in_specs=[pl.BlockSpec((tm, D), lambda i: (i, 0)),
                  pl.BlockSpec((2 * E, D), lambda i: (0, 0)),
                  pl.BlockSpec((E, 1), lambda i: (0, 0))],
        out_specs=[kspec, kspec, kspec, pl.BlockSpec((E, 128), lambda i: (0, 0))],
        out_shape=[jax.ShapeDtypeStruct((TOP_K, N), jnp.int32),
                   jax.ShapeDtypeStruct((TOP_K, N), f32),
                   jax.ShapeDtypeStruct((TOP_K, N), jnp.int32),
                   jax.ShapeDtypeStruct((E, 128), f32)],
        scratch_shapes=[pltpu.VMEM((E, 128), f32)],
        compiler_params=_cparams("arbitrary"),
        name="moe_router",
    )(h, w_split, router_b.reshape(E, 1))
    return idx, gate, rank, cnt[:, 0].astype(jnp.int32)


def _routing_tables(idx, rank, counts, n_tok, tile):
    B = EXPERT_ROWS
    E = counts.shape[0]
    A = TOP_K * n_tok
    assert A % B == 0
    n_blocks = A // B + E
    stride = A + B
    blocks = (counts + B - 1) // B
    bend = jnp.cumsum(blocks)
    n_used = bend[-1:].astype(jnp.int32)
    block_ids = jnp.arange(n_blocks, dtype=jnp.int32)
    block_expert = jnp.minimum(jnp.sum(bend[None, :] <= block_ids[:, None], axis=1), E - 1).astype(jnp.int32)

    first_row = ((bend - blocks) * B).astype(jnp.int32)
    onehot = idx[:, :, None] == jnp.arange(E, dtype=jnp.int32)
    dest = rank + jnp.sum(jnp.where(onehot, first_row, 0), axis=-1)
    dest = dest.reshape(TOP_K, n_tok // tile, tile).transpose(1, 0, 2)

    token = jnp.arange(n_tok, dtype=jnp.int32)[None, :]
    real = (idx * stride + token).reshape(-1)
    pad_count = blocks * B - counts
    j = jnp.arange(B, dtype=jnp.int32)[None, :]
    e = jnp.arange(E, dtype=jnp.int32)[:, None]
    pad = jnp.where(j < pad_count[:, None], e * stride + A + j, E * stride)
    keys = lax.sort(jnp.concatenate([real, pad.reshape(-1).astype(jnp.int32)]))
    low = keys % stride
    is_real = jnp.logical_and(low < A, keys < E * stride)
    src = jnp.where(is_real, low, 0).reshape(n_blocks, 1, B)
    return src, dest, block_expert, n_used


def _unpack_pairs(u):
    lo = lax.bitcast_convert_type(u << 16, f32)
    hi = lax.bitcast_convert_type(u & jnp.uint32(0xFFFF0000), f32)
    return lo, hi


def _silu(x):
    return x * jax.nn.sigmoid(x)


def _expert_kernel(be_ref, nu_ref, lay_ref, tab_hbm, hp_hbm, wu_ref, wd_ref, ys_ref,
                   tab_sm, xbuf, wu_sc, wd_sc, sem_t, gsem):
    del lay_ref
    i = pl.program_id(0)
    n_used = nu_ref[0]
    B = tab_sm.shape[2]
    s_tiles = hp_hbm.shape[1]

    def table_copy(j):
        return pltpu.make_async_copy(tab_hbm.at[j], tab_sm.at[j % 3], sem_t.at[j % 3])

    def gather_copy(j, r):
        return pltpu.make_async_copy(hp_hbm.at[tab_sm[j % 3, 0, r]],
                                     xbuf.at[j % 2, pl.ds(r * s_tiles, s_tiles)], gsem.at[j % 2])

    def start_rows(j):
        for r in range(B):
            gather_copy(j, r).start()

    def wait_rows(j):
        pltpu.make_async_copy(xbuf.at[j % 2], xbuf.at[j % 2], gsem.at[j % 2]).wait()

    @pl.when(i == 0)
    def _():
        table_copy(0).start()
        table_copy(0).wait()

        @pl.when(1 < n_used)
        def _():
            table_copy(1).start()

        start_rows(0)

    @pl.when(i + 1 < n_used)
    def _():
        table_copy(i + 1).wait()
        start_rows(i + 1)

    @pl.when(i + 2 < n_used)
    def _():
        table_copy(i + 2).start()

    @pl.when(i < n_used)
    def _():
        @pl.when(jnp.logical_or(i == 0, be_ref[i] != be_ref[jnp.maximum(i - 1, 0)]))
        def _():
            wu_sc[...] = wu_ref[...].astype(bf16)
            wd_sc[...] = wd_ref[...].astype(bf16)

        wait_rows(i)
        tiles = [_unpack_pairs(_load_packed_tile(xbuf, (i % 2,), s, B, s_tiles)) for s in range(s_tiles)]
        lo = jnp.concatenate([t[0].astype(bf16) for t in tiles], axis=1)
        hi = jnp.concatenate([t[1].astype(bf16) for t in tiles], axis=1)
        half = s_tiles * LANES
        hu = (jnp.dot(lo, wu_sc[:half, :], preferred_element_type=f32)
              + jnp.dot(hi, wu_sc[half:, :], preferred_element_type=f32))
        hid = wd_sc.shape[0]
        act = (_silu(hu[:, :hid]) * hu[:, hid:]).astype(bf16)
        _store_packed(jnp.dot(act, wd_sc[...], preferred_element_type=f32), ys_ref)


def _experts(hp, src, block_expert, n_used, w_up, w_down, layer):
    _, E, D, H2 = w_up.shape
    H = H2 // 2
    s_tiles = D // 2 // LANES
    n_tok = hp.shape[0] // s_tiles
    B = EXPERT_ROWS
    n_blocks = src.shape[0]
    live = lambda i, be, nu, lay: (jnp.minimum(i, nu[0] - 1), 0)
    return pl.pallas_call(
        _expert_kernel,
        grid_spec=pltpu.PrefetchScalarGridSpec(
            num_scalar_prefetch=3,
            grid=(n_blocks,),
            in_specs=[pl.BlockSpec(memory_space=pl.ANY),
                      pl.BlockSpec(memory_space=pl.ANY),
                      pl.BlockSpec((None, None, D, H2), lambda i, be, nu, lay: (lay[0], be[i], 0, 0)),
                      pl.BlockSpec((None, None, H, D), lambda i, be, nu, lay: (lay[0], be[i], 0, 0))],
            out_specs=pl.BlockSpec((B * s_tiles, LANES), live),
            scratch_shapes=[pltpu.SMEM((3, 1, B), jnp.int32),
                            pltpu.VMEM((2, B * s_tiles, LANES), jnp.uint32),
                            pltpu.VMEM((D, H2), bf16), pltpu.VMEM((H, D), bf16),
                            pltpu.SemaphoreType.DMA((3,)), pltpu.SemaphoreType.DMA((2,))],
        ),
        out_shape=jax.ShapeDtypeStruct((n_blocks * B * s_tiles, LANES), jnp.uint32),
        compiler_params=_cparams("arbitrary"),
        name="moe_experts",
    )(block_expert, n_used, jnp.full((1,), layer, jnp.int32), src,
      hp.reshape(n_tok, s_tiles, LANES), w_up, w_down)


COMBINE_TOKENS = 128


def _combine_kernel(dest_hbm, ys_hbm, gate_ref, h_ref, hb_ref, sg_ref, sv_ref, sd_ref, g_ref, b_ref,
                    ho_ref, hbo_ref, hpo_ref, tab_sm, ybuf, z_sc, gate_sc, sem_t, gsem, *, alpha):
    i = pl.program_id(0)
    n_tiles = pl.num_programs(0)
    tm, d = h_ref.shape
    half = d // 2
    s_tiles = half // LANES

    def table_copy(j):
        return pltpu.make_async_copy(dest_hbm.at[j], tab_sm.at[j % 3], sem_t.at[j % 3])

    def start_rows(j):
        for k in range(TOP_K):
            for t in range(tm):
                pltpu.make_async_copy(ys_hbm.at[tab_sm[j % 3, k, t]],
                                      ybuf.at[j % 2, k, pl.ds(t * s_tiles, s_tiles)], gsem.at[j % 2]).start()

    def wait_rows(j):
        pltpu.make_async_copy(ybuf.at[j % 2], ybuf.at[j % 2], gsem.at[j % 2]).wait()

    @pl.when(i == 0)
    def _():
        table_copy(0).start()
        table_copy(0).wait()

        @pl.when(1 < n_tiles)
        def _():
            table_copy(1).start()

        start_rows(0)

    @pl.when(i + 1 < n_tiles)
    def _():
        table_copy(i + 1).wait()
        start_rows(i + 1)

    @pl.when(i + 2 < n_tiles)
    def _():
        table_copy(i + 2).start()

    hb = hb_ref[...]
    act = (_silu(jnp.dot(hb, sg_ref[...], preferred_element_type=f32))
           * jnp.dot(hb, sv_ref[...], preferred_element_type=f32)).astype(bf16)
    z_sc[...] = alpha * h_ref[...] + jnp.dot(act, sd_ref[...], preferred_element_type=f32)
    gate = gate_ref[...]
    for k in range(TOP_K):
        gate_sc[k] = jnp.broadcast_to(gate[:, k:k + 1], (tm, LANES))
    wait_rows(i)
    slot = i % 2
    for c in range(s_tiles):
        lo_sl = slice(c * LANES, (c + 1) * LANES)
        hi_sl = slice(half + c * LANES, half + (c + 1) * LANES)
        z_lo, z_hi = z_sc[:, lo_sl], z_sc[:, hi_sl]
        for k in range(TOP_K):
            lo, hi = _unpack_pairs(_load_packed_tile(ybuf, (slot, k), c, tm, s_tiles))
            g = gate_sc[k]
            z_lo = z_lo + lo * g
            z_hi = z_hi + hi * g
        z_sc[:, lo_sl] = z_lo
        z_sc[:, hi_sl] = z_hi
    _store_hidden(_layer_norm_rows(z_sc[...], g_ref[...], b_ref[...]), ho_ref, hbo_ref, hpo_ref)


def _combine(ys, dest, gate_t, h, hb, s_gate, s_val, s_down, ln_g, ln_b, alpha):
    N, D = h.shape
    H = s_gate.shape[1]
    nt, _, tm = dest.shape
    s_tiles = D // 2 // LANES
    specs, shapes = _hidden_out(N, D, tm)
    row = pl.BlockSpec((tm, D), lambda i: (i, 0))
    full = lambda a, b: pl.BlockSpec((a, b), lambda i: (0, 0))
    hbm = pl.BlockSpec(memory_space=pl.ANY)
    return pl.pallas_call(
        functools.partial(_combine_kernel, alpha=alpha),
        grid=(nt,),
        in_specs=[hbm, hbm, pl.BlockSpec((tm, TOP_K), lambda i: (i, 0)),
                  row, row, full(D, H), full(D, H), full(H, D), full(1, D), full(1, D)],
        out_specs=specs,
        out_shape=shapes,
        scratch_shapes=[pltpu.SMEM((3, TOP_K, tm), jnp.int32),
                        pltpu.VMEM((2, TOP_K, tm * s_tiles, LANES), jnp.uint32),
                        pltpu.VMEM((tm, D), f32), pltpu.VMEM((TOP_K, tm, LANES), f32),
                        pltpu.SemaphoreType.DMA((3,)), pltpu.SemaphoreType.DMA((2,))],
        compiler_params=_cparams("arbitrary"),
        name="moe_combine",
    )(dest, ys.reshape(-1, s_tiles, LANES), gate_t, h, hb, s_gate, s_val, s_down,
      ln_g.reshape(1, D), ln_b.reshape(1, D))


def _moe_layer(h, hb, hp, router_w, router_b, w_up, w_down, layer, s_gate, s_val, s_down,
               ln_g, ln_b, alpha):
    n_tok = h.shape[0]
    idx, gate, rank, counts = _router(h, router_w, router_b)
    src, dest, block_expert, n_used = _routing_tables(idx, rank, counts, n_tok, _pick(n_tok, COMBINE_TOKENS))
    ys = _experts(hp, src, block_expert, n_used, w_up, w_down, layer)
    return _combine(ys, dest, gate.T, h, hb, s_gate, s_val, s_down, ln_g, ln_b, alpha)


def kernel(x, lru_w_in, lru_b_in, lru_conv_w, lru_conv_b, lru_gx_w, lru_gx_b, lru_ga_w, lru_ga_b,
           lru_a_param, lru_w_out, attn_w_qkv, attn_sinks, attn_w_o, ln_g, ln_b,
           moe_router_w, moe_router_b, moe_w_up, moe_w_down, moe_shared_up, moe_shared_down):
    Bsz, T, D = x.shape
    depth = ln_g.shape[0]
    alpha = (2.0 * depth) ** 0.25
    W = lru_w_in.shape[2] // 2
    H = moe_w_down.shape[2]
    n_q = D // HEAD_DIM
    n_kv = (attn_w_qkv.shape[2] // HEAD_DIM - n_q) // 2
    N = Bsz * T
    assert Bsz == 1, "sequence mixers below assume one sequence"

    h = x.reshape(N, D)
    hb = h.astype(bf16)
    hp = None
    for layer in range(depth):
        j = layer // 2
        if layer % 2 == 0:
            w_in = lru_w_in[j].astype(bf16)
            xr, gb = _lru_in(hb, w_in[:, :W], w_in[:, W:],
                             lru_b_in[j, :W].reshape(1, W), lru_b_in[j, W:].reshape(1, W))
            y = _lru_core(xr, gb, lru_conv_w[j], lru_conv_b[j], lru_gx_w[j], lru_gx_b[j].reshape(-1),
                          lru_ga_w[j], lru_ga_b[j].reshape(-1), lru_a_param[j])
            z = _matmul_residual(y, lru_w_out[j].astype(bf16), h, alpha)
        else:
            qkv = _matmul(hb, attn_w_qkv[j].astype(bf16), bf16)
            o = _attention(qkv, attn_sinks[j], n_q, n_kv)
            z = _matmul_residual(o, attn_w_o[j].astype(bf16), h, alpha)
        h, hb, hp = _layer_norm(z, ln_g[layer, 0], ln_b[layer, 0])
        h, hb, hp = _moe_layer(
            h, hb, hp, moe_router_w[layer], moe_router_b[layer], moe_w_up, moe_w_down, layer,
            moe_shared_up[layer, :, :H].astype(bf16), moe_shared_up[layer, :, H:].astype(bf16),
            moe_shared_down[layer].astype(bf16), ln_g[layer, 1], ln_b[layer, 1], alpha)
    return h.reshape(Bsz, T, D)
```

```python
import functools

import jax
import jax.numpy as jnp
from jax import lax
from jax.experimental import pallas as pl
from jax.experimental.pallas import tpu as pltpu

HEAD_DIM = 128
WINDOW = 128
CONV_WIDTH = 4
LRU_C = 8.0
TOP_K = 8
N_GROUPS = 8
TOPK_GROUPS = 4
ROUTED_SCALE = 2.5
LN_EPS = 1e-5

V7X_VMEM_BYTES = 64 * 1024 * 1024
VMEM_LIMIT = V7X_VMEM_BYTES - 8 * 1024 * 1024
EXPERT_ROWS = 256

f32 = jnp.float32
bf16 = jnp.bfloat16


def _cparams(*sem):
    return pltpu.CompilerParams(dimension_semantics=sem, vmem_limit_bytes=VMEM_LIMIT)


def _pick(n, pref):
    t = min(pref, n)
    while n % t:
        t //= 2
    return t


def _mm_kernel(x_ref, w_ref, o_ref):
    o_ref[...] = jnp.dot(x_ref[...], w_ref[...], preferred_element_type=f32).astype(o_ref.dtype)


def _matmul(x, w, out_dtype):
    M, K = x.shape
    N = w.shape[1]
    tm, tn = _pick(M, 1024), _pick(N, 512)
    return pl.pallas_call(
        _mm_kernel,
        grid=(M // tm, N // tn),
        in_specs=[pl.BlockSpec((tm, K), lambda i, j: (i, 0)),
                  pl.BlockSpec((K, tn), lambda i, j: (0, j))],
        out_specs=pl.BlockSpec((tm, tn), lambda i, j: (i, j)),
        out_shape=jax.ShapeDtypeStruct((M, N), out_dtype),
        compiler_params=_cparams("parallel", "arbitrary"),
        name="matmul",
    )(x, w)


def _mm_res_kernel(x_ref, w_ref, r_ref, o_ref, *, alpha):
    o_ref[...] = alpha * r_ref[...] + jnp.dot(x_ref[...], w_ref[...], preferred_element_type=f32)


def _matmul_residual(x, w, res, alpha):
    M, K = x.shape
    N = w.shape[1]
    tm, tn = _pick(M, 1024), _pick(N, 512)
    return pl.pallas_call(
        functools.partial(_mm_res_kernel, alpha=alpha),
        grid=(M // tm, N // tn),
        in_specs=[pl.BlockSpec((tm, K), lambda i, j: (i, 0)),
                  pl.BlockSpec((K, tn), lambda i, j: (0, j)),
                  pl.BlockSpec((tm, tn), lambda i, j: (i, j))],
        out_specs=pl.BlockSpec((tm, tn), lambda i, j: (i, j)),
        out_shape=jax.ShapeDtypeStruct((M, N), f32),
        compiler_params=_cparams("parallel", "arbitrary"),
        name="matmul_residual",
    )(x, w, res)


def _gelu_tanh(x):
    return 0.5 * x * (1.0 + jnp.tanh(0.7978845608028654 * (x + 0.044715 * (x * x * x))))


def _lru_in_kernel(x_ref, wx_ref, wy_ref, bx_ref, by_ref, xr_ref, gb_ref):
    x = x_ref[...]
    xr_ref[...] = jnp.dot(x, wx_ref[...], preferred_element_type=f32) + bx_ref[...]
    gb_ref[...] = _gelu_tanh(jnp.dot(x, wy_ref[...], preferred_element_type=f32) + by_ref[...])


def _lru_in(x, wx, wy, bx, by):
    M, K = x.shape
    W = wx.shape[1]
    tm, tn = _pick(M, 1024), _pick(W, 256)
    wspec = pl.BlockSpec((K, tn), lambda i, j: (0, j))
    bspec = pl.BlockSpec((1, tn), lambda i, j: (0, j))
    ospec = pl.BlockSpec((tm, tn), lambda i, j: (i, j))
    return pl.pallas_call(
        _lru_in_kernel,
        grid=(M // tm, W // tn),
        in_specs=[pl.BlockSpec((tm, K), lambda i, j: (i, 0)), wspec, wspec, bspec, bspec],
        out_specs=[ospec, ospec],
        out_shape=[jax.ShapeDtypeStruct((M, W), f32)] * 2,
        compiler_params=_cparams("parallel", "arbitrary"),
        name="lru_in",
    )(x, wx, wy, bx, by)


def _layer_norm_rows(z, g, b):
    mu = jnp.mean(z, axis=-1, keepdims=True)
    zc = z - mu
    var = jnp.mean(zc * zc, axis=-1, keepdims=True)
    return zc * lax.rsqrt(var + LN_EPS) * g + b


def _pack_pairs(x):
    half = x.shape[1] // 2
    bits = lax.bitcast_convert_type(x.astype(bf16).astype(f32), jnp.uint32)
    return (bits[:, :half] >> 16) | bits[:, half:]


def _store_hidden(h, h_ref, hb_ref, hp_ref):
    h_ref[...] = h
    hb_ref[...] = h.astype(bf16)
    hp_ref[...] = _pack_pairs(h)


def _ln_kernel(z_ref, g_ref, b_ref, h_ref, hb_ref, hp_ref):
    _store_hidden(_layer_norm_rows(z_ref[...], g_ref[...], b_ref[...]), h_ref, hb_ref, hp_ref)


def _hidden_out(M, D, tm):
    specs = [pl.BlockSpec((tm, D), lambda i: (i, 0)),
             pl.BlockSpec((tm, D), lambda i: (i, 0)),
             pl.BlockSpec((tm, D // 2), lambda i: (i, 0))]
    shapes = [jax.ShapeDtypeStruct((M, D), f32),
              jax.ShapeDtypeStruct((M, D), bf16),
              jax.ShapeDtypeStruct((M, D // 2), jnp.uint32)]
    return specs, shapes


def _layer_norm(z, g, b):
    M, D = z.shape
    tm = _pick(M, 256)
    specs, shapes = _hidden_out(M, D, tm)
    vspec = pl.BlockSpec((1, D), lambda i: (0, 0))
    return pl.pallas_call(
        _ln_kernel,
        grid=(M // tm,),
        in_specs=[pl.BlockSpec((tm, D), lambda i: (i, 0)), vspec, vspec],
        out_specs=specs,
        out_shape=shapes,
        compiler_params=_cparams("parallel"),
        name="layer_norm",
    )(z, g.reshape(1, D), b.reshape(1, D))


def _sigmoid(x):
    return 0.5 + 0.5 * jnp.tanh(0.5 * x)


def _softplus(x):
    return jnp.maximum(x, 0.0) + jnp.log1p(jnp.exp(-jnp.abs(x)))


def _lru_core_kernel(xr_ref, gb_ref, cw_ref, cb_ref, gxw_ref, gxb_ref, gaw_ref, gab_ref, ap_ref,
                     y_ref, tail_sc, h_sc, a_sc, b_sc, hs_sc, *, heads, blk):
    tb = xr_ref.shape[0]

    @pl.when(pl.program_id(1) == 0)
    def _():
        tail_sc[...] = jnp.zeros_like(tail_sc)
        h_sc[...] = jnp.zeros_like(h_sc)

    xr = xr_ref[...]
    ext = jnp.concatenate([tail_sc[...], xr], axis=0)
    tail_sc[...] = xr[tb - 8:, :]
    cw = cw_ref[...]
    xc = cb_ref[...]
    for j in range(CONV_WIDTH):
        off = 8 - (CONV_WIDTH - 1) + j
        xc = xc + cw[j:j + 1, :] * ext[off:off + tb, :]
    xcb = xc.astype(bf16)

    for hd in range(heads):
        sl = slice(hd * blk, (hd + 1) * blk)
        xh = xcb[:, sl]
        gate_x = _sigmoid(jnp.dot(xh, gxw_ref[hd], preferred_element_type=f32) + gxb_ref[:, sl])
        gate_a = _sigmoid(jnp.dot(xh, gaw_ref[hd], preferred_element_type=f32) + gab_ref[:, sl])
        a = jnp.exp((-LRU_C * _softplus(-ap_ref[:, sl])) * gate_a)
        a_sc[:, sl] = a
        b_sc[:, sl] = xc[:, sl] * gate_x * jnp.sqrt(1.0 - a * a)

    def body(i, h):
        base = pl.multiple_of(i * 8, 8)
        a8 = a_sc[pl.ds(base, 8), :]
        b8 = b_sc[pl.ds(base, 8), :]
        rows = []
        for r in range(8):
            h = a8[r:r + 1, :] * h + b8[r:r + 1, :]
            rows.append(h)
        hs_sc[pl.ds(base, 8), :] = jnp.concatenate(rows, axis=0)
        return h

    h_sc[0:1, :] = lax.fori_loop(0, tb // 8, body, h_sc[0:1, :])
    y_ref[...] = (hs_sc[...] * gb_ref[...]).astype(y_ref.dtype)


def _lru_core(xr, gb, conv_w, conv_b, gx_w, gx_b, ga_w, ga_b, a_param):
    T, W = xr.shape
    nh, blk = gx_w.shape[0], gx_w.shape[1]
    wb = _pick(W, 1024)
    heads = wb // blk
    tb = _pick(T, 256)
    xspec = pl.BlockSpec((tb, wb), lambda c, t: (t, c))
    vspec = pl.BlockSpec((1, wb), lambda c, t: (0, c))
    gspec = pl.BlockSpec((heads, blk, blk), lambda c, t: (c, 0, 0))
    return pl.pallas_call(
        functools.partial(_lru_core_kernel, heads=heads, blk=blk),
        grid=(W // wb, T // tb),
        in_specs=[xspec, xspec, pl.BlockSpec((CONV_WIDTH, wb), lambda c, t: (0, c)), vspec,
                  gspec, vspec, gspec, vspec, vspec],
        out_specs=xspec,
        out_shape=jax.ShapeDtypeStruct((T, W), bf16),
        scratch_shapes=[pltpu.VMEM((8, wb), f32), pltpu.VMEM((8, wb), f32),
                        pltpu.VMEM((tb, wb), f32), pltpu.VMEM((tb, wb), f32), pltpu.VMEM((tb, wb), f32)],
        compiler_params=_cparams("parallel", "arbitrary"),
        name="lru_core",
    )(xr, gb, conv_w, conv_b.reshape(1, W), gx_w.astype(bf16), gx_b.reshape(1, W),
      ga_w.astype(bf16), ga_b.reshape(1, W), a_param.reshape(1, W))


def _attn_kernel(q_ref, kp_ref, kc_ref, vp_ref, vc_ref, bias_ref, sink_ref, o_ref, *, group, n_kv):
    blk = q_ref.shape[0]
    hd = HEAD_DIM
    col = lax.broadcasted_iota(jnp.int32, (group * blk, 2 * blk), 1)
    no_prev = jnp.logical_and(pl.program_id(0) == 0, col < blk)
    for kv in range(n_kv):
        q = jnp.concatenate([q_ref[:, (kv * group + g) * hd:(kv * group + g + 1) * hd] for g in range(group)],
                            axis=0)
        ksl = slice(kv * hd, (kv + 1) * hd)
        k = jnp.concatenate([kp_ref[:, ksl], kc_ref[:, ksl]], axis=0)
        v = jnp.concatenate([vp_ref[:, ksl], vc_ref[:, ksl]], axis=0)
        s = lax.dot_general(q, k, (((1,), (1,)), ((), ())), preferred_element_type=f32) * (hd ** -0.5)
        logits = jnp.where(no_prev, -jnp.inf, s + bias_ref[kv])
        sink = sink_ref[kv]
        m = jnp.maximum(jnp.max(logits, axis=-1, keepdims=True), sink)
        p = jnp.exp(logits - m)
        probs = p / (jnp.sum(p, axis=-1, keepdims=True) + jnp.exp(sink - m))
        o = jnp.dot(probs.astype(v.dtype), v, preferred_element_type=f32)
        for g in range(group):
            o_ref[:, (kv * group + g) * hd:(kv * group + g + 1) * hd] = (
                o[g * blk:(g + 1) * blk, :].astype(o_ref.dtype))


def _attention(qkv, sinks, n_q, n_kv):
    T = qkv.shape[0]
    group = n_q // n_kv
    blk = WINDOW
    nb = T // blk
    i = jnp.arange(blk)[:, None]
    j = jnp.arange(2 * blk)[None, :]
    dist = blk + i - j
    in_band = (dist >= 0) & (dist < WINDOW)
    slopes = jnp.exp2(-8.0 * jnp.arange(1, n_q + 1, dtype=f32) / n_q)
    bias = jnp.where(in_band[None], -slopes[:, None, None] * dist.astype(f32)[None], -jnp.inf)
    bias = bias.reshape(n_kv, group * blk, 2 * blk)
    sink = jnp.repeat(sinks.astype(f32).reshape(n_kv, group), blk, axis=1).reshape(n_kv, group * blk, 1)

    assert group * n_kv == n_q
    prev = lambda n: jnp.maximum(n - 1, 0)
    kvw = n_kv * HEAD_DIM
    return pl.pallas_call(
        functools.partial(_attn_kernel, group=group, n_kv=n_kv),
        grid=(nb,),
        in_specs=[pl.BlockSpec((blk, n_q * HEAD_DIM), lambda n: (n, 0)),
                  pl.BlockSpec((blk, kvw), lambda n: (prev(n), group)),
                  pl.BlockSpec((blk, kvw), lambda n: (n, group)),
                  pl.BlockSpec((blk, kvw), lambda n: (prev(n), group + 1)),
                  pl.BlockSpec((blk, kvw), lambda n: (n, group + 1)),
                  pl.BlockSpec((n_kv, group * blk, 2 * blk), lambda n: (0, 0, 0)),
                  pl.BlockSpec((n_kv, group * blk, 1), lambda n: (0, 0, 0))],
        out_specs=pl.BlockSpec((blk, n_q * HEAD_DIM), lambda n: (n, 0)),
        out_shape=jax.ShapeDtypeStruct((T, n_q * HEAD_DIM), bf16),
        compiler_params=_cparams("arbitrary"),
        name="swa_attention",
    )(qkv, qkv, qkv, qkv, qkv, bias, sink)


def _first_index_of_max(x, rows):
    m = jnp.max(x, axis=0, keepdims=True)
    idx = jnp.min(jnp.where(x == m, rows, x.shape[0]), axis=0, keepdims=True)
    return m, idx


def _router_kernel(h_ref, rwt_ref, rb_ref, idx_ref, gate_ref, rank_ref, cnt_ref, carry_sc, *, n_exp):
    tm = h_ref.shape[0]
    gsz = n_exp // N_GROUPS

    @pl.when(pl.program_id(0) == 0)
    def _():
        carry_sc[...] = jnp.zeros_like(carry_sc)

    h = h_ref[...]
    h_hi = h.astype(bf16)
    h_lo = (h - h_hi.astype(f32)).astype(bf16)
    nt = (((1,), (1,)), ((), ()))
    p = lax.dot_general(rwt_ref[...], h_hi, nt, preferred_element_type=f32)
    q = lax.dot_general(rwt_ref[:n_exp, :], h_lo, nt, preferred_element_type=f32)
    logits = p[:n_exp, :] + p[n_exp:, :] + q
    scores = jax.nn.sigmoid(logits)
    sel = scores + rb_ref[...]

    grow = lax.broadcasted_iota(jnp.int32, (gsz, tm), 0)
    gscores = []
    for g in range(N_GROUPS):
        sg = sel[g * gsz:(g + 1) * gsz, :]
        m1, i1 = _first_index_of_max(sg, grow)
        m2 = jnp.max(jnp.where(grow == i1, -jnp.inf, sg), axis=0, keepdims=True)
        gscores.append(m1 + m2)
    gs = jnp.concatenate(gscores, axis=0)
    g_iota = lax.broadcasted_iota(jnp.int32, (N_GROUPS, tm), 0)
    gpick = jnp.zeros((N_GROUPS, tm), jnp.bool_)
    for _ in range(TOPK_GROUPS):
        _, gi = _first_index_of_max(jnp.where(gpick, -jnp.inf, gs), g_iota)
        gpick = jnp.logical_or(gpick, g_iota == gi)

    erow = lax.broadcasted_iota(jnp.int32, (n_exp, tm), 0)
    emask = jnp.concatenate(
        [jnp.broadcast_to(gpick[g:g + 1, :], (gsz, tm)) for g in range(N_GROUPS)], axis=0)
    cand = jnp.where(emask, sel, -jnp.inf)
    idxs, gates = [], []
    taken = jnp.zeros((n_exp, tm), jnp.bool_)
    for _ in range(TOP_K):
        _, ei = _first_index_of_max(jnp.where(taken, -jnp.inf, cand), erow)
        hit = erow == ei
        taken = jnp.logical_or(taken, hit)
        idxs.append(ei)
        gates.append(jnp.sum(jnp.where(hit, scores, 0.0), axis=0, keepdims=True))
    gate = jnp.concatenate(gates, axis=0)
    idx_ref[...] = jnp.concatenate(idxs, axis=0)
    gate_ref[...] = gate / jnp.sum(gate, axis=0, keepdims=True) * ROUTED_SCALE

    onehot = jnp.where(taken, 1.0, 0.0).astype(bf16)
    r_i = lax.broadcasted_iota(jnp.int32, (tm, tm), 0)
    c_i = lax.broadcasted_iota(jnp.int32, (tm, tm), 1)
    before = jnp.where(r_i < c_i, 1.0, 0.0).astype(bf16)
    prefix = jnp.dot(onehot, before, preferred_element_type=f32) + carry_sc[:, 0:1]
    ranks = [jnp.sum(jnp.where(erow == idxs[k], prefix, 0.0), axis=0, keepdims=True) for k in range(TOP_K)]
    rank_ref[...] = jnp.concatenate(ranks, axis=0).astype(jnp.int32)

    total = carry_sc[...] + jnp.sum(jnp.where(taken, 1.0, 0.0), axis=1, keepdims=True)
    carry_sc[...] = total
    cnt_ref[...] = total


def _router(h, router_w, router_b):
    N, D = h.shape
    E = router_w.shape[1]
    tm = _pick(N, 512)
    wt = router_w.T
    w_hi = wt.astype(bf16)
    w_split = jnp.concatenate([w_hi, (wt - w_hi.astype(f32)).astype(bf16)], axis=0)
    kspec = pl.BlockSpec((TOP_K, tm), lambda i: (0, i))
    idx, gate, rank, cnt = pl.pallas_call(
        functools.partial(_router_kernel, n_exp=E),
        grid=(N // tm,),
        in_specs=[pl.BlockSpec((tm, D), lambda i: (i, 0)),
                  pl.BlockSpec((2 * E, D), lambda i: (0, 0)),
                  pl.BlockSpec((E, 1), lambda i: (0, 0))],
        out_specs=[kspec, kspec, kspec, pl.BlockSpec((E, 128), lambda i: (0, 0))],
        out_shape=[jax.ShapeDtypeStruct((TOP_K, N), jnp.int32),
                   jax.ShapeDtypeStruct((TOP_K, N), f32),
                   jax.ShapeDtypeStruct((TOP_K, N), jnp.int32),
                   jax.ShapeDtypeStruct((E, 128), f32)],
        scratch_shapes=[pltpu.VMEM((E, 128), f32)],
        compiler_params=_cparams("arbitrary"),
        name="moe_router",
    )(h, w_split, router_b.reshape(E, 1))
    return idx, gate, rank, cnt[:, 0].astype(jnp.int32)


def _routing_tables(idx, rank, counts, n_tok, tile):
    B = EXPERT_ROWS
    E = counts.shape[0]
    A = TOP_K * n_tok
    assert A % B == 0
    n_blocks = A // B + E
    stride = A + B
    blocks = (counts + B - 1) // B
    bend = jnp.cumsum(blocks)
    n_used = bend[-1:].astype(jnp.int32)
    block_ids = jnp.arange(n_blocks, dtype=jnp.int32)
    block_expert = jnp.minimum(jnp.sum(bend[None, :] <= block_ids[:, None], axis=1), E - 1).astype(jnp.int32)

    first_row = ((bend - blocks) * B).astype(jnp.int32)
    onehot = idx[:, :, None] == jnp.arange(E, dtype=jnp.int32)
    dest = rank + jnp.sum(jnp.where(onehot, first_row, 0), axis=-1)
    dest = dest.reshape(TOP_K, n_tok // tile, tile).transpose(1, 0, 2)

    token = jnp.arange(n_tok, dtype=jnp.int32)[None, :]
    real = (idx * stride + token).reshape(-1)
    pad_count = blocks * B - counts
    j = jnp.arange(B, dtype=jnp.int32)[None, :]
    e = jnp.arange(E, dtype=jnp.int32)[:, None]
    pad = jnp.where(j < pad_count[:, None], e * stride + A + j, E * stride)
    keys = lax.sort(jnp.concatenate([real, pad.reshape(-1).astype(jnp.int32)]))
    low = keys % stride
    is_real = jnp.logical_and(low < A, keys < E * stride)
    src = jnp.where(is_real, low, 0).reshape(n_blocks, 1, B)
    return src, dest, block_expert, n_used


def _unpack_pairs(u):
    lo = lax.bitcast_convert_type(u << 16, f32)
    hi = lax.bitcast_convert_type(u & jnp.uint32(0xFFFF0000), f32)
    return lo, hi


def _silu(x):
    return x * jax.nn.sigmoid(x)


X_CHUNK = 256


def _expert_kernel(be_ref, nu_ref, lay_ref, tab_hbm, hp_hbm, wu_ref, wd_ref, ys_ref,
                   tab_sm, xbuf, wu_sc, wd_sc, sem_t, gsem):
    del lay_ref
    i = pl.program_id(0)
    n_used = nu_ref[0]
    last_block = pl.num_programs(0) - 1
    B, half = xbuf.shape[1], xbuf.shape[2]
    n_chunks = half // X_CHUNK
    rows_per_chunk = B // n_chunks

    def table_copy(j):
        return pltpu.make_async_copy(tab_hbm.at[jnp.minimum(j, last_block)], tab_sm.at[j % 3], sem_t.at[j % 3])

    def gather_copy(j, r):
        return pltpu.make_async_copy(hp_hbm.at[pl.ds(tab_sm[j % 3, 0, r], 1)],
                                     xbuf.at[j % 2, pl.ds(r, 1)], gsem.at[j % 2])

    def wait_rows(j):
        pltpu.make_async_copy(hp_hbm.at[pl.ds(0, B)], xbuf.at[j % 2], gsem.at[j % 2]).wait()

    @pl.when(i == 0)
    def _():
        table_copy(0).start()
        table_copy(0).wait()
        table_copy(1).start()
        for r in range(B):
            gather_copy(0, r).start()

    @pl.when(i < n_used)
    def _():
        table_copy(i + 1).wait()

        @pl.when(i + 1 < n_used)
        def _():
            table_copy(i + 2).start()

        @pl.when(jnp.logical_or(i == 0, be_ref[i] != be_ref[jnp.maximum(i - 1, 0)]))
        def _():
            wu_sc[...] = wu_ref[...].astype(bf16)
            wd_sc[...] = wd_ref[...].astype(bf16)

        wait_rows(i)
        hu = None
        for c in range(n_chunks):
            cols = slice(c * X_CHUNK, (c + 1) * X_CHUNK)
            lo, hi = _unpack_pairs(xbuf[i % 2, :, cols])
            part = (jnp.dot(lo.astype(bf16), wu_sc[cols, :], preferred_element_type=f32)
                    + jnp.dot(hi.astype(bf16), wu_sc[half + c * X_CHUNK:half + (c + 1) * X_CHUNK, :],
                              preferred_element_type=f32))
            hu = part if hu is None else hu + part
            for r in range(c * rows_per_chunk, (c + 1) * rows_per_chunk):
                gather_copy(i + 1, r).start()
        hid = wd_sc.shape[0]
        act = (_silu(hu[:, :hid]) * hu[:, hid:]).astype(bf16)
        ys_ref[...] = _pack_pairs(jnp.dot(act, wd_sc[...], preferred_element_type=f32))

        @pl.when(i == n_used - 1)
        def _():
            wait_rows(i + 1)


def _experts(hp, src, block_expert, n_used, w_up, w_down, layer):
    N, D2 = hp.shape
    _, E, D, H2 = w_up.shape
    H = H2 // 2
    B = EXPERT_ROWS
    n_blocks = src.shape[0]
    live = lambda i, be, nu, lay: (jnp.minimum(i, nu[0] - 1), 0)
    return pl.pallas_call(
        _expert_kernel,
        grid_spec=pltpu.PrefetchScalarGridSpec(
            num_scalar_prefetch=3,
            grid=(n_blocks,),
            in_specs=[pl.BlockSpec(memory_space=pl.ANY),
                      pl.BlockSpec(memory_space=pl.ANY),
                      pl.BlockSpec((None, None, D, H2), lambda i, be, nu, lay: (lay[0], be[i], 0, 0)),
                      pl.BlockSpec((None, None, H, D), lambda i, be, nu, lay: (lay[0], be[i], 0, 0))],
            out_specs=pl.BlockSpec((B, D2), live),
            scratch_shapes=[pltpu.SMEM((3, 1, B), jnp.int32),
                            pltpu.VMEM((2, B, D2), jnp.uint32),
                            pltpu.VMEM((D, H2), bf16), pltpu.VMEM((H, D), bf16),
                            pltpu.SemaphoreType.DMA((3,)), pltpu.SemaphoreType.DMA((2,))],
        ),
        out_shape=jax.ShapeDtypeStruct((n_blocks * B, D2), jnp.uint32),
        compiler_params=_cparams("arbitrary"),
        name="moe_experts",
    )(block_expert, n_used, jnp.full((1,), layer, jnp.int32), src, hp, w_up, w_down)


COMBINE_TOKENS = 128


def _combine_kernel(dest_hbm, ys_hbm, gate_ref, h_ref, hb_ref, sg_ref, sv_ref, sd_ref, g_ref, b_ref,
                    ho_ref, hbo_ref, hpo_ref, tab_sm, ybuf, z_sc, gate_sc, sem_t, gsem, fence, *, alpha):
    i = pl.program_id(0)
    last_tile = pl.num_programs(0) - 1
    tm, d = h_ref.shape
    half = d // 2
    lanes = gate_sc.shape[2]
    n_chunks = half // lanes
    copies_per_chunk = TOP_K * tm // n_chunks

    def table_copy(j):
        return pltpu.make_async_copy(dest_hbm.at[jnp.minimum(j, last_tile)], tab_sm.at[j % 3], sem_t.at[j % 3])

    def row_copy(j, q):
        k, t = q // tm, q % tm
        return pltpu.make_async_copy(ys_hbm.at[pl.ds(tab_sm[j % 3, k, t], 1)],
                                     ybuf.at[j % 2, k, pl.ds(t, 1)], gsem.at[j % 2])

    def wait_rows(j):
        for k in range(TOP_K):
            pltpu.make_async_copy(ys_hbm.at[pl.ds(0, tm)], ybuf.at[j % 2, k], gsem.at[j % 2]).wait()

    @pl.when(i == 0)
    def _():
        table_copy(0).start()
        table_copy(0).wait()
        table_copy(1).start()
        for q in range(TOP_K * tm):
            row_copy(0, q).start()

    table_copy(i + 1).wait()

    @pl.when(i < last_tile)
    def _():
        table_copy(i + 2).start()

    hb = hb_ref[...]
    act = (_silu(jnp.dot(hb, sg_ref[...], preferred_element_type=f32))
           * jnp.dot(hb, sv_ref[...], preferred_element_type=f32)).astype(bf16)
    z_sc[...] = alpha * h_ref[...] + jnp.dot(act, sd_ref[...], preferred_element_type=f32)
    gate = gate_ref[...]
    for k in range(TOP_K):
        gate_sc[k] = jnp.broadcast_to(gate[:, k:k + 1], (tm, lanes))
    wait_rows(i)
    slot = i % 2
    for c in range(n_chunks):
        lo_sl = slice(c * lanes, (c + 1) * lanes)
        hi_sl = slice(half + c * lanes, half + (c + 1) * lanes)
        z_lo, z_hi = z_sc[:, lo_sl], z_sc[:, hi_sl]
        for k in range(TOP_K):
            lo, hi = _unpack_pairs(ybuf[slot, k, :, lo_sl])
            g = gate_sc[k]
            z_lo = z_lo + lo * g
            z_hi = z_hi + hi * g
        z_sc[:, lo_sl] = z_lo
        z_sc[:, hi_sl] = z_hi
        for q in range(c * copies_per_chunk, (c + 1) * copies_per_chunk):
            row_copy(i + 1, q).start()
        pl.semaphore_signal(fence, 1)
        pl.semaphore_wait(fence, 1)
    _store_hidden(_layer_norm_rows(z_sc[...], g_ref[...], b_ref[...]), ho_ref, hbo_ref, hpo_ref)

    @pl.when(i == last_tile)
    def _():
        wait_rows(i + 1)


def _combine(ys, dest, gate_t, h, hb, s_gate, s_val, s_down, ln_g, ln_b, alpha):
    N, D = h.shape
    H = s_gate.shape[1]
    nt, _, tm = dest.shape
    specs, shapes = _hidden_out(N, D, tm)
    row = pl.BlockSpec((tm, D), lambda i: (i, 0))
    full = lambda a, b: pl.BlockSpec((a, b), lambda i: (0, 0))
    hbm = pl.BlockSpec(memory_space=pl.ANY)
    return pl.pallas_call(
        functools.partial(_combine_kernel, alpha=alpha),
        grid=(nt,),
        in_specs=[hbm, hbm, pl.BlockSpec((tm, TOP_K), lambda i: (i, 0)),
                  row, row, full(D, H), full(D, H), full(H, D), full(1, D), full(1, D)],
        out_specs=specs,
        out_shape=shapes,
        scratch_shapes=[pltpu.SMEM((3, TOP_K, tm), jnp.int32),
                        pltpu.VMEM((2, TOP_K, tm, D // 2), jnp.uint32),
                        pltpu.VMEM((tm, D), f32), pltpu.VMEM((TOP_K, tm, 128), f32),
                        pltpu.SemaphoreType.DMA((3,)), pltpu.SemaphoreType.DMA((2,)),
                        pltpu.SemaphoreType.REGULAR],
        compiler_params=_cparams("arbitrary"),
        name="moe_combine",
    )(dest, ys, gate_t, h, hb, s_gate, s_val, s_down, ln_g.reshape(1, D), ln_b.reshape(1, D))


def _moe_layer(h, hb, hp, router_w, router_b, w_up, w_down, layer, s_gate, s_val, s_down,
               ln_g, ln_b, alpha):
    n_tok = h.shape[0]
    idx, gate, rank, counts = _router(h, router_w, router_b)
    src, dest, block_expert, n_used = _routing_tables(idx, rank, counts, n_tok, _pick(n_tok, COMBINE_TOKENS))
    ys = _experts(hp, src, block_expert, n_used, w_up, w_down, layer)
    return _combine(ys, dest, gate.T, h, hb, s_gate, s_val, s_down, ln_g, ln_b, alpha)


def kernel(x, lru_w_in, lru_b_in, lru_conv_w, lru_conv_b, lru_gx_w, lru_gx_b, lru_ga_w, lru_ga_b,
           lru_a_param, lru_w_out, attn_w_qkv, attn_sinks, attn_w_o, ln_g, ln_b,
           moe_router_w, moe_router_b, moe_w_up, moe_w_down, moe_shared_up, moe_shared_down):
    Bsz, T, D = x.shape
    depth = ln_g.shape[0]
    alpha = (2.0 * depth) ** 0.25
    W = lru_w_in.shape[2] // 2
    H = moe_w_down.shape[2]
    n_q = D // HEAD_DIM
    n_kv = (attn_w_qkv.shape[2] // HEAD_DIM - n_q) // 2
    N = Bsz * T
    assert Bsz == 1, "sequence mixers below assume one sequence"

    h = x.reshape(N, D)
    hb = h.astype(bf16)
    hp = None
    for layer in range(depth):
        j = layer // 2
        if layer % 2 == 0:
            w_in = lru_w_in[j].astype(bf16)
            xr, gb = _lru_in(hb, w_in[:, :W], w_in[:, W:],
                             lru_b_in[j, :W].reshape(1, W), lru_b_in[j, W:].reshape(1, W))
            y = _lru_core(xr, gb, lru_conv_w[j], lru_conv_b[j], lru_gx_w[j], lru_gx_b[j].reshape(-1),
                          lru_ga_w[j], lru_ga_b[j].reshape(-1), lru_a_param[j])
            z = _matmul_residual(y, lru_w_out[j].astype(bf16), h, alpha)
        else:
            qkv = _matmul(hb, attn_w_qkv[j].astype(bf16), bf16)
            o = _attention(qkv, attn_sinks[j], n_q, n_kv)
            z = _matmul_residual(o, attn_w_o[j].astype(bf16), h, alpha)
        h, hb, hp = _layer_norm(z, ln_g[layer, 0], ln_b[layer, 0])
        h, hb, hp = _moe_layer(
            h, hb, hp, moe_router_w[layer], moe_router_b[layer], moe_w_up, moe_w_down, layer,
            moe_shared_up[layer, :, :H].astype(bf16), moe_shared_up[layer, :, H:].astype(bf16),
            moe_shared_down[layer].astype(bf16), ln_g[layer, 1], ln_b[layer, 1], alpha)
    return h.reshape(Bsz, T, D)
```

```python
import functools

import jax
import jax.numpy as jnp
from jax import lax
from jax.experimental import pallas as pl
from jax.experimental.pallas import tpu as pltpu

HEAD_DIM = 128
WINDOW = 128
CONV_WIDTH = 4
LRU_C = 8.0
TOP_K = 8
N_GROUPS = 8
TOPK_GROUPS = 4
ROUTED_SCALE = 2.5
LN_EPS = 1e-5

V7X_VMEM_BYTES = 64 * 1024 * 1024
VMEM_LIMIT = V7X_VMEM_BYTES - 8 * 1024 * 1024
EXPERT_ROWS = 256

f32 = jnp.float32
bf16 = jnp.bfloat16


def _cparams(*sem):
    return pltpu.CompilerParams(dimension_semantics=sem, vmem_limit_bytes=VMEM_LIMIT)


def _pick(n, pref):
    t = min(pref, n)
    while n % t:
        t //= 2
    return t


def _mm_kernel(x_ref, w_ref, o_ref):
    o_ref[...] = jnp.dot(x_ref[...], w_ref[...], preferred_element_type=f32).astype(o_ref.dtype)


def _matmul(x, w, out_dtype):
    M, K = x.shape
    N = w.shape[1]
    tm, tn = _pick(M, 1024), _pick(N, 512)
    return pl.pallas_call(
        _mm_kernel,
        grid=(M // tm, N // tn),
        in_specs=[pl.BlockSpec((tm, K), lambda i, j: (i, 0)),
                  pl.BlockSpec((K, tn), lambda i, j: (0, j))],
        out_specs=pl.BlockSpec((tm, tn), lambda i, j: (i, j)),
        out_shape=jax.ShapeDtypeStruct((M, N), out_dtype),
        compiler_params=_cparams("parallel", "arbitrary"),
        name="matmul",
    )(x, w)


def _mm_res_kernel(x_ref, w_ref, r_ref, o_ref, *, alpha):
    o_ref[...] = alpha * r_ref[...] + jnp.dot(x_ref[...], w_ref[...], preferred_element_type=f32)


def _matmul_residual(x, w, res, alpha):
    M, K = x.shape
    N = w.shape[1]
    tm, tn = _pick(M, 1024), _pick(N, 512)
    return pl.pallas_call(
        functools.partial(_mm_res_kernel, alpha=alpha),
        grid=(M // tm, N // tn),
        in_specs=[pl.BlockSpec((tm, K), lambda i, j: (i, 0)),
                  pl.BlockSpec((K, tn), lambda i, j: (0, j)),
                  pl.BlockSpec((tm, tn), lambda i, j: (i, j))],
        out_specs=pl.BlockSpec((tm, tn), lambda i, j: (i, j)),
        out_shape=jax.ShapeDtypeStruct((M, N), f32),
        compiler_params=_cparams("parallel", "arbitrary"),
        name="matmul_residual",
    )(x, w, res)


def _gelu_tanh(x):
    return 0.5 * x * (1.0 + jnp.tanh(0.7978845608028654 * (x + 0.044715 * (x * x * x))))


def _lru_in_kernel(x_ref, wx_ref, wy_ref, bx_ref, by_ref, xr_ref, gb_ref):
    x = x_ref[...]
    xr_ref[...] = jnp.dot(x, wx_ref[...], preferred_element_type=f32) + bx_ref[...]
    gb_ref[...] = _gelu_tanh(jnp.dot(x, wy_ref[...], preferred_element_type=f32) + by_ref[...])


def _lru_in(x, wx, wy, bx, by):
    M, K = x.shape
    W = wx.shape[1]
    tm, tn = _pick(M, 1024), _pick(W, 256)
    wspec = pl.BlockSpec((K, tn), lambda i, j: (0, j))
    bspec = pl.BlockSpec((1, tn), lambda i, j: (0, j))
    ospec = pl.BlockSpec((tm, tn), lambda i, j: (i, j))
    return pl.pallas_call(
        _lru_in_kernel,
        grid=(M // tm, W // tn),
        in_specs=[pl.BlockSpec((tm, K), lambda i, j: (i, 0)), wspec, wspec, bspec, bspec],
        out_specs=[ospec, ospec],
        out_shape=[jax.ShapeDtypeStruct((M, W), f32)] * 2,
        compiler_params=_cparams("parallel", "arbitrary"),
        name="lru_in",
    )(x, wx, wy, bx, by)


def _layer_norm_rows(z, g, b):
    mu = jnp.mean(z, axis=-1, keepdims=True)
    zc = z - mu
    var = jnp.mean(zc * zc, axis=-1, keepdims=True)
    return zc * lax.rsqrt(var + LN_EPS) * g + b


def _pack_pairs(x):
    half = x.shape[1] // 2
    bits = lax.bitcast_convert_type(x.astype(bf16).astype(f32), jnp.uint32)
    return (bits[:, :half] >> 16) | bits[:, half:]


def _store_hidden(h, h_ref, hb_ref, hp_ref):
    h_ref[...] = h
    hb_ref[...] = h.astype(bf16)
    hp_ref[...] = _pack_pairs(h)


def _ln_kernel(z_ref, g_ref, b_ref, h_ref, hb_ref, hp_ref):
    _store_hidden(_layer_norm_rows(z_ref[...], g_ref[...], b_ref[...]), h_ref, hb_ref, hp_ref)


def _hidden_out(M, D, tm):
    specs = [pl.BlockSpec((tm, D), lambda i: (i, 0)),
             pl.BlockSpec((tm, D), lambda i: (i, 0)),
             pl.BlockSpec((tm, D // 2), lambda i: (i, 0))]
    shapes = [jax.ShapeDtypeStruct((M, D), f32),
              jax.ShapeDtypeStruct((M, D), bf16),
              jax.ShapeDtypeStruct((M, D // 2), jnp.uint32)]
    return specs, shapes


def _layer_norm(z, g, b):
    M, D = z.shape
    tm = _pick(M, 256)
    specs, shapes = _hidden_out(M, D, tm)
    vspec = pl.BlockSpec((1, D), lambda i: (0, 0))
    return pl.pallas_call(
        _ln_kernel,
        grid=(M // tm,),
        in_specs=[pl.BlockSpec((tm, D), lambda i: (i, 0)), vspec, vspec],
        out_specs=specs,
        out_shape=shapes,
        compiler_params=_cparams("parallel"),
        name="layer_norm",
    )(z, g.reshape(1, D), b.reshape(1, D))


def _sigmoid(x):
    return 0.5 + 0.5 * jnp.tanh(0.5 * x)


def _softplus(x):
    return jnp.maximum(x, 0.0) + jnp.log1p(jnp.exp(-jnp.abs(x)))


def _lru_core_kernel(xr_ref, gb_ref, cw_ref, cb_ref, gxw_ref, gxb_ref, gaw_ref, gab_ref, ap_ref,
                     y_ref, tail_sc, h_sc, a_sc, b_sc, hs_sc, *, heads, blk):
    tb = xr_ref.shape[0]

    @pl.when(pl.program_id(1) == 0)
    def _():
        tail_sc[...] = jnp.zeros_like(tail_sc)
        h_sc[...] = jnp.zeros_like(h_sc)

    xr = xr_ref[...]
    ext = jnp.concatenate([tail_sc[...], xr], axis=0)
    tail_sc[...] = xr[tb - 8:, :]
    cw = cw_ref[...]
    xc = cb_ref[...]
    for j in range(CONV_WIDTH):
        off = 8 - (CONV_WIDTH - 1) + j
        xc = xc + cw[j:j + 1, :] * ext[off:off + tb, :]
    xcb = xc.astype(bf16)

    for hd in range(heads):
        sl = slice(hd * blk, (hd + 1) * blk)
        xh = xcb[:, sl]
        gate_x = _sigmoid(jnp.dot(xh, gxw_ref[hd], preferred_element_type=f32) + gxb_ref[:, sl])
        gate_a = _sigmoid(jnp.dot(xh, gaw_ref[hd], preferred_element_type=f32) + gab_ref[:, sl])
        a = jnp.exp((-LRU_C * _softplus(-ap_ref[:, sl])) * gate_a)
        a_sc[:, sl] = a
        b_sc[:, sl] = xc[:, sl] * gate_x * jnp.sqrt(1.0 - a * a)

    def body(i, h):
        base = pl.multiple_of(i * 8, 8)
        a8 = a_sc[pl.ds(base, 8), :]
        b8 = b_sc[pl.ds(base, 8), :]
        rows = []
        for r in range(8):
            h = a8[r:r + 1, :] * h + b8[r:r + 1, :]
            rows.append(h)
        hs_sc[pl.ds(base, 8), :] = jnp.concatenate(rows, axis=0)
        return h

    h_sc[0:1, :] = lax.fori_loop(0, tb // 8, body, h_sc[0:1, :])
    y_ref[...] = (hs_sc[...] * gb_ref[...]).astype(y_ref.dtype)


def _lru_core(xr, gb, conv_w, conv_b, gx_w, gx_b, ga_w, ga_b, a_param):
    T, W = xr.shape
    nh, blk = gx_w.shape[0], gx_w.shape[1]
    wb = _pick(W, 1024)
    heads = wb // blk
    tb = _pick(T, 256)
    xspec = pl.BlockSpec((tb, wb), lambda c, t: (t, c))
    vspec = pl.BlockSpec((1, wb), lambda c, t: (0, c))
    gspec = pl.BlockSpec((heads, blk, blk), lambda c, t: (c, 0, 0))
    return pl.pallas_call(
        functools.partial(_lru_core_kernel, heads=heads, blk=blk),
        grid=(W // wb, T // tb),
        in_specs=[xspec, xspec, pl.BlockSpec((CONV_WIDTH, wb), lambda c, t: (0, c)), vspec,
                  gspec, vspec, gspec, vspec, vspec],
        out_specs=xspec,
        out_shape=jax.ShapeDtypeStruct((T, W), bf16),
        scratch_shapes=[pltpu.VMEM((8, wb), f32), pltpu.VMEM((8, wb), f32),
                        pltpu.VMEM((tb, wb), f32), pltpu.VMEM((tb, wb), f32), pltpu.VMEM((tb, wb), f32)],
        compiler_params=_cparams("parallel", "arbitrary"),
        name="lru_core",
    )(xr, gb, conv_w, conv_b.reshape(1, W), gx_w.astype(bf16), gx_b.reshape(1, W),
      ga_w.astype(bf16), ga_b.reshape(1, W), a_param.reshape(1, W))


def _attn_kernel(q_ref, kp_ref, kc_ref, vp_ref, vc_ref, bias_ref, sink_ref, o_ref, *, group, n_kv):
    blk = q_ref.shape[0]
    hd = HEAD_DIM
    col = lax.broadcasted_iota(jnp.int32, (group * blk, 2 * blk), 1)
    no_prev = jnp.logical_and(pl.program_id(0) == 0, col < blk)
    for kv in range(n_kv):
        q = jnp.concatenate([q_ref[:, (kv * group + g) * hd:(kv * group + g + 1) * hd] for g in range(group)],
                            axis=0)
        ksl = slice(kv * hd, (kv + 1) * hd)
        k = jnp.concatenate([kp_ref[:, ksl], kc_ref[:, ksl]], axis=0)
        v = jnp.concatenate([vp_ref[:, ksl], vc_ref[:, ksl]], axis=0)
        s = lax.dot_general(q, k, (((1,), (1,)), ((), ())), preferred_element_type=f32) * (hd ** -0.5)
        logits = jnp.where(no_prev, -jnp.inf, s + bias_ref[kv])
        sink = sink_ref[kv]
        m = jnp.maximum(jnp.max(logits, axis=-1, keepdims=True), sink)
        p = jnp.exp(logits - m)
        probs = p / (jnp.sum(p, axis=-1, keepdims=True) + jnp.exp(sink - m))
        o = jnp.dot(probs.astype(v.dtype), v, preferred_element_type=f32)
        for g in range(group):
            o_ref[:, (kv * group + g) * hd:(kv * group + g + 1) * hd] = (
                o[g * blk:(g + 1) * blk, :].astype(o_ref.dtype))


def _attention(qkv, sinks, n_q, n_kv):
    T = qkv.shape[0]
    group = n_q // n_kv
    blk = WINDOW
    nb = T // blk
    i = jnp.arange(blk)[:, None]
    j = jnp.arange(2 * blk)[None, :]
    dist = blk + i - j
    in_band = (dist >= 0) & (dist < WINDOW)
    slopes = jnp.exp2(-8.0 * jnp.arange(1, n_q + 1, dtype=f32) / n_q)
    bias = jnp.where(in_band[None], -slopes[:, None, None] * dist.astype(f32)[None], -jnp.inf)
    bias = bias.reshape(n_kv, group * blk, 2 * blk)
    sink = jnp.repeat(sinks.astype(f32).reshape(n_kv, group), blk, axis=1).reshape(n_kv, group * blk, 1)

    assert group * n_kv == n_q
    prev = lambda n: jnp.maximum(n - 1, 0)
    kvw = n_kv * HEAD_DIM
    return pl.pallas_call(
        functools.partial(_attn_kernel, group=group, n_kv=n_kv),
        grid=(nb,),
        in_specs=[pl.BlockSpec((blk, n_q * HEAD_DIM), lambda n: (n, 0)),
                  pl.BlockSpec((blk, kvw), lambda n: (prev(n), group)),
                  pl.BlockSpec((blk, kvw), lambda n: (n, group)),
                  pl.BlockSpec((blk, kvw), lambda n: (prev(n), group + 1)),
                  pl.BlockSpec((blk, kvw), lambda n: (n, group + 1)),
                  pl.BlockSpec((n_kv, group * blk, 2 * blk), lambda n: (0, 0, 0)),
                  pl.BlockSpec((n_kv, group * blk, 1), lambda n: (0, 0, 0))],
        out_specs=pl.BlockSpec((blk, n_q * HEAD_DIM), lambda n: (n, 0)),
        out_shape=jax.ShapeDtypeStruct((T, n_q * HEAD_DIM), bf16),
        compiler_params=_cparams("arbitrary"),
        name="swa_attention",
    )(qkv, qkv, qkv, qkv, qkv, bias, sink)


def _first_index_of_max(x, rows):
    m = jnp.max(x, axis=0, keepdims=True)
    idx = jnp.min(jnp.where(x == m, rows, x.shape[0]), axis=0, keepdims=True)
    return m, idx


def _router_kernel(h_ref, rwt_ref, rb_ref, idx_ref, gate_ref, rank_ref, cnt_ref, carry_sc, *, n_exp):
    tm = h_ref.shape[0]
    gsz = n_exp // N_GROUPS

    @pl.when(pl.program_id(0) == 0)
    def _():
        carry_sc[...] = jnp.zeros_like(carry_sc)

    h = h_ref[...]
    h_hi = h.astype(bf16)
    h_lo = (h - h_hi.astype(f32)).astype(bf16)
    nt = (((1,), (1,)), ((), ()))
    p = lax.dot_general(rwt_ref[...], h_hi, nt, preferred_element_type=f32)
    q = lax.dot_general(rwt_ref[:n_exp, :], h_lo, nt, preferred_element_type=f32)
    logits = p[:n_exp, :] + p[n_exp:, :] + q
    scores = jax.nn.sigmoid(logits)
    sel = scores + rb_ref[...]

    grow = lax.broadcasted_iota(jnp.int32, (gsz, tm), 0)
    gscores = []
    for g in range(N_GROUPS):
        sg = sel[g * gsz:(g + 1) * gsz, :]
        m1, i1 = _first_index_of_max(sg, grow)
        m2 = jnp.max(jnp.where(grow == i1, -jnp.inf, sg), axis=0, keepdims=True)
        gscores.append(m1 + m2)
    gs = jnp.concatenate(gscores, axis=0)
    g_iota = lax.broadcasted_iota(jnp.int32, (N_GROUPS, tm), 0)
    gpick = jnp.zeros((N_GROUPS, tm), jnp.bool_)
    for _ in range(TOPK_GROUPS):
        _, gi = _first_index_of_max(jnp.where(gpick, -jnp.inf, gs), g_iota)
        gpick = jnp.logical_or(gpick, g_iota == gi)

    erow = lax.broadcasted_iota(jnp.int32, (n_exp, tm), 0)
    emask = jnp.concatenate(
        [jnp.broadcast_to(gpick[g:g + 1, :], (gsz, tm)) for g in range(N_GROUPS)], axis=0)
    cand = jnp.where(emask, sel, -jnp.inf)
    idxs, gates = [], []
    taken = jnp.zeros((n_exp, tm), jnp.bool_)
    for _ in range(TOP_K):
        _, ei = _first_index_of_max(jnp.where(taken, -jnp.inf, cand), erow)
        hit = erow == ei
        taken = jnp.logical_or(taken, hit)
        idxs.append(ei)
        gates.append(jnp.sum(jnp.where(hit, scores, 0.0), axis=0, keepdims=True))
    gate = jnp.concatenate(gates, axis=0)
    idx_ref[...] = jnp.concatenate(idxs, axis=0)
    gate_ref[...] = gate / jnp.sum(gate, axis=0, keepdims=True) * ROUTED_SCALE

    onehot = jnp.where(taken, 1.0, 0.0).astype(bf16)
    r_i = lax.broadcasted_iota(jnp.int32, (tm, tm), 0)
    c_i = lax.broadcasted_iota(jnp.int32, (tm, tm), 1)
    before = jnp.where(r_i < c_i, 1.0, 0.0).astype(bf16)
    prefix = jnp.dot(onehot, before, preferred_element_type=f32) + carry_sc[:, 0:1]
    ranks = [jnp.sum(jnp.where(erow == idxs[k], prefix, 0.0), axis=0, keepdims=True) for k in range(TOP_K)]
    rank_ref[...] = jnp.concatenate(ranks, axis=0).astype(jnp.int32)

    total = carry_sc[...] + jnp.sum(jnp.where(taken, 1.0, 0.0), axis=1, keepdims=True)
    carry_sc[...] = total
    cnt_ref[...] = total


def _router(h, router_w, router_b):
    N, D = h.shape
    E = router_w.shape[1]
    tm = _pick(N, 512)
    wt = router_w.T
    w_hi = wt.astype(bf16)
    w_split = jnp.concatenate([w_hi, (wt - w_hi.astype(f32)).astype(bf16)], axis=0)
    kspec = pl.BlockSpec((TOP_K, tm), lambda i: (0, i))
    idx, gate, rank, cnt = pl.pallas_call(
        functools.partial(_router_kernel, n_exp=E),
        grid=(N // tm,),
        in_specs=[pl.BlockSpec((tm, D), lambda i: (i, 0)),
                  pl.BlockSpec((2 * E, D), lambda i: (0, 0)),
                  pl.BlockSpec((E, 1), lambda i: (0, 0))],
        out_specs=[kspec, kspec, kspec, pl.BlockSpec((E, 128), lambda i: (0, 0))],
        out_shape=[jax.ShapeDtypeStruct((TOP_K, N), jnp.int32),
                   jax.ShapeDtypeStruct((TOP_K, N), f32),
                   jax.ShapeDtypeStruct((TOP_K, N), jnp.int32),
                   jax.ShapeDtypeStruct((E, 128), f32)],
        scratch_shapes=[pltpu.VMEM((E, 128), f32)],
        compiler_params=_cparams("arbitrary"),
        name="moe_router",
    )(h, w_split, router_b.reshape(E, 1))
    return idx, gate, rank, cnt[:, 0].astype(jnp.int32)


def _routing_tables(idx, rank, counts, n_tok, tile):
    B = EXPERT_ROWS
    E = counts.shape[0]
    A = TOP_K * n_tok
    assert A % B == 0
    n_blocks = A // B + E
    stride = A + B
    blocks = (counts + B - 1) // B
    bend = jnp.cumsum(blocks)
    n_used = bend[-1:].astype(jnp.int32)
    block_ids = jnp.arange(n_blocks, dtype=jnp.int32)
    block_expert = jnp.minimum(jnp.sum(bend[None, :] <= block_ids[:, None], axis=1), E - 1).astype(jnp.int32)

    first_row = ((bend - blocks) * B).astype(jnp.int32)
    onehot = idx[:, :, None] == jnp.arange(E, dtype=jnp.int32)
    dest = rank + jnp.sum(jnp.where(onehot, first_row, 0), axis=-1)
    dest = dest.reshape(TOP_K, n_tok // tile, tile).transpose(1, 0, 2)

    token = jnp.arange(n_tok, dtype=jnp.int32)[None, :]
    real = (idx * stride + token).reshape(-1)
    pad_count = blocks * B - counts
    j = jnp.arange(B, dtype=jnp.int32)[None, :]
    e = jnp.arange(E, dtype=jnp.int32)[:, None]
    pad = jnp.where(j < pad_count[:, None], e * stride + A + j, E * stride)
    keys = lax.sort(jnp.concatenate([real, pad.reshape(-1).astype(jnp.int32)]))
    low = keys % stride
    is_real = jnp.logical_and(low < A, keys < E * stride)
    src = jnp.where(is_real, low, 0).reshape(n_blocks, 1, B)
    return src, dest, block_expert, n_used


def _unpack_pairs(u):
    lo = lax.bitcast_convert_type(u << 16, f32)
    hi = lax.bitcast_convert_type(u & jnp.uint32(0xFFFF0000), f32)
    return lo, hi


def _silu(x):
    return x * jax.nn.sigmoid(x)


X_CHUNK = 256


def _expert_kernel(be_ref, nu_ref, lay_ref, tab_hbm, hp_hbm, wu_ref, wd_ref, ys_ref,
                   tab_sm, xbuf, wu_sc, wd_sc, sem_t, gsem):
    del lay_ref
    i = pl.program_id(0)
    n_used = nu_ref[0]
    last_block = pl.num_programs(0) - 1
    B, half = xbuf.shape[1], xbuf.shape[2]
    n_chunks = half // X_CHUNK
    rows_per_chunk = B // n_chunks

    def table_copy(j):
        return pltpu.make_async_copy(tab_hbm.at[jnp.minimum(j, last_block)], tab_sm.at[j % 3], sem_t.at[j % 3])

    def gather_copy(j, r):
        return pltpu.make_async_copy(hp_hbm.at[pl.ds(tab_sm[j % 3, 0, r], 1)],
                                     xbuf.at[j % 2, pl.ds(r, 1)], gsem.at[j % 2])

    def wait_rows(j):
        pltpu.make_async_copy(hp_hbm.at[pl.ds(0, B)], xbuf.at[j % 2], gsem.at[j % 2]).wait()

    @pl.when(i == 0)
    def _():
        table_copy(0).start()
        table_copy(0).wait()
        table_copy(1).start()
        for r in range(B):
            gather_copy(0, r).start()

    @pl.when(i < n_used)
    def _():
        table_copy(i + 1).wait()

        @pl.when(i + 1 < n_used)
        def _():
            table_copy(i + 2).start()

        @pl.when(jnp.logical_or(i == 0, be_ref[i] != be_ref[jnp.maximum(i - 1, 0)]))
        def _():
            wu_sc[...] = wu_ref[...].astype(bf16)
            wd_sc[...] = wd_ref[...].astype(bf16)

        wait_rows(i)
        hu = None
        for c in range(n_chunks):
            cols = slice(c * X_CHUNK, (c + 1) * X_CHUNK)
            lo, hi = _unpack_pairs(xbuf[i % 2, :, cols])
            part = (jnp.dot(lo.astype(bf16), wu_sc[cols, :], preferred_element_type=f32)
                    + jnp.dot(hi.astype(bf16), wu_sc[half + c * X_CHUNK:half + (c + 1) * X_CHUNK, :],
                              preferred_element_type=f32))
            hu = part if hu is None else hu + part
            for r in range(c * rows_per_chunk, (c + 1) * rows_per_chunk):
                gather_copy(i + 1, r).start(priority=r % 2)
        hid = wd_sc.shape[0]
        act = (_silu(hu[:, :hid]) * hu[:, hid:]).astype(bf16)
        ys_ref[...] = _pack_pairs(jnp.dot(act, wd_sc[...], preferred_element_type=f32))

        @pl.when(i == n_used - 1)
        def _():
            wait_rows(i + 1)


def _experts(hp, src, block_expert, n_used, w_up, w_down, layer):
    N, D2 = hp.shape
    _, E, D, H2 = w_up.shape
    H = H2 // 2
    B = EXPERT_ROWS
    n_blocks = src.shape[0]
    live = lambda i, be, nu, lay: (jnp.minimum(i, nu[0] - 1), 0)
    return pl.pallas_call(
        _expert_kernel,
        grid_spec=pltpu.PrefetchScalarGridSpec(
            num_scalar_prefetch=3,
            grid=(n_blocks,),
            in_specs=[pl.BlockSpec(memory_space=pl.ANY),
                      pl.BlockSpec(memory_space=pl.ANY),
                      pl.BlockSpec((None, None, D, H2), lambda i, be, nu, lay: (lay[0], be[i], 0, 0)),
                      pl.BlockSpec((None, None, H, D), lambda i, be, nu, lay: (lay[0], be[i], 0, 0))],
            out_specs=pl.BlockSpec((B, D2), live),
            scratch_shapes=[pltpu.SMEM((3, 1, B), jnp.int32),
                            pltpu.VMEM((2, B, D2), jnp.uint32),
                            pltpu.VMEM((D, H2), bf16), pltpu.VMEM((H, D), bf16),
                            pltpu.SemaphoreType.DMA((3,)), pltpu.SemaphoreType.DMA((2,))],
        ),
        out_shape=jax.ShapeDtypeStruct((n_blocks * B, D2), jnp.uint32),
        compiler_params=_cparams("arbitrary"),
        name="moe_experts",
    )(block_expert, n_used, jnp.full((1,), layer, jnp.int32), src, hp, w_up, w_down)


COMBINE_TOKENS = 128


def _combine_kernel(dest_hbm, ys_hbm, gate_ref, h_ref, hb_ref, sg_ref, sv_ref, sd_ref, g_ref, b_ref,
                    ho_ref, hbo_ref, hpo_ref, tab_sm, ybuf, z_sc, gate_sc, sem_t, gsem, fence, *, alpha):
    i = pl.program_id(0)
    last_tile = pl.num_programs(0) - 1
    tm, d = h_ref.shape
    half = d // 2
    lanes = gate_sc.shape[2]
    n_chunks = half // lanes
    copies_per_chunk = TOP_K * tm // n_chunks

    def table_copy(j):
        return pltpu.make_async_copy(dest_hbm.at[jnp.minimum(j, last_tile)], tab_sm.at[j % 3], sem_t.at[j % 3])

    def row_copy(j, q):
        k, t = q // tm, q % tm
        return pltpu.make_async_copy(ys_hbm.at[pl.ds(tab_sm[j % 3, k, t], 1)],
                                     ybuf.at[j % 2, k, pl.ds(t, 1)], gsem.at[j % 2])

    def wait_rows(j):
        for k in range(TOP_K):
            pltpu.make_async_copy(ys_hbm.at[pl.ds(0, tm)], ybuf.at[j % 2, k], gsem.at[j % 2]).wait()

    @pl.when(i == 0)
    def _():
        table_copy(0).start()
        table_copy(0).wait()
        table_copy(1).start()
        for q in range(TOP_K * tm):
            row_copy(0, q).start()

    table_copy(i + 1).wait()

    @pl.when(i < last_tile)
    def _():
        table_copy(i + 2).start()

    hb = hb_ref[...]
    act = (_silu(jnp.dot(hb, sg_ref[...], preferred_element_type=f32))
           * jnp.dot(hb, sv_ref[...], preferred_element_type=f32)).astype(bf16)
    z_sc[...] = alpha * h_ref[...] + jnp.dot(act, sd_ref[...], preferred_element_type=f32)
    gate = gate_ref[...]
    for k in range(TOP_K):
        gate_sc[k] = jnp.broadcast_to(gate[:, k:k + 1], (tm, lanes))
    wait_rows(i)
    slot = i % 2
    for c in range(n_chunks):
        lo_sl = slice(c * lanes, (c + 1) * lanes)
        hi_sl = slice(half + c * lanes, half + (c + 1) * lanes)
        z_lo, z_hi = z_sc[:, lo_sl], z_sc[:, hi_sl]
        for k in range(TOP_K):
            lo, hi = _unpack_pairs(ybuf[slot, k, :, lo_sl])
            g = gate_sc[k]
            z_lo = z_lo + lo * g
            z_hi = z_hi + hi * g
        z_sc[:, lo_sl] = z_lo
        z_sc[:, hi_sl] = z_hi
        for q in range(c * copies_per_chunk, (c + 1) * copies_per_chunk):
            row_copy(i + 1, q).start(priority=q % 2)
        pl.semaphore_signal(fence, 1)
        pl.semaphore_wait(fence, 1)
    _store_hidden(_layer_norm_rows(z_sc[...], g_ref[...], b_ref[...]), ho_ref, hbo_ref, hpo_ref)

    @pl.when(i == last_tile)
    def _():
        wait_rows(i + 1)


def _combine(ys, dest, gate_t, h, hb, s_gate, s_val, s_down, ln_g, ln_b, alpha):
    N, D = h.shape
    H = s_gate.shape[1]
    nt, _, tm = dest.shape
    specs, shapes = _hidden_out(N, D, tm)
    row = pl.BlockSpec((tm, D), lambda i: (i, 0))
    full = lambda a, b: pl.BlockSpec((a, b), lambda i: (0, 0))
    hbm = pl.BlockSpec(memory_space=pl.ANY)
    return pl.pallas_call(
        functools.partial(_combine_kernel, alpha=alpha),
        grid=(nt,),
        in_specs=[hbm, hbm, pl.BlockSpec((tm, TOP_K), lambda i: (i, 0)),
                  row, row, full(D, H), full(D, H), full(H, D), full(1, D), full(1, D)],
        out_specs=specs,
        out_shape=shapes,
        scratch_shapes=[pltpu.SMEM((3, TOP_K, tm), jnp.int32),
                        pltpu.VMEM((2, TOP_K, tm, D // 2), jnp.uint32),
                        pltpu.VMEM((tm, D), f32), pltpu.VMEM((TOP_K, tm, 128), f32),
                        pltpu.SemaphoreType.DMA((3,)), pltpu.SemaphoreType.DMA((2,)),
                        pltpu.SemaphoreType.REGULAR],
        compiler_params=_cparams("arbitrary"),
        name="moe_combine",
    )(dest, ys, gate_t, h, hb, s_gate, s_val, s_down, ln_g.reshape(1, D), ln_b.reshape(1, D))


def _moe_layer(h, hb, hp, router_w, router_b, w_up, w_down, layer, s_gate, s_val, s_down,
               ln_g, ln_b, alpha):
    n_tok = h.shape[0]
    idx, gate, rank, counts = _router(h, router_w, router_b)
    src, dest, block_expert, n_used = _routing_tables(idx, rank, counts, n_tok, _pick(n_tok, COMBINE_TOKENS))
    ys = _experts(hp, src, block_expert, n_used, w_up, w_down, layer)
    return _combine(ys, dest, gate.T, h, hb, s_gate, s_val, s_down, ln_g, ln_b, alpha)


def kernel(x, lru_w_in, lru_b_in, lru_conv_w, lru_conv_b, lru_gx_w, lru_gx_b, lru_ga_w, lru_ga_b,
           lru_a_param, lru_w_out, attn_w_qkv, attn_sinks, attn_w_o, ln_g, ln_b,
           moe_router_w, moe_router_b, moe_w_up, moe_w_down, moe_shared_up, moe_shared_down):
    Bsz, T, D = x.shape
    depth = ln_g.shape[0]
    alpha = (2.0 * depth) ** 0.25
    W = lru_w_in.shape[2] // 2
    H = moe_w_down.shape[2]
    n_q = D // HEAD_DIM
    n_kv = (attn_w_qkv.shape[2] // HEAD_DIM - n_q) // 2
    N = Bsz * T
    assert Bsz == 1, "sequence mixers below assume one sequence"

    h = x.reshape(N, D)
    hb = h.astype(bf16)
    hp = None
    for layer in range(depth):
        j = layer // 2
        if layer % 2 == 0:
            w_in = lru_w_in[j].astype(bf16)
            xr, gb = _lru_in(hb, w_in[:, :W], w_in[:, W:],
                             lru_b_in[j, :W].reshape(1, W), lru_b_in[j, W:].reshape(1, W))
            y = _lru_core(xr, gb, lru_conv_w[j], lru_conv_b[j], lru_gx_w[j], lru_gx_b[j].reshape(-1),
                          lru_ga_w[j], lru_ga_b[j].reshape(-1), lru_a_param[j])
            z = _matmul_residual(y, lru_w_out[j].astype(bf16), h, alpha)
        else:
            qkv = _matmul(hb, attn_w_qkv[j].astype(bf16), bf16)
            o = _attention(qkv, attn_sinks[j], n_q, n_kv)
            z = _matmul_residual(o, attn_w_o[j].astype(bf16), h, alpha)
        h, hb, hp = _layer_norm(z, ln_g[layer, 0], ln_b[layer, 0])
        h, hb, hp = _moe_layer(
            h, hb, hp, moe_router_w[layer], moe_router_b[layer], moe_w_up, moe_w_down, layer,
            moe_shared_up[layer, :, :H].astype(bf16), moe_shared_up[layer, :, H:].astype(bf16),
            moe_shared_down[layer].astype(bf16), ln_g[layer, 1], ln_b[layer, 1], alpha)
    return h.reshape(Bsz, T, D)
```

```python
import functools

import jax
import jax.numpy as jnp
from jax import lax
from jax.experimental import pallas as pl
from jax.experimental.pallas import tpu as pltpu

HEAD_DIM = 128
WINDOW = 128
CONV_WIDTH = 4
LRU_C = 8.0
TOP_K = 8
N_GROUPS = 8
TOPK_GROUPS = 4
ROUTED_SCALE = 2.5
LN_EPS = 1e-5

V7X_VMEM_BYTES = 64 * 1024 * 1024
VMEM_LIMIT = V7X_VMEM_BYTES - 8 * 1024 * 1024
EXPERT_ROWS = 256

f32 = jnp.float32
bf16 = jnp.bfloat16


def _cparams(*sem):
    return pltpu.CompilerParams(dimension_semantics=sem, vmem_limit_bytes=VMEM_LIMIT)


def _pick(n, pref):
    t = min(pref, n)
    while n % t:
        t //= 2
    return t


def _mm_kernel(x_ref, w_ref, o_ref):
    w = w_ref[...].astype(bf16)
    o_ref[...] = jnp.dot(x_ref[...], w, preferred_element_type=f32).astype(o_ref.dtype)


def _matmul(x, w_stack, li, out_dtype):
    M, K = x.shape
    N = w_stack.shape[2]
    tm, tn = _pick(M, 1024), _pick(N, 512)
    return pl.pallas_call(
        _mm_kernel,
        grid=(M // tm, N // tn),
        in_specs=[pl.BlockSpec((tm, K), lambda i, j: (i, 0)),
                  pl.BlockSpec((None, K, tn), lambda i, j: (li, 0, j))],
        out_specs=pl.BlockSpec((tm, tn), lambda i, j: (i, j)),
        out_shape=jax.ShapeDtypeStruct((M, N), out_dtype),
        compiler_params=_cparams("parallel", "arbitrary"),
        name="matmul",
    )(x, w_stack)


def _mm_res_kernel(x_ref, w_ref, r_ref, o_ref, *, alpha):
    w = w_ref[...].astype(bf16)
    o_ref[...] = alpha * r_ref[...] + jnp.dot(x_ref[...], w, preferred_element_type=f32)


def _matmul_residual(x, w_stack, li, res, alpha):
    M, K = x.shape
    N = w_stack.shape[2]
    tm, tn = _pick(M, 1024), _pick(N, 512)
    return pl.pallas_call(
        functools.partial(_mm_res_kernel, alpha=alpha),
        grid=(M // tm, N // tn),
        in_specs=[pl.BlockSpec((tm, K), lambda i, j: (i, 0)),
                  pl.BlockSpec((None, K, tn), lambda i, j: (li, 0, j)),
                  pl.BlockSpec((tm, tn), lambda i, j: (i, j))],
        out_specs=pl.BlockSpec((tm, tn), lambda i, j: (i, j)),
        out_shape=jax.ShapeDtypeStruct((M, N), f32),
        compiler_params=_cparams("parallel", "arbitrary"),
        name="matmul_residual",
    )(x, w_stack, res)


def _gelu_tanh(x):
    return 0.5 * x * (1.0 + jnp.tanh(0.7978845608028654 * (x + 0.044715 * (x * x * x))))


def _lru_in_kernel(x_ref, wx_ref, wy_ref, bx_ref, by_ref, xr_ref, gb_ref):
    x = x_ref[...]
    xr_ref[...] = jnp.dot(x, wx_ref[...].astype(bf16), preferred_element_type=f32) + bx_ref[...]
    gb_ref[...] = _gelu_tanh(jnp.dot(x, wy_ref[...].astype(bf16), preferred_element_type=f32) + by_ref[...])


def _lru_in(x, w_stack, b_stack, li):
    M, K = x.shape
    W = w_stack.shape[2] // 2
    tm, tn = _pick(M, 1024), _pick(W, 256)
    gate0 = W // tn
    wx_spec = pl.BlockSpec((None, K, tn), lambda i, j: (li, 0, j))
    wy_spec = pl.BlockSpec((None, K, tn), lambda i, j: (li, 0, gate0 + j))
    bx_spec = pl.BlockSpec((None, 1, tn), lambda i, j: (li, 0, j))
    by_spec = pl.BlockSpec((None, 1, tn), lambda i, j: (li, 0, gate0 + j))
    ospec = pl.BlockSpec((tm, tn), lambda i, j: (i, j))
    return pl.pallas_call(
        _lru_in_kernel,
        grid=(M // tm, W // tn),
        in_specs=[pl.BlockSpec((tm, K), lambda i, j: (i, 0)), wx_spec, wy_spec, bx_spec, by_spec],
        out_specs=[ospec, ospec],
        out_shape=[jax.ShapeDtypeStruct((M, W), f32)] * 2,
        compiler_params=_cparams("parallel", "arbitrary"),
        name="lru_in",
    )(x, w_stack, w_stack, b_stack, b_stack)


def _layer_norm_rows(z, g, b):
    mu = jnp.mean(z, axis=-1, keepdims=True)
    zc = z - mu
    var = jnp.mean(zc * zc, axis=-1, keepdims=True)
    return zc * lax.rsqrt(var + LN_EPS) * g + b


def _pack_pairs(x):
    half = x.shape[1] // 2
    bits = lax.bitcast_convert_type(x.astype(bf16).astype(f32), jnp.uint32)
    return (bits[:, :half] >> 16) | bits[:, half:]


def _store_hidden(h, h_ref, hb_ref, hp_ref):
    h_ref[...] = h
    hb_ref[...] = h.astype(bf16)
    hp_ref[...] = _pack_pairs(h)


def _ln_kernel(z_ref, g_ref, b_ref, h_ref, hb_ref, hp_ref):
    _store_hidden(_layer_norm_rows(z_ref[...], g_ref[...], b_ref[...]), h_ref, hb_ref, hp_ref)


def _hidden_out(M, D, tm):
    specs = [pl.BlockSpec((tm, D), lambda i: (i, 0)),
             pl.BlockSpec((tm, D), lambda i: (i, 0)),
             pl.BlockSpec((tm, D // 2), lambda i: (i, 0))]
    shapes = [jax.ShapeDtypeStruct((M, D), f32),
              jax.ShapeDtypeStruct((M, D), bf16),
              jax.ShapeDtypeStruct((M, D // 2), jnp.uint32)]
    return specs, shapes


def _layer_norm(z, g, b):
    M, D = z.shape
    tm = _pick(M, 256)
    specs, shapes = _hidden_out(M, D, tm)
    vspec = pl.BlockSpec((1, D), lambda i: (0, 0))
    return pl.pallas_call(
        _ln_kernel,
        grid=(M // tm,),
        in_specs=[pl.BlockSpec((tm, D), lambda i: (i, 0)), vspec, vspec],
        out_specs=specs,
        out_shape=shapes,
        compiler_params=_cparams("parallel"),
        name="layer_norm",
    )(z, g.reshape(1, D), b.reshape(1, D))


def _sigmoid(x):
    return 0.5 + 0.5 * jnp.tanh(0.5 * x)


def _softplus(x):
    return jnp.maximum(x, 0.0) + jnp.log1p(jnp.exp(-jnp.abs(x)))


def _lru_core_kernel(xr_ref, gb_ref, cw_ref, cb_ref, gxw_ref, gxb_ref, gaw_ref, gab_ref, ap_ref,
                     y_ref, tail_sc, h_sc, a_sc, b_sc, hs_sc, *, heads, blk):
    tb = xr_ref.shape[0]

    @pl.when(pl.program_id(1) == 0)
    def _():
        tail_sc[...] = jnp.zeros_like(tail_sc)
        h_sc[...] = jnp.zeros_like(h_sc)

    xr = xr_ref[...]
    ext = jnp.concatenate([tail_sc[...], xr], axis=0)
    tail_sc[...] = xr[tb - 8:, :]
    cw = cw_ref[...]
    xc = cb_ref[...]
    for j in range(CONV_WIDTH):
        off = 8 - (CONV_WIDTH - 1) + j
        xc = xc + cw[j:j + 1, :] * ext[off:off + tb, :]
    xcb = xc.astype(bf16)

    for hd in range(heads):
        sl = slice(hd * blk, (hd + 1) * blk)
        xh = xcb[:, sl]
        gate_x = _sigmoid(jnp.dot(xh, gxw_ref[hd], preferred_element_type=f32) + gxb_ref[:, sl])
        gate_a = _sigmoid(jnp.dot(xh, gaw_ref[hd], preferred_element_type=f32) + gab_ref[:, sl])
        a = jnp.exp((-LRU_C * _softplus(-ap_ref[:, sl])) * gate_a)
        a_sc[:, sl] = a
        b_sc[:, sl] = xc[:, sl] * gate_x * jnp.sqrt(1.0 - a * a)

    def body(i, h):
        base = pl.multiple_of(i * 8, 8)
        a8 = a_sc[pl.ds(base, 8), :]
        b8 = b_sc[pl.ds(base, 8), :]
        rows = []
        for r in range(8):
            h = a8[r:r + 1, :] * h + b8[r:r + 1, :]
            rows.append(h)
        hs_sc[pl.ds(base, 8), :] = jnp.concatenate(rows, axis=0)
        return h

    h_sc[0:1, :] = lax.fori_loop(0, tb // 8, body, h_sc[0:1, :])
    y_ref[...] = (hs_sc[...] * gb_ref[...]).astype(y_ref.dtype)


def _lru_core(xr, gb, conv_w, conv_b, gx_w, gx_b, ga_w, ga_b, a_param):
    T, W = xr.shape
    nh, blk = gx_w.shape[0], gx_w.shape[1]
    wb = _pick(W, 1024)
    heads = wb // blk
    tb = _pick(T, 256)
    xspec = pl.BlockSpec((tb, wb), lambda c, t: (t, c))
    vspec = pl.BlockSpec((1, wb), lambda c, t: (0, c))
    gspec = pl.BlockSpec((heads, blk, blk), lambda c, t: (c, 0, 0))
    return pl.pallas_call(
        functools.partial(_lru_core_kernel, heads=heads, blk=blk),
        grid=(W // wb, T // tb),
        in_specs=[xspec, xspec, pl.BlockSpec((CONV_WIDTH, wb), lambda c, t: (0, c)), vspec,
                  gspec, vspec, gspec, vspec, vspec],
        out_specs=xspec,
        out_shape=jax.ShapeDtypeStruct((T, W), bf16),
        scratch_shapes=[pltpu.VMEM((8, wb), f32), pltpu.VMEM((8, wb), f32),
                        pltpu.VMEM((tb, wb), f32), pltpu.VMEM((tb, wb), f32), pltpu.VMEM((tb, wb), f32)],
        compiler_params=_cparams("parallel", "arbitrary"),
        name="lru_core",
    )(xr, gb, conv_w, conv_b.reshape(1, W), gx_w.astype(bf16), gx_b.reshape(1, W),
      ga_w.astype(bf16), ga_b.reshape(1, W), a_param.reshape(1, W))


def _attn_kernel(q_ref, kp_ref, kc_ref, vp_ref, vc_ref, bias_ref, sink_ref, o_ref, *, group, n_kv):
    blk = q_ref.shape[0]
    hd = HEAD_DIM
    col = lax.broadcasted_iota(jnp.int32, (group * blk, 2 * blk), 1)
    no_prev = jnp.logical_and(pl.program_id(0) == 0, col < blk)
    for kv in range(n_kv):
        q = jnp.concatenate([q_ref[:, (kv * group + g) * hd:(kv * group + g + 1) * hd] for g in range(group)],
                            axis=0)
        ksl = slice(kv * hd, (kv + 1) * hd)
        k = jnp.concatenate([kp_ref[:, ksl], kc_ref[:, ksl]], axis=0)
        v = jnp.concatenate([vp_ref[:, ksl], vc_ref[:, ksl]], axis=0)
        s = lax.dot_general(q, k, (((1,), (1,)), ((), ())), preferred_element_type=f32) * (hd ** -0.5)
        logits = jnp.where(no_prev, -jnp.inf, s + bias_ref[kv])
        sink = sink_ref[kv]
        m = jnp.maximum(jnp.max(logits, axis=-1, keepdims=True), sink)
        p = jnp.exp(logits - m)
        probs = p / (jnp.sum(p, axis=-1, keepdims=True) + jnp.exp(sink - m))
        o = jnp.dot(probs.astype(v.dtype), v, preferred_element_type=f32)
        for g in range(group):
            o_ref[:, (kv * group + g) * hd:(kv * group + g + 1) * hd] = (
                o[g * blk:(g + 1) * blk, :].astype(o_ref.dtype))


def _attention(qkv, sinks, n_q, n_kv):
    T = qkv.shape[0]
    group = n_q // n_kv
    blk = WINDOW
    nb = T // blk
    i = jnp.arange(blk)[:, None]
    j = jnp.arange(2 * blk)[None, :]
    dist = blk + i - j
    in_band = (dist >= 0) & (dist < WINDOW)
    slopes = jnp.exp2(-8.0 * jnp.arange(1, n_q + 1, dtype=f32) / n_q)
    bias = jnp.where(in_band[None], -slopes[:, None, None] * dist.astype(f32)[None], -jnp.inf)
    bias = bias.reshape(n_kv, group * blk, 2 * blk)
    sink = jnp.repeat(sinks.astype(f32).reshape(n_kv, group), blk, axis=1).reshape(n_kv, group * blk, 1)

    assert group * n_kv == n_q
    prev = lambda n: jnp.maximum(n - 1, 0)
    kvw = n_kv * HEAD_DIM
    return pl.pallas_call(
        functools.partial(_attn_kernel, group=group, n_kv=n_kv),
        grid=(nb,),
        in_specs=[pl.BlockSpec((blk, n_q * HEAD_DIM), lambda n: (n, 0)),
                  pl.BlockSpec((blk, kvw), lambda n: (prev(n), group)),
                  pl.BlockSpec((blk, kvw), lambda n: (n, group)),
                  pl.BlockSpec((blk, kvw), lambda n: (prev(n), group + 1)),
                  pl.BlockSpec((blk, kvw), lambda n: (n, group + 1)),
                  pl.BlockSpec((n_kv, group * blk, 2 * blk), lambda n: (0, 0, 0)),
                  pl.BlockSpec((n_kv, group * blk, 1), lambda n: (0, 0, 0))],
        out_specs=pl.BlockSpec((blk, n_q * HEAD_DIM), lambda n: (n, 0)),
        out_shape=jax.ShapeDtypeStruct((T, n_q * HEAD_DIM), bf16),
        compiler_params=_cparams("arbitrary"),
        name="swa_attention",
    )(qkv, qkv, qkv, qkv, qkv, bias, sink)


def _first_index_of_max(x, rows):
    m = jnp.max(x, axis=0, keepdims=True)
    idx = jnp.min(jnp.where(x == m, rows, x.shape[0]), axis=0, keepdims=True)
    return m, idx


def _router_kernel(h_ref, rwt_ref, rb_ref, idx_ref, gate_ref, rank_ref, cnt_ref, carry_sc, *, n_exp):
    tm = h_ref.shape[0]
    gsz = n_exp // N_GROUPS

    @pl.when(pl.program_id(0) == 0)
    def _():
        carry_sc[...] = jnp.zeros_like(carry_sc)

    h = h_ref[...]
    h_hi = h.astype(bf16)
    h_lo = (h - h_hi.astype(f32)).astype(bf16)
    nt = (((1,), (1,)), ((), ()))
    p = lax.dot_general(rwt_ref[...], h_hi, nt, preferred_element_type=f32)
    q = lax.dot_general(rwt_ref[:n_exp, :], h_lo, nt, preferred_element_type=f32)
    logits = p[:n_exp, :] + p[n_exp:, :] + q
    scores = jax.nn.sigmoid(logits)
    sel = scores + rb_ref[...]

    grow = lax.broadcasted_iota(jnp.int32, (gsz, tm), 0)
    gscores = []
    for g in range(N_GROUPS):
        sg = sel[g * gsz:(g + 1) * gsz, :]
        m1, i1 = _first_index_of_max(sg, grow)
        m2 = jnp.max(jnp.where(grow == i1, -jnp.inf, sg), axis=0, keepdims=True)
        gscores.append(m1 + m2)
    gs = jnp.concatenate(gscores, axis=0)
    g_iota = lax.broadcasted_iota(jnp.int32, (N_GROUPS, tm), 0)
    gpick = jnp.zeros((N_GROUPS, tm), jnp.bool_)
    for _ in range(TOPK_GROUPS):
        _, gi = _first_index_of_max(jnp.where(gpick, -jnp.inf, gs), g_iota)
        gpick = jnp.logical_or(gpick, g_iota == gi)

    erow = lax.broadcasted_iota(jnp.int32, (n_exp, tm), 0)
    emask = jnp.concatenate(
        [jnp.broadcast_to(gpick[g:g + 1, :], (gsz, tm)) for g in range(N_GROUPS)], axis=0)
    cand = jnp.where(emask, sel, -jnp.inf)
    idxs, gates = [], []
    taken = jnp.zeros((n_exp, tm), jnp.bool_)
    for _ in range(TOP_K):
        _, ei = _first_index_of_max(jnp.where(taken, -jnp.inf, cand), erow)
        hit = erow == ei
        taken = jnp.logical_or(taken, hit)
        idxs.append(ei)
        gates.append(jnp.sum(jnp.where(hit, scores, 0.0), axis=0, keepdims=True))
    gate = jnp.concatenate(gates, axis=0)
    idx_ref[...] = jnp.concatenate(idxs, axis=0)
    gate_ref[...] = gate / jnp.sum(gate, axis=0, keepdims=True) * ROUTED_SCALE

    onehot = jnp.where(taken, 1.0, 0.0).astype(bf16)
    r_i = lax.broadcasted_iota(jnp.int32, (tm, tm), 0)
    c_i = lax.broadcasted_iota(jnp.int32, (tm, tm), 1)
    before = jnp.where(r_i < c_i, 1.0, 0.0).astype(bf16)
    prefix = jnp.dot(onehot, before, preferred_element_type=f32) + carry_sc[:, 0:1]
    ranks = [jnp.sum(jnp.where(erow == idxs[k], prefix, 0.0), axis=0, keepdims=True) for k in range(TOP_K)]
    rank_ref[...] = jnp.concatenate(ranks, axis=0).astype(jnp.int32)

    total = carry_sc[...] + jnp.sum(jnp.where(taken, 1.0, 0.0), axis=1, keepdims=True)
    carry_sc[...] = total
    cnt_ref[...] = total


def _router(h, router_w, router_b):
    N, D = h.shape
    E = router_w.shape[1]
    tm = _pick(N, 512)
    wt = router_w.T
    w_hi = wt.astype(bf16)
    w_split = jnp.concatenate([w_hi, (wt - w_hi.astype(f32)).astype(bf16)], axis=0)
    kspec = pl.BlockSpec((TOP_K, tm), lambda i: (0, i))
    idx, gate, rank, cnt = pl.pallas_call(
        functools.partial(_router_kernel, n_exp=E),
        grid=(N // tm,),
        in_specs=[pl.BlockSpec((tm, D), lambda i: (i, 0)),
                  pl.BlockSpec((2 * E, D), lambda i: (0, 0)),
                  pl.BlockSpec((E, 1), lambda i: (0, 0))],
        out_specs=[kspec, kspec, kspec, pl.BlockSpec((E, 128), lambda i: (0, 0))],
        out_shape=[jax.ShapeDtypeStruct((TOP_K, N), jnp.int32),
                   jax.ShapeDtypeStruct((TOP_K, N), f32),
                   jax.ShapeDtypeStruct((TOP_K, N), jnp.int32),
                   jax.ShapeDtypeStruct((E, 128), f32)],
        scratch_shapes=[pltpu.VMEM((E, 128), f32)],
        compiler_params=_cparams("arbitrary"),
        name="moe_router",
    )(h, w_split, router_b.reshape(E, 1))
    return idx, gate, rank, cnt[:, 0].astype(jnp.int32)


def _routing_tables(idx, rank, counts, n_tok, tile):
    B = EXPERT_ROWS
    E = counts.shape[0]
    A = TOP_K * n_tok
    assert A % B == 0
    n_blocks = A // B + E
    stride = A + B
    blocks = (counts + B - 1) // B
    bend = jnp.cumsum(blocks)
    n_used = bend[-1:].astype(jnp.int32)
    block_ids = jnp.arange(n_blocks, dtype=jnp.int32)
    block_expert = jnp.minimum(jnp.sum(bend[None, :] <= block_ids[:, None], axis=1), E - 1).astype(jnp.int32)

    first_row = ((bend - blocks) * B).astype(jnp.int32)
    onehot = idx[:, :, None] == jnp.arange(E, dtype=jnp.int32)
    dest = rank + jnp.sum(jnp.where(onehot, first_row, 0), axis=-1)
    dest = dest.reshape(TOP_K, n_tok // tile, tile).transpose(1, 0, 2)

    token = jnp.arange(n_tok, dtype=jnp.int32)[None, :]
    real = (idx * stride + token).reshape(-1)
    pad_count = blocks * B - counts
    j = jnp.arange(B, dtype=jnp.int32)[None, :]
    e = jnp.arange(E, dtype=jnp.int32)[:, None]
    pad = jnp.where(j < pad_count[:, None], e * stride + A + j, E * stride)
    keys = lax.sort(jnp.concatenate([real, pad.reshape(-1).astype(jnp.int32)]))
    low = keys % stride
    is_real = jnp.logical_and(low < A, keys < E * stride)
    src = jnp.where(is_real, low, 0).reshape(n_blocks, 1, B)
    return src, dest, block_expert, n_used


def _unpack_pairs(u):
    lo = lax.bitcast_convert_type(u << 16, f32)
    hi = lax.bitcast_convert_type(u & jnp.uint32(0xFFFF0000), f32)
    return lo, hi


def _silu(x):
    return x * jax.nn.sigmoid(x)


X_CHUNK = 256


def _expert_kernel(be_ref, nu_ref, lay_ref, tab_hbm, hp_hbm, wu_ref, wd_ref, ys_ref,
                   tab_sm, xbuf, wu_sc, wd_sc, sem_t, gsem):
    del lay_ref
    i = pl.program_id(0)
    n_used = nu_ref[0]
    last_block = pl.num_programs(0) - 1
    B, half = xbuf.shape[1], xbuf.shape[2]
    n_chunks = half // X_CHUNK
    rows_per_chunk = B // n_chunks

    def table_copy(j):
        return pltpu.make_async_copy(tab_hbm.at[jnp.minimum(j, last_block)], tab_sm.at[j % 3], sem_t.at[j % 3])

    def gather_copy(j, r):
        return pltpu.make_async_copy(hp_hbm.at[pl.ds(tab_sm[j % 3, 0, r], 1)],
                                     xbuf.at[j % 2, pl.ds(r, 1)], gsem.at[j % 2])

    def wait_rows(j):
        pltpu.make_async_copy(hp_hbm.at[pl.ds(0, B)], xbuf.at[j % 2], gsem.at[j % 2]).wait()

    @pl.when(i == 0)
    def _():
        table_copy(0).start()
        table_copy(0).wait()
        table_copy(1).start()
        for r in range(B):
            gather_copy(0, r).start()

    @pl.when(i < n_used)
    def _():
        table_copy(i + 1).wait()

        @pl.when(i + 1 < n_used)
        def _():
            table_copy(i + 2).start()

        @pl.when(jnp.logical_or(i == 0, be_ref[i] != be_ref[jnp.maximum(i - 1, 0)]))
        def _():
            wu_sc[...] = wu_ref[...].astype(bf16)
            wd_sc[...] = wd_ref[...].astype(bf16)

        wait_rows(i)
        hu = None
        for c in range(n_chunks):
            cols = slice(c * X_CHUNK, (c + 1) * X_CHUNK)
            lo, hi = _unpack_pairs(xbuf[i % 2, :, cols])
            part = (jnp.dot(lo.astype(bf16), wu_sc[cols, :], preferred_element_type=f32)
                    + jnp.dot(hi.astype(bf16), wu_sc[half + c * X_CHUNK:half + (c + 1) * X_CHUNK, :],
                              preferred_element_type=f32))
            hu = part if hu is None else hu + part
            for r in range(c * rows_per_chunk, (c + 1) * rows_per_chunk):
                gather_copy(i + 1, r).start()
        hid = wd_sc.shape[0]
        act = (_silu(hu[:, :hid]) * hu[:, hid:]).astype(bf16)
        ys_ref[...] = _pack_pairs(jnp.dot(act, wd_sc[...], preferred_element_type=f32))

        @pl.when(i == n_used - 1)
        def _():
            wait_rows(i + 1)


def _experts(hp, src, block_expert, n_used, w_up, w_down, layer):
    N, D2 = hp.shape
    _, E, D, H2 = w_up.shape
    H = H2 // 2
    B = EXPERT_ROWS
    n_blocks = src.shape[0]
    live = lambda i, be, nu, lay: (jnp.minimum(i, nu[0] - 1), 0)
    return pl.pallas_call(
        _expert_kernel,
        grid_spec=pltpu.PrefetchScalarGridSpec(
            num_scalar_prefetch=3,
            grid=(n_blocks,),
            in_specs=[pl.BlockSpec(memory_space=pl.ANY),
                      pl.BlockSpec(memory_space=pl.ANY),
                      pl.BlockSpec((None, None, D, H2), lambda i, be, nu, lay: (lay[0], be[i], 0, 0)),
                      pl.BlockSpec((None, None, H, D), lambda i, be, nu, lay: (lay[0], be[i], 0, 0))],
            out_specs=pl.BlockSpec((B, D2), live),
            scratch_shapes=[pltpu.SMEM((3, 1, B), jnp.int32),
                            pltpu.VMEM((2, B, D2), jnp.uint32),
                            pltpu.VMEM((D, H2), bf16), pltpu.VMEM((H, D), bf16),
                            pltpu.SemaphoreType.DMA((3,)), pltpu.SemaphoreType.DMA((2,))],
        ),
        out_shape=jax.ShapeDtypeStruct((n_blocks * B, D2), jnp.uint32),
        compiler_params=_cparams("arbitrary"),
        name="moe_experts",
    )(block_expert, n_used, jnp.full((1,), layer, jnp.int32), src, hp, w_up, w_down)


COMBINE_TOKENS = 128


def _combine_kernel(dest_hbm, ys_hbm, gate_ref, h_ref, hb_ref, sg_ref, sv_ref, sd_ref, g_ref, b_ref,
                    ho_ref, hbo_ref, hpo_ref, tab_sm, ybuf, z_sc, gate_sc, sem_t, gsem, fence, *, alpha):
    i = pl.program_id(0)
    last_tile = pl.num_programs(0) - 1
    tm, d = h_ref.shape
    half = d // 2
    lanes = gate_sc.shape[2]
    n_chunks = half // lanes
    copies_per_chunk = TOP_K * tm // n_chunks

    def table_copy(j):
        return pltpu.make_async_copy(dest_hbm.at[jnp.minimum(j, last_tile)], tab_sm.at[j % 3], sem_t.at[j % 3])

    def row_copy(j, q, slot=None):
        k, t = q // tm, q % tm
        slot = j % 2 if slot is None else slot
        return pltpu.make_async_copy(ys_hbm.at[pl.ds(tab_sm[j % 3, k, t], 1)],
                                     ybuf.at[slot, k, pl.ds(t, 1)], gsem.at[slot])

    def wait_rows(j):
        for k in range(TOP_K):
            pltpu.make_async_copy(ys_hbm.at[pl.ds(0, tm)], ybuf.at[j % 2, k], gsem.at[j % 2]).wait()

    @pl.when(i == 0)
    def _():
        table_copy(0).start()
        table_copy(0).wait()
        table_copy(1).start()
        for q in range(TOP_K * tm):
            row_copy(0, q).start()

    table_copy(i + 1).wait()

    @pl.when(i < last_tile)
    def _():
        table_copy(i + 2).start()

    hb = hb_ref[...]
    act = (_silu(jnp.dot(hb, sg_ref[...], preferred_element_type=f32))
           * jnp.dot(hb, sv_ref[...], preferred_element_type=f32)).astype(bf16)
    z_sc[...] = alpha * h_ref[...] + jnp.dot(act, sd_ref[...], preferred_element_type=f32)
    gate = gate_ref[...]
    for k in range(TOP_K):
        gate_sc[k] = jnp.broadcast_to(gate[:, k:k + 1], (tm, lanes))
    wait_rows(i)

    def accumulate(slot):
        for c in range(n_chunks):
            lo_sl = slice(c * lanes, (c + 1) * lanes)
            hi_sl = slice(half + c * lanes, half + (c + 1) * lanes)
            z_lo, z_hi = z_sc[:, lo_sl], z_sc[:, hi_sl]
            for k in range(TOP_K):
                lo, hi = _unpack_pairs(ybuf[slot, k, :, lo_sl])
                g = gate_sc[k]
                z_lo = z_lo + lo * g
                z_hi = z_hi + hi * g
            z_sc[:, lo_sl] = z_lo
            z_sc[:, hi_sl] = z_hi
            for q in range(c * copies_per_chunk, (c + 1) * copies_per_chunk):
                row_copy(i + 1, q, 1 - slot).start()
            pl.semaphore_signal(fence, 1)
            pl.semaphore_wait(fence, 1)

    for parity in (0, 1):
        pl.when(i % 2 == parity)(functools.partial(accumulate, parity))

    _store_hidden(_layer_norm_rows(z_sc[...], g_ref[...], b_ref[...]), ho_ref, hbo_ref, hpo_ref)

    @pl.when(i == last_tile)
    def _():
        wait_rows(i + 1)


def _combine(ys, dest, gate_t, h, hb, s_gate, s_val, s_down, ln_g, ln_b, alpha):
    N, D = h.shape
    H = s_gate.shape[1]
    nt, _, tm = dest.shape
    specs, shapes = _hidden_out(N, D, tm)
    row = pl.BlockSpec((tm, D), lambda i: (i, 0))
    full = lambda a, b: pl.BlockSpec((a, b), lambda i: (0, 0))
    hbm = pl.BlockSpec(memory_space=pl.ANY)
    return pl.pallas_call(
        functools.partial(_combine_kernel, alpha=alpha),
        grid=(nt,),
        in_specs=[hbm, hbm, pl.BlockSpec((tm, TOP_K), lambda i: (i, 0)),
                  row, row, full(D, H), full(D, H), full(H, D), full(1, D), full(1, D)],
        out_specs=specs,
        out_shape=shapes,
        scratch_shapes=[pltpu.SMEM((3, TOP_K, tm), jnp.int32),
                        pltpu.VMEM((2, TOP_K, tm, D // 2), jnp.uint32),
                        pltpu.VMEM((tm, D), f32), pltpu.VMEM((TOP_K, tm, 128), f32),
                        pltpu.SemaphoreType.DMA((3,)), pltpu.SemaphoreType.DMA((2,)),
                        pltpu.SemaphoreType.REGULAR],
        compiler_params=_cparams("arbitrary"),
        name="moe_combine",
    )(dest, ys, gate_t, h, hb, s_gate, s_val, s_down, ln_g.reshape(1, D), ln_b.reshape(1, D))


def _moe_layer(h, hb, hp, router_w, router_b, w_up, w_down, layer, s_gate, s_val, s_down,
               ln_g, ln_b, alpha):
    n_tok = h.shape[0]
    idx, gate, rank, counts = _router(h, router_w, router_b)
    src, dest, block_expert, n_used = _routing_tables(idx, rank, counts, n_tok, _pick(n_tok, COMBINE_TOKENS))
    ys = _experts(hp, src, block_expert, n_used, w_up, w_down, layer)
    return _combine(ys, dest, gate.T, h, hb, s_gate, s_val, s_down, ln_g, ln_b, alpha)


def kernel(x, lru_w_in, lru_b_in, lru_conv_w, lru_conv_b, lru_gx_w, lru_gx_b, lru_ga_w, lru_ga_b,
           lru_a_param, lru_w_out, attn_w_qkv, attn_sinks, attn_w_o, ln_g, ln_b,
           moe_router_w, moe_router_b, moe_w_up, moe_w_down, moe_shared_up, moe_shared_down):
    Bsz, T, D = x.shape
    depth = ln_g.shape[0]
    alpha = (2.0 * depth) ** 0.25
    W = lru_w_in.shape[2] // 2
    H = moe_w_down.shape[2]
    n_q = D // HEAD_DIM
    n_kv = (attn_w_qkv.shape[2] // HEAD_DIM - n_q) // 2
    N = Bsz * T
    assert Bsz == 1, "sequence mixers below assume one sequence"

    h = x.reshape(N, D)
    hb = h.astype(bf16)
    hp = None
    for layer in range(depth):
        j = layer // 2
        if layer % 2 == 0:
            xr, gb = _lru_in(hb, lru_w_in, lru_b_in.reshape(-1, 1, 2 * W), j)
            y = _lru_core(xr, gb, lru_conv_w[j], lru_conv_b[j], lru_gx_w[j], lru_gx_b[j].reshape(-1),
                          lru_ga_w[j], lru_ga_b[j].reshape(-1), lru_a_param[j])
            z = _matmul_residual(y, lru_w_out, j, h, alpha)
        else:
            qkv = _matmul(hb, attn_w_qkv, j, bf16)
            o = _attention(qkv, attn_sinks[j], n_q, n_kv)
            z = _matmul_residual(o, attn_w_o, j, h, alpha)
        h, hb, hp = _layer_norm(z, ln_g[layer, 0], ln_b[layer, 0])
        h, hb, hp = _moe_layer(
            h, hb, hp, moe_router_w[layer], moe_router_b[layer], moe_w_up, moe_w_down, layer,
            moe_shared_up[layer, :, :H].astype(bf16), moe_shared_up[layer, :, H:].astype(bf16),
            moe_shared_down[layer].astype(bf16), ln_g[layer, 1], ln_b[layer, 1], alpha)
    return h.reshape(Bsz, T, D)
```

```python
import functools

import jax
import jax.numpy as jnp
from jax import lax
from jax.experimental import pallas as pl
from jax.experimental.pallas import tpu as pltpu

HEAD_DIM = 128
WINDOW = 128
CONV_WIDTH = 4
LRU_C = 8.0
TOP_K = 8
N_GROUPS = 8
TOPK_GROUPS = 4
ROUTED_SCALE = 2.5
LN_EPS = 1e-5

V7X_VMEM_BYTES = 64 * 1024 * 1024
VMEM_LIMIT = V7X_VMEM_BYTES - 8 * 1024 * 1024
EXPERT_ROWS = 256

f32 = jnp.float32
bf16 = jnp.bfloat16


def _cparams(*sem):
    return pltpu.CompilerParams(dimension_semantics=sem, vmem_limit_bytes=VMEM_LIMIT)


def _pick(n, pref):
    t = min(pref, n)
    while n % t:
        t //= 2
    return t


def _mm_kernel(x_ref, w_ref, o_ref):
    w = w_ref[...].astype(bf16)
    o_ref[...] = jnp.dot(x_ref[...], w, preferred_element_type=f32).astype(o_ref.dtype)


def _matmul(x, w_stack, li, out_dtype):
    M, K = x.shape
    N = w_stack.shape[2]
    tm, tn = _pick(M, 1024), _pick(N, 512)
    return pl.pallas_call(
        _mm_kernel,
        grid=(M // tm, N // tn),
        in_specs=[pl.BlockSpec((tm, K), lambda i, j: (i, 0)),
                  pl.BlockSpec((None, K, tn), lambda i, j: (li, 0, j))],
        out_specs=pl.BlockSpec((tm, tn), lambda i, j: (i, j)),
        out_shape=jax.ShapeDtypeStruct((M, N), out_dtype),
        compiler_params=_cparams("parallel", "arbitrary"),
        name="matmul",
    )(x, w_stack)


def _mm_res_kernel(x_ref, w_ref, r_ref, o_ref, *, alpha):
    w = w_ref[...].astype(bf16)
    o_ref[...] = alpha * r_ref[...] + jnp.dot(x_ref[...], w, preferred_element_type=f32)


def _matmul_residual(x, w_stack, li, res, alpha):
    M, K = x.shape
    N = w_stack.shape[2]
    tm, tn = _pick(M, 1024), _pick(N, 512)
    return pl.pallas_call(
        functools.partial(_mm_res_kernel, alpha=alpha),
        grid=(M // tm, N // tn),
        in_specs=[pl.BlockSpec((tm, K), lambda i, j: (i, 0)),
                  pl.BlockSpec((None, K, tn), lambda i, j: (li, 0, j)),
                  pl.BlockSpec((tm, tn), lambda i, j: (i, j))],
        out_specs=pl.BlockSpec((tm, tn), lambda i, j: (i, j)),
        out_shape=jax.ShapeDtypeStruct((M, N), f32),
        compiler_params=_cparams("parallel", "arbitrary"),
        name="matmul_residual",
    )(x, w_stack, res)


def _gelu_tanh(x):
    return 0.5 * x * (1.0 + jnp.tanh(0.7978845608028654 * (x + 0.044715 * (x * x * x))))


def _lru_in_kernel(x_ref, wx_ref, wy_ref, bx_ref, by_ref, xr_ref, gb_ref):
    x = x_ref[...]
    xr_ref[...] = jnp.dot(x, wx_ref[...].astype(bf16), preferred_element_type=f32) + bx_ref[...]
    gb_ref[...] = _gelu_tanh(jnp.dot(x, wy_ref[...].astype(bf16), preferred_element_type=f32) + by_ref[...])


def _lru_in(x, w_stack, b_stack, li):
    M, K = x.shape
    W = w_stack.shape[2] // 2
    tm, tn = _pick(M, 1024), _pick(W, 256)
    gate0 = W // tn
    wx_spec = pl.BlockSpec((None, K, tn), lambda i, j: (li, 0, j))
    wy_spec = pl.BlockSpec((None, K, tn), lambda i, j: (li, 0, gate0 + j))
    bx_spec = pl.BlockSpec((None, 1, tn), lambda i, j: (li, 0, j))
    by_spec = pl.BlockSpec((None, 1, tn), lambda i, j: (li, 0, gate0 + j))
    ospec = pl.BlockSpec((tm, tn), lambda i, j: (i, j))
    return pl.pallas_call(
        _lru_in_kernel,
        grid=(M // tm, W // tn),
        in_specs=[pl.BlockSpec((tm, K), lambda i, j: (i, 0)), wx_spec, wy_spec, bx_spec, by_spec],
        out_specs=[ospec, ospec],
        out_shape=[jax.ShapeDtypeStruct((M, W), f32)] * 2,
        compiler_params=_cparams("parallel", "arbitrary"),
        name="lru_in",
    )(x, w_stack, w_stack, b_stack, b_stack)


def _layer_norm_rows(z, g, b):
    mu = jnp.mean(z, axis=-1, keepdims=True)
    zc = z - mu
    var = jnp.mean(zc * zc, axis=-1, keepdims=True)
    return zc * lax.rsqrt(var + LN_EPS) * g + b


def _pack_pairs(x):
    half = x.shape[1] // 2
    bits = lax.bitcast_convert_type(x.astype(bf16).astype(f32), jnp.uint32)
    return (bits[:, :half] >> 16) | bits[:, half:]


def _store_hidden(h, h_ref, hb_ref, hp_ref):
    h_ref[...] = h
    hb_ref[...] = h.astype(bf16)
    hp_ref[...] = _pack_pairs(h)


def _ln_kernel(z_ref, g_ref, b_ref, h_ref, hb_ref, hp_ref):
    _store_hidden(_layer_norm_rows(z_ref[...], g_ref[...], b_ref[...]), h_ref, hb_ref, hp_ref)


def _hidden_out(M, D, tm):
    specs = [pl.BlockSpec((tm, D), lambda i: (i, 0)),
             pl.BlockSpec((tm, D), lambda i: (i, 0)),
             pl.BlockSpec((tm, D // 2), lambda i: (i, 0))]
    shapes = [jax.ShapeDtypeStruct((M, D), f32),
              jax.ShapeDtypeStruct((M, D), bf16),
              jax.ShapeDtypeStruct((M, D // 2), jnp.uint32)]
    return specs, shapes


def _layer_norm(z, g, b):
    M, D = z.shape
    tm = _pick(M, 256)
    specs, shapes = _hidden_out(M, D, tm)
    vspec = pl.BlockSpec((1, D), lambda i: (0, 0))
    return pl.pallas_call(
        _ln_kernel,
        grid=(M // tm,),
        in_specs=[pl.BlockSpec((tm, D), lambda i: (i, 0)), vspec, vspec],
        out_specs=specs,
        out_shape=shapes,
        compiler_params=_cparams("parallel"),
        name="layer_norm",
    )(z, g.reshape(1, D), b.reshape(1, D))


def _sigmoid(x):
    return 0.5 + 0.5 * jnp.tanh(0.5 * x)


def _softplus(x):
    return jnp.maximum(x, 0.0) + jnp.log1p(jnp.exp(-jnp.abs(x)))


def _lru_core_kernel(xr_ref, gb_ref, cw_ref, cb_ref, gxw_ref, gxb_ref, gaw_ref, gab_ref, ap_ref,
                     y_ref, tail_sc, h_sc, a_sc, b_sc, hs_sc, *, heads, blk):
    tb = xr_ref.shape[0]

    @pl.when(pl.program_id(1) == 0)
    def _():
        tail_sc[...] = jnp.zeros_like(tail_sc)
        h_sc[...] = jnp.zeros_like(h_sc)

    xr = xr_ref[...]
    ext = jnp.concatenate([tail_sc[...], xr], axis=0)
    tail_sc[...] = xr[tb - 8:, :]
    cw = cw_ref[...]
    xc = cb_ref[...]
    for j in range(CONV_WIDTH):
        off = 8 - (CONV_WIDTH - 1) + j
        xc = xc + cw[j:j + 1, :] * ext[off:off + tb, :]
    xcb = xc.astype(bf16)

    for hd in range(heads):
        sl = slice(hd * blk, (hd + 1) * blk)
        xh = xcb[:, sl]
        gate_x = _sigmoid(jnp.dot(xh, gxw_ref[hd], preferred_element_type=f32) + gxb_ref[:, sl])
        gate_a = _sigmoid(jnp.dot(xh, gaw_ref[hd], preferred_element_type=f32) + gab_ref[:, sl])
        a = jnp.exp((-LRU_C * _softplus(-ap_ref[:, sl])) * gate_a)
        a_sc[:, sl] = a
        b_sc[:, sl] = xc[:, sl] * gate_x * jnp.sqrt(1.0 - a * a)

    def body(i, h):
        base = pl.multiple_of(i * 8, 8)
        a8 = a_sc[pl.ds(base, 8), :]
        b8 = b_sc[pl.ds(base, 8), :]
        rows = []
        for r in range(8):
            h = a8[r:r + 1, :] * h + b8[r:r + 1, :]
            rows.append(h)
        hs_sc[pl.ds(base, 8), :] = jnp.concatenate(rows, axis=0)
        return h

    h_sc[0:1, :] = lax.fori_loop(0, tb // 8, body, h_sc[0:1, :])
    y_ref[...] = (hs_sc[...] * gb_ref[...]).astype(y_ref.dtype)


def _lru_core(xr, gb, conv_w, conv_b, gx_w, gx_b, ga_w, ga_b, a_param):
    T, W = xr.shape
    nh, blk = gx_w.shape[0], gx_w.shape[1]
    wb = _pick(W, 1024)
    heads = wb // blk
    tb = _pick(T, 256)
    xspec = pl.BlockSpec((tb, wb), lambda c, t: (t, c))
    vspec = pl.BlockSpec((1, wb), lambda c, t: (0, c))
    gspec = pl.BlockSpec((heads, blk, blk), lambda c, t: (c, 0, 0))
    return pl.pallas_call(
        functools.partial(_lru_core_kernel, heads=heads, blk=blk),
        grid=(W // wb, T // tb),
        in_specs=[xspec, xspec, pl.BlockSpec((CONV_WIDTH, wb), lambda c, t: (0, c)), vspec,
                  gspec, vspec, gspec, vspec, vspec],
        out_specs=xspec,
        out_shape=jax.ShapeDtypeStruct((T, W), bf16),
        scratch_shapes=[pltpu.VMEM((8, wb), f32), pltpu.VMEM((8, wb), f32),
                        pltpu.VMEM((tb, wb), f32), pltpu.VMEM((tb, wb), f32), pltpu.VMEM((tb, wb), f32)],
        compiler_params=_cparams("parallel", "arbitrary"),
        name="lru_core",
    )(xr, gb, conv_w, conv_b.reshape(1, W), gx_w.astype(bf16), gx_b.reshape(1, W),
      ga_w.astype(bf16), ga_b.reshape(1, W), a_param.reshape(1, W))


def _attn_kernel(q_ref, kp_ref, kc_ref, vp_ref, vc_ref, bias_ref, sink_ref, o_ref, *, group, n_kv):
    blk = q_ref.shape[0]
    hd = HEAD_DIM
    col = lax.broadcasted_iota(jnp.int32, (group * blk, 2 * blk), 1)
    no_prev = jnp.logical_and(pl.program_id(0) == 0, col < blk)
    for kv in range(n_kv):
        q = jnp.concatenate([q_ref[:, (kv * group + g) * hd:(kv * group + g + 1) * hd] for g in range(group)],
                            axis=0)
        ksl = slice(kv * hd, (kv + 1) * hd)
        k = jnp.concatenate([kp_ref[:, ksl], kc_ref[:, ksl]], axis=0)
        v = jnp.concatenate([vp_ref[:, ksl], vc_ref[:, ksl]], axis=0)
        s = lax.dot_general(q, k, (((1,), (1,)), ((), ())), preferred_element_type=f32) * (hd ** -0.5)
        logits = jnp.where(no_prev, -jnp.inf, s + bias_ref[kv])
        sink = sink_ref[kv]
        m = jnp.maximum(jnp.max(logits, axis=-1, keepdims=True), sink)
        p = jnp.exp(logits - m)
        probs = p / (jnp.sum(p, axis=-1, keepdims=True) + jnp.exp(sink - m))
        o = jnp.dot(probs.astype(v.dtype), v, preferred_element_type=f32)
        for g in range(group):
            o_ref[:, (kv * group + g) * hd:(kv * group + g + 1) * hd] = (
                o[g * blk:(g + 1) * blk, :].astype(o_ref.dtype))


def _attention(qkv, sinks, n_q, n_kv):
    T = qkv.shape[0]
    group = n_q // n_kv
    blk = WINDOW
    nb = T // blk
    i = jnp.arange(blk)[:, None]
    j = jnp.arange(2 * blk)[None, :]
    dist = blk + i - j
    in_band = (dist >= 0) & (dist < WINDOW)
    slopes = jnp.exp2(-8.0 * jnp.arange(1, n_q + 1, dtype=f32) / n_q)
    bias = jnp.where(in_band[None], -slopes[:, None, None] * dist.astype(f32)[None], -jnp.inf)
    bias = bias.reshape(n_kv, group * blk, 2 * blk)
    sink = jnp.repeat(sinks.astype(f32).reshape(n_kv, group), blk, axis=1).reshape(n_kv, group * blk, 1)

    assert group * n_kv == n_q
    prev = lambda n: jnp.maximum(n - 1, 0)
    kvw = n_kv * HEAD_DIM
    return pl.pallas_call(
        functools.partial(_attn_kernel, group=group, n_kv=n_kv),
        grid=(nb,),
        in_specs=[pl.BlockSpec((blk, n_q * HEAD_DIM), lambda n: (n, 0)),
                  pl.BlockSpec((blk, kvw), lambda n: (prev(n), group)),
                  pl.BlockSpec((blk, kvw), lambda n: (n, group)),
                  pl.BlockSpec((blk, kvw), lambda n: (prev(n), group + 1)),
                  pl.BlockSpec((blk, kvw), lambda n: (n, group + 1)),
                  pl.BlockSpec((n_kv, group * blk, 2 * blk), lambda n: (0, 0, 0)),
                  pl.BlockSpec((n_kv, group * blk, 1), lambda n: (0, 0, 0))],
        out_specs=pl.BlockSpec((blk, n_q * HEAD_DIM), lambda n: (n, 0)),
        out_shape=jax.ShapeDtypeStruct((T, n_q * HEAD_DIM), bf16),
        compiler_params=_cparams("arbitrary"),
        name="swa_attention",
    )(qkv, qkv, qkv, qkv, qkv, bias, sink)


def _first_index_of_max(x, rows):
    m = jnp.max(x, axis=0, keepdims=True)
    idx = jnp.min(jnp.where(x == m, rows, x.shape[0]), axis=0, keepdims=True)
    return m, idx


def _router_kernel(h_ref, rwt_ref, rb_ref, idx_ref, gate_ref, rank_ref, cnt_ref, carry_sc, *, n_exp):
    tm = h_ref.shape[0]
    gsz = n_exp // N_GROUPS

    @pl.when(pl.program_id(0) == 0)
    def _():
        carry_sc[...] = jnp.zeros_like(carry_sc)

    h = h_ref[...]
    h_hi = h.astype(bf16)
    h_lo = (h - h_hi.astype(f32)).astype(bf16)
    nt = (((1,), (1,)), ((), ()))
    p = lax.dot_general(rwt_ref[...], h_hi, nt, preferred_element_type=f32)
    q = lax.dot_general(rwt_ref[:n_exp, :], h_lo, nt, preferred_element_type=f32)
    logits = p[:n_exp, :] + p[n_exp:, :] + q
    scores = jax.nn.sigmoid(logits)
    sel = scores + rb_ref[...]

    grow = lax.broadcasted_iota(jnp.int32, (gsz, tm), 0)
    gscores = []
    for g in range(N_GROUPS):
        sg = sel[g * gsz:(g + 1) * gsz, :]
        m1, i1 = _first_index_of_max(sg, grow)
        m2 = jnp.max(jnp.where(grow == i1, -jnp.inf, sg), axis=0, keepdims=True)
        gscores.append(m1 + m2)
    gs = jnp.concatenate(gscores, axis=0)
    g_iota = lax.broadcasted_iota(jnp.int32, (N_GROUPS, tm), 0)
    gpick = jnp.zeros((N_GROUPS, tm), jnp.bool_)
    for _ in range(TOPK_GROUPS):
        _, gi = _first_index_of_max(jnp.where(gpick, -jnp.inf, gs), g_iota)
        gpick = jnp.logical_or(gpick, g_iota == gi)

    erow = lax.broadcasted_iota(jnp.int32, (n_exp, tm), 0)
    emask = jnp.concatenate(
        [jnp.broadcast_to(gpick[g:g + 1, :], (gsz, tm)) for g in range(N_GROUPS)], axis=0)
    cand = jnp.where(emask, sel, -jnp.inf)
    idxs, gates = [], []
    taken = jnp.zeros((n_exp, tm), jnp.bool_)
    for _ in range(TOP_K):
        _, ei = _first_index_of_max(jnp.where(taken, -jnp.inf, cand), erow)
        hit = erow == ei
        taken = jnp.logical_or(taken, hit)
        idxs.append(ei)
        gates.append(jnp.sum(jnp.where(hit, scores, 0.0), axis=0, keepdims=True))
    gate = jnp.concatenate(gates, axis=0)
    idx_ref[...] = jnp.concatenate(idxs, axis=0)
    gate_ref[...] = gate / jnp.sum(gate, axis=0, keepdims=True) * ROUTED_SCALE

    onehot = jnp.where(taken, 1.0, 0.0).astype(bf16)
    r_i = lax.broadcasted_iota(jnp.int32, (tm, tm), 0)
    c_i = lax.broadcasted_iota(jnp.int32, (tm, tm), 1)
    before = jnp.where(r_i < c_i, 1.0, 0.0).astype(bf16)
    prefix = jnp.dot(onehot, before, preferred_element_type=f32) + carry_sc[:, 0:1]
    ranks = [jnp.sum(jnp.where(erow == idxs[k], prefix, 0.0), axis=0, keepdims=True) for k in range(TOP_K)]
    rank_ref[...] = jnp.concatenate(ranks, axis=0).astype(jnp.int32)

    total = carry_sc[...] + jnp.sum(jnp.where(taken, 1.0, 0.0), axis=1, keepdims=True)
    carry_sc[...] = total
    cnt_ref[...] = total


def _router(h, router_w, router_b):
    N, D = h.shape
    E = router_w.shape[1]
    tm = _pick(N, 512)
    wt = router_w.T
    w_hi = wt.astype(bf16)
    w_split = jnp.concatenate([w_hi, (wt - w_hi.astype(f32)).astype(bf16)], axis=0)
    kspec = pl.BlockSpec((TOP_K, tm), lambda i: (0, i))
    idx, gate, rank, cnt = pl.pallas_call(
        functools.partial(_router_kernel, n_exp=E),
        grid=(N // tm,),
        in_specs=[pl.BlockSpec((tm, D), lambda i: (i, 0)),
                  pl.BlockSpec((2 * E, D), lambda i: (0, 0)),
                  pl.BlockSpec((E, 1), lambda i: (0, 0))],
        out_specs=[kspec, kspec, kspec, pl.BlockSpec((E, 128), lambda i: (0, 0))],
        out_shape=[jax.ShapeDtypeStruct((TOP_K, N), jnp.int32),
                   jax.ShapeDtypeStruct((TOP_K, N), f32),
                   jax.ShapeDtypeStruct((TOP_K, N), jnp.int32),
                   jax.ShapeDtypeStruct((E, 128), f32)],
        scratch_shapes=[pltpu.VMEM((E, 128), f32)],
        compiler_params=_cparams("arbitrary"),
        name="moe_router",
    )(h, w_split, router_b.reshape(E, 1))
    return idx, gate, rank, cnt[:, 0].astype(jnp.int32)


def _routing_tables(idx, rank, counts, n_tok, tile):
    B = EXPERT_ROWS
    E = counts.shape[0]
    A = TOP_K * n_tok
    assert A % B == 0
    n_blocks = A // B + E
    stride = A + B
    blocks = (counts + B - 1) // B
    bend = jnp.cumsum(blocks)
    n_used = bend[-1:].astype(jnp.int32)
    block_ids = jnp.arange(n_blocks, dtype=jnp.int32)
    block_expert = jnp.minimum(jnp.sum(bend[None, :] <= block_ids[:, None], axis=1), E - 1).astype(jnp.int32)

    first_row = ((bend - blocks) * B).astype(jnp.int32)
    onehot = idx[:, :, None] == jnp.arange(E, dtype=jnp.int32)
    dest = rank + jnp.sum(jnp.where(onehot, first_row, 0), axis=-1)
    dest = dest.reshape(TOP_K, n_tok // tile, tile).transpose(1, 0, 2)

    token = jnp.arange(n_tok, dtype=jnp.int32)[None, :]
    real = (idx * stride + token).reshape(-1)
    pad_count = blocks * B - counts
    j = jnp.arange(B, dtype=jnp.int32)[None, :]
    e = jnp.arange(E, dtype=jnp.int32)[:, None]
    pad = jnp.where(j < pad_count[:, None], e * stride + A + j, E * stride)
    keys = lax.sort(jnp.concatenate([real, pad.reshape(-1).astype(jnp.int32)]))
    low = keys % stride
    is_real = jnp.logical_and(low < A, keys < E * stride)
    src = jnp.where(is_real, low, 0).reshape(n_blocks, 1, B)
    return src, dest, block_expert, n_used


def _unpack_pairs(u):
    lo = lax.bitcast_convert_type(u << 16, f32)
    hi = lax.bitcast_convert_type(u & jnp.uint32(0xFFFF0000), f32)
    return lo, hi


def _silu(x):
    return x * jax.nn.sigmoid(x)


X_CHUNK = 256


def _expert_kernel(be_ref, nu_ref, lay_ref, tab_hbm, hp_hbm, wu_ref, wd_ref, ys_ref,
                   tab_sm, xbuf, wu_sc, wd_sc, sem_t, gsem):
    del lay_ref
    i = pl.program_id(0)
    n_used = nu_ref[0]
    last_block = pl.num_programs(0) - 1
    B, half = xbuf.shape[1], xbuf.shape[2]
    n_chunks = half // X_CHUNK
    issue_chunks = max(1, n_chunks // 2)
    rows_per_chunk = B // issue_chunks

    def table_copy(j):
        return pltpu.make_async_copy(tab_hbm.at[jnp.minimum(j, last_block)], tab_sm.at[j % 3], sem_t.at[j % 3])

    def gather_copy(j, r):
        return pltpu.make_async_copy(hp_hbm.at[pl.ds(tab_sm[j % 3, 0, r], 1)],
                                     xbuf.at[j % 2, pl.ds(r, 1)], gsem.at[j % 2])

    def wait_rows(j):
        pltpu.make_async_copy(hp_hbm.at[pl.ds(0, B)], xbuf.at[j % 2], gsem.at[j % 2]).wait()

    @pl.when(i == 0)
    def _():
        table_copy(0).start()
        table_copy(0).wait()
        table_copy(1).start()
        for r in range(B):
            gather_copy(0, r).start()

    @pl.when(i < n_used)
    def _():
        table_copy(i + 1).wait()

        @pl.when(i + 1 < n_used)
        def _():
            table_copy(i + 2).start()

        @pl.when(jnp.logical_or(i == 0, be_ref[i] != be_ref[jnp.maximum(i - 1, 0)]))
        def _():
            wu_sc[...] = wu_ref[...].astype(bf16)
            wd_sc[...] = wd_ref[...].astype(bf16)

        wait_rows(i)
        hu = None
        for c in range(n_chunks):
            cols = slice(c * X_CHUNK, (c + 1) * X_CHUNK)
            lo, hi = _unpack_pairs(xbuf[i % 2, :, cols])
            part = (jnp.dot(lo.astype(bf16), wu_sc[cols, :], preferred_element_type=f32)
                    + jnp.dot(hi.astype(bf16), wu_sc[half + c * X_CHUNK:half + (c + 1) * X_CHUNK, :],
                              preferred_element_type=f32))
            hu = part if hu is None else hu + part
            if c < issue_chunks:
                for r in range(c * rows_per_chunk, (c + 1) * rows_per_chunk):
                    gather_copy(i + 1, r).start()
        hid = wd_sc.shape[0]
        act = (_silu(hu[:, :hid]) * hu[:, hid:]).astype(bf16)
        ys_ref[...] = _pack_pairs(jnp.dot(act, wd_sc[...], preferred_element_type=f32))

        @pl.when(i == n_used - 1)
        def _():
            wait_rows(i + 1)


def _experts(hp, src, block_expert, n_used, w_up, w_down, layer):
    N, D2 = hp.shape
    _, E, D, H2 = w_up.shape
    H = H2 // 2
    B = EXPERT_ROWS
    n_blocks = src.shape[0]
    live = lambda i, be, nu, lay: (jnp.minimum(i, nu[0] - 1), 0)
    return pl.pallas_call(
        _expert_kernel,
        grid_spec=pltpu.PrefetchScalarGridSpec(
            num_scalar_prefetch=3,
            grid=(n_blocks,),
            in_specs=[pl.BlockSpec(memory_space=pl.ANY),
                      pl.BlockSpec(memory_space=pl.ANY),
                      pl.BlockSpec((None, None, D, H2), lambda i, be, nu, lay: (lay[0], be[i], 0, 0)),
                      pl.BlockSpec((None, None, H, D), lambda i, be, nu, lay: (lay[0], be[i], 0, 0))],
            out_specs=pl.BlockSpec((B, D2), live),
            scratch_shapes=[pltpu.SMEM((3, 1, B), jnp.int32),
                            pltpu.VMEM((2, B, D2), jnp.uint32),
                            pltpu.VMEM((D, H2), bf16), pltpu.VMEM((H, D), bf16),
                            pltpu.SemaphoreType.DMA((3,)), pltpu.SemaphoreType.DMA((2,))],
        ),
        out_shape=jax.ShapeDtypeStruct((n_blocks * B, D2), jnp.uint32),
        compiler_params=_cparams("arbitrary"),
        name="moe_experts",
    )(block_expert, n_used, jnp.full((1,), layer, jnp.int32), src, hp, w_up, w_down)


COMBINE_TOKENS = 128


def _combine_kernel(dest_hbm, ys_hbm, gate_ref, h_ref, hb_ref, sg_ref, sv_ref, sd_ref, g_ref, b_ref,
                    ho_ref, hbo_ref, hpo_ref, tab_sm, ybuf, z_sc, gate_sc, sem_t, gsem, fence, *, alpha):
    i = pl.program_id(0)
    last_tile = pl.num_programs(0) - 1
    tm, d = h_ref.shape
    half = d // 2
    lanes = gate_sc.shape[2]
    n_chunks = half // lanes
    copies_per_chunk = TOP_K * tm // n_chunks

    def table_copy(j):
        return pltpu.make_async_copy(dest_hbm.at[jnp.minimum(j, last_tile)], tab_sm.at[j % 3], sem_t.at[j % 3])

    def row_copy(j, q, slot=None):
        k, t = q // tm, q % tm
        slot = j % 2 if slot is None else slot
        return pltpu.make_async_copy(ys_hbm.at[pl.ds(tab_sm[j % 3, k, t], 1)],
                                     ybuf.at[slot, k, pl.ds(t, 1)], gsem.at[slot])

    def wait_rows(j):
        for k in range(TOP_K):
            pltpu.make_async_copy(ys_hbm.at[pl.ds(0, tm)], ybuf.at[j % 2, k], gsem.at[j % 2]).wait()

    @pl.when(i == 0)
    def _():
        table_copy(0).start()
        table_copy(0).wait()
        table_copy(1).start()
        for q in range(TOP_K * tm):
            row_copy(0, q).start()

    table_copy(i + 1).wait()

    @pl.when(i < last_tile)
    def _():
        table_copy(i + 2).start()

    hb = hb_ref[...]
    act = (_silu(jnp.dot(hb, sg_ref[...], preferred_element_type=f32))
           * jnp.dot(hb, sv_ref[...], preferred_element_type=f32)).astype(bf16)
    z_sc[...] = alpha * h_ref[...] + jnp.dot(act, sd_ref[...], preferred_element_type=f32)
    gate = gate_ref[...]
    for k in range(TOP_K):
        gate_sc[k] = jnp.broadcast_to(gate[:, k:k + 1], (tm, lanes))
    wait_rows(i)

    def accumulate(slot):
        for c in range(n_chunks):
            lo_sl = slice(c * lanes, (c + 1) * lanes)
            hi_sl = slice(half + c * lanes, half + (c + 1) * lanes)
            z_lo, z_hi = z_sc[:, lo_sl], z_sc[:, hi_sl]
            for k in range(TOP_K):
                lo, hi = _unpack_pairs(ybuf[slot, k, :, lo_sl])
                g = gate_sc[k]
                z_lo = z_lo + lo * g
                z_hi = z_hi + hi * g
            z_sc[:, lo_sl] = z_lo
            z_sc[:, hi_sl] = z_hi
            for q in range(c * copies_per_chunk, (c + 1) * copies_per_chunk):
                row_copy(i + 1, q, 1 - slot).start()
            pl.semaphore_signal(fence, 1)
            pl.semaphore_wait(fence, 1)

    for parity in (0, 1):
        pl.when(i % 2 == parity)(functools.partial(accumulate, parity))

    _store_hidden(_layer_norm_rows(z_sc[...], g_ref[...], b_ref[...]), ho_ref, hbo_ref, hpo_ref)

    @pl.when(i == last_tile)
    def _():
        wait_rows(i + 1)


def _combine(ys, dest, gate_t, h, hb, s_gate, s_val, s_down, ln_g, ln_b, alpha):
    N, D = h.shape
    H = s_gate.shape[1]
    nt, _, tm = dest.shape
    specs, shapes = _hidden_out(N, D, tm)
    row = pl.BlockSpec((tm, D), lambda i: (i, 0))
    full = lambda a, b: pl.BlockSpec((a, b), lambda i: (0, 0))
    hbm = pl.BlockSpec(memory_space=pl.ANY)
    return pl.pallas_call(
        functools.partial(_combine_kernel, alpha=alpha),
        grid=(nt,),
        in_specs=[hbm, hbm, pl.BlockSpec((tm, TOP_K), lambda i: (i, 0)),
                  row, row, full(D, H), full(D, H), full(H, D), full(1, D), full(1, D)],
        out_specs=specs,
        out_shape=shapes,
        scratch_shapes=[pltpu.SMEM((3, TOP_K, tm), jnp.int32),
                        pltpu.VMEM((2, TOP_K, tm, D // 2), jnp.uint32),
                        pltpu.VMEM((tm, D), f32), pltpu.VMEM((TOP_K, tm, 128), f32),
                        pltpu.SemaphoreType.DMA((3,)), pltpu.SemaphoreType.DMA((2,)),
                        pltpu.SemaphoreType.REGULAR],
        compiler_params=_cparams("arbitrary"),
        name="moe_combine",
    )(dest, ys, gate_t, h, hb, s_gate, s_val, s_down, ln_g.reshape(1, D), ln_b.reshape(1, D))


def _moe_layer(h, hb, hp, router_w, router_b, w_up, w_down, layer, s_gate, s_val, s_down,
               ln_g, ln_b, alpha):
    n_tok = h.shape[0]
    idx, gate, rank, counts = _router(h, router_w, router_b)
    src, dest, block_expert, n_used = _routing_tables(idx, rank, counts, n_tok, _pick(n_tok, COMBINE_TOKENS))
    ys = _experts(hp, src, block_expert, n_used, w_up, w_down, layer)
    return _combine(ys, dest, gate.T, h, hb, s_gate, s_val, s_down, ln_g, ln_b, alpha)


def kernel(x, lru_w_in, lru_b_in, lru_conv_w, lru_conv_b, lru_gx_w, lru_gx_b, lru_ga_w, lru_ga_b,
           lru_a_param, lru_w_out, attn_w_qkv, attn_sinks, attn_w_o, ln_g, ln_b,
           moe_router_w, moe_router_b, moe_w_up, moe_w_down, moe_shared_up, moe_shared_down):
    Bsz, T, D = x.shape
    depth = ln_g.shape[0]
    alpha = (2.0 * depth) ** 0.25
    W = lru_w_in.shape[2] // 2
    H = moe_w_down.shape[2]
    n_q = D // HEAD_DIM
    n_kv = (attn_w_qkv.shape[2] // HEAD_DIM - n_q) // 2
    N = Bsz * T
    assert Bsz == 1, "sequence mixers below assume one sequence"

    h = x.reshape(N, D)
    hb = h.astype(bf16)
    hp = None
    for layer in range(depth):
        j = layer // 2
        if layer % 2 == 0:
            xr, gb = _lru_in(hb, lru_w_in, lru_b_in.reshape(-1, 1, 2 * W), j)
            y = _lru_core(xr, gb, lru_conv_w[j], lru_conv_b[j], lru_gx_w[j], lru_gx_b[j].reshape(-1),
                          lru_ga_w[j], lru_ga_b[j].reshape(-1), lru_a_param[j])
            z = _matmul_residual(y, lru_w_out, j, h, alpha)
        else:
            qkv = _matmul(hb, attn_w_qkv, j, bf16)
            o = _attention(qkv, attn_sinks[j], n_q, n_kv)
            z = _matmul_residual(o, attn_w_o, j, h, alpha)
        h, hb, hp = _layer_norm(z, ln_g[layer, 0], ln_b[layer, 0])
        h, hb, hp = _moe_layer(
            h, hb, hp, moe_router_w[layer], moe_router_b[layer], moe_w_up, moe_w_down, layer,
            moe_shared_up[layer, :, :H].astype(bf16), moe_shared_up[layer, :, H:].astype(bf16),
            moe_shared_down[layer].astype(bf16), ln_g[layer, 1], ln_b[layer, 1], alpha)
    return h.reshape(Bsz, T, D)
```

```python
import functools

import jax
import jax.numpy as jnp
from jax import lax
from jax.experimental import pallas as pl
from jax.experimental.pallas import tpu as pltpu

HEAD_DIM = 128
WINDOW = 128
CONV_WIDTH = 4
LRU_C = 8.0
TOP_K = 8
N_GROUPS = 8
TOPK_GROUPS = 4
ROUTED_SCALE = 2.5
LN_EPS = 1e-5

V7X_VMEM_BYTES = 64 * 1024 * 1024
VMEM_LIMIT = V7X_VMEM_BYTES - 8 * 1024 * 1024
EXPERT_ROWS = 256

f32 = jnp.float32
bf16 = jnp.bfloat16


def _cparams(*sem):
    return pltpu.CompilerParams(dimension_semantics=sem, vmem_limit_bytes=VMEM_LIMIT)


def _pick(n, pref):
    t = min(pref, n)
    while n % t:
        t //= 2
    return t


def _mm_kernel(x_ref, w_ref, o_ref):
    w = w_ref[...].astype(bf16)
    o_ref[...] = jnp.dot(x_ref[...], w, preferred_element_type=f32).astype(o_ref.dtype)


def _matmul(x, w_stack, li, out_dtype):
    M, K = x.shape
    N = w_stack.shape[2]
    tm, tn = _pick(M, 1024), _pick(N, 512)
    return pl.pallas_call(
        _mm_kernel,
        grid=(M // tm, N // tn),
        in_specs=[pl.BlockSpec((tm, K), lambda i, j: (i, 0)),
                  pl.BlockSpec((None, K, tn), lambda i, j: (li, 0, j))],
        out_specs=pl.BlockSpec((tm, tn), lambda i, j: (i, j)),
        out_shape=jax.ShapeDtypeStruct((M, N), out_dtype),
        compiler_params=_cparams("parallel", "arbitrary"),
        name="matmul",
    )(x, w_stack)


def _mm_res_kernel(x_ref, w_ref, r_ref, o_ref, *, alpha):
    w = w_ref[...].astype(bf16)
    o_ref[...] = alpha * r_ref[...] + jnp.dot(x_ref[...], w, preferred_element_type=f32)


def _matmul_residual(x, w_stack, li, res, alpha):
    M, K = x.shape
    N = w_stack.shape[2]
    tm, tn = _pick(M, 1024), _pick(N, 512)
    return pl.pallas_call(
        functools.partial(_mm_res_kernel, alpha=alpha),
        grid=(M // tm, N // tn),
        in_specs=[pl.BlockSpec((tm, K), lambda i, j: (i, 0)),
                  pl.BlockSpec((None, K, tn), lambda i, j: (li, 0, j)),
                  pl.BlockSpec((tm, tn), lambda i, j: (i, j))],
        out_specs=pl.BlockSpec((tm, tn), lambda i, j: (i, j)),
        out_shape=jax.ShapeDtypeStruct((M, N), f32),
        compiler_params=_cparams("parallel", "arbitrary"),
        name="matmul_residual",
    )(x, w_stack, res)


def _gelu_tanh(x):
    return 0.5 * x * (1.0 + jnp.tanh(0.7978845608028654 * (x + 0.044715 * (x * x * x))))


def _lru_in_kernel(x_ref, wx_ref, wy_ref, bx_ref, by_ref, xr_ref, gb_ref):
    x = x_ref[...]
    xr_ref[...] = jnp.dot(x, wx_ref[...].astype(bf16), preferred_element_type=f32) + bx_ref[...]
    gb_ref[...] = _gelu_tanh(jnp.dot(x, wy_ref[...].astype(bf16), preferred_element_type=f32) + by_ref[...])


def _lru_in(x, w_stack, b_stack, li):
    M, K = x.shape
    W = w_stack.shape[2] // 2
    tm, tn = _pick(M, 1024), _pick(W, 256)
    gate0 = W // tn
    wx_spec = pl.BlockSpec((None, K, tn), lambda i, j: (li, 0, j))
    wy_spec = pl.BlockSpec((None, K, tn), lambda i, j: (li, 0, gate0 + j))
    bx_spec = pl.BlockSpec((None, 1, tn), lambda i, j: (li, 0, j))
    by_spec = pl.BlockSpec((None, 1, tn), lambda i, j: (li, 0, gate0 + j))
    ospec = pl.BlockSpec((tm, tn), lambda i, j: (i, j))
    return pl.pallas_call(
        _lru_in_kernel,
        grid=(M // tm, W // tn),
        in_specs=[pl.BlockSpec((tm, K), lambda i, j: (i, 0)), wx_spec, wy_spec, bx_spec, by_spec],
        out_specs=[ospec, ospec],
        out_shape=[jax.ShapeDtypeStruct((M, W), f32)] * 2,
        compiler_params=_cparams("parallel", "arbitrary"),
        name="lru_in",
    )(x, w_stack, w_stack, b_stack, b_stack)


def _layer_norm_rows(z, g, b):
    mu = jnp.mean(z, axis=-1, keepdims=True)
    zc = z - mu
    var = jnp.mean(zc * zc, axis=-1, keepdims=True)
    return zc * lax.rsqrt(var + LN_EPS) * g + b


def _pack_pairs(x):
    half = x.shape[1] // 2
    bits = lax.bitcast_convert_type(x.astype(bf16).astype(f32), jnp.uint32)
    return (bits[:, :half] >> 16) | bits[:, half:]


def _store_hidden(h, h_ref, hb_ref, hp_ref):
    h_ref[...] = h
    hb_ref[...] = h.astype(bf16)
    hp_ref[...] = _pack_pairs(h)


def _ln_kernel(z_ref, g_ref, b_ref, h_ref, hb_ref, hp_ref):
    _store_hidden(_layer_norm_rows(z_ref[...], g_ref[...], b_ref[...]), h_ref, hb_ref, hp_ref)


def _hidden_out(M, D, tm):
    specs = [pl.BlockSpec((tm, D), lambda i: (i, 0)),
             pl.BlockSpec((tm, D), lambda i: (i, 0)),
             pl.BlockSpec((tm, D // 2), lambda i: (i, 0))]
    shapes = [jax.ShapeDtypeStruct((M, D), f32),
              jax.ShapeDtypeStruct((M, D), bf16),
              jax.ShapeDtypeStruct((M, D // 2), jnp.uint32)]
    return specs, shapes


def _layer_norm(z, g, b):
    M, D = z.shape
    tm = _pick(M, 256)
    specs, shapes = _hidden_out(M, D, tm)
    vspec = pl.BlockSpec((1, D), lambda i: (0, 0))
    return pl.pallas_call(
        _ln_kernel,
        grid=(M // tm,),
        in_specs=[pl.BlockSpec((tm, D), lambda i: (i, 0)), vspec, vspec],
        out_specs=specs,
        out_shape=shapes,
        compiler_params=_cparams("parallel"),
        name="layer_norm",
    )(z, g.reshape(1, D), b.reshape(1, D))


def _sigmoid(x):
    return 0.5 + 0.5 * jnp.tanh(0.5 * x)


def _softplus(x):
    return jnp.maximum(x, 0.0) + jnp.log1p(jnp.exp(-jnp.abs(x)))


def _lru_core_kernel(xr_ref, gb_ref, cw_ref, cb_ref, gxw_ref, gxb_ref, gaw_ref, gab_ref, ap_ref,
                     y_ref, tail_sc, h_sc, a_sc, b_sc, hs_sc, *, heads, blk):
    tb = xr_ref.shape[0]

    @pl.when(pl.program_id(1) == 0)
    def _():
        tail_sc[...] = jnp.zeros_like(tail_sc)
        h_sc[...] = jnp.zeros_like(h_sc)

    xr = xr_ref[...]
    ext = jnp.concatenate([tail_sc[...], xr], axis=0)
    tail_sc[...] = xr[tb - 8:, :]
    cw = cw_ref[...]
    xc = cb_ref[...]
    for j in range(CONV_WIDTH):
        off = 8 - (CONV_WIDTH - 1) + j
        xc = xc + cw[j:j + 1, :] * ext[off:off + tb, :]
    xcb = xc.astype(bf16)

    for hd in range(heads):
        sl = slice(hd * blk, (hd + 1) * blk)
        xh = xcb[:, sl]
        gate_x = _sigmoid(jnp.dot(xh, gxw_ref[hd], preferred_element_type=f32) + gxb_ref[:, sl])
        gate_a = _sigmoid(jnp.dot(xh, gaw_ref[hd], preferred_element_type=f32) + gab_ref[:, sl])
        a = jnp.exp((-LRU_C * _softplus(-ap_ref[:, sl])) * gate_a)
        a_sc[:, sl] = a
        b_sc[:, sl] = xc[:, sl] * gate_x * jnp.sqrt(1.0 - a * a)

    def body(i, h):
        base = pl.multiple_of(i * 8, 8)
        a8 = a_sc[pl.ds(base, 8), :]
        b8 = b_sc[pl.ds(base, 8), :]
        rows = []
        for r in range(8):
            h = a8[r:r + 1, :] * h + b8[r:r + 1, :]
            rows.append(h)
        hs_sc[pl.ds(base, 8), :] = jnp.concatenate(rows, axis=0)
        return h

    h_sc[0:1, :] = lax.fori_loop(0, tb // 8, body, h_sc[0:1, :])
    y_ref[...] = (hs_sc[...] * gb_ref[...]).astype(y_ref.dtype)


def _lru_core(xr, gb, conv_w, conv_b, gx_w, gx_b, ga_w, ga_b, a_param):
    T, W = xr.shape
    nh, blk = gx_w.shape[0], gx_w.shape[1]
    wb = _pick(W, 1024)
    heads = wb // blk
    tb = _pick(T, 256)
    xspec = pl.BlockSpec((tb, wb), lambda c, t: (t, c))
    vspec = pl.BlockSpec((1, wb), lambda c, t: (0, c))
    gspec = pl.BlockSpec((heads, blk, blk), lambda c, t: (c, 0, 0))
    return pl.pallas_call(
        functools.partial(_lru_core_kernel, heads=heads, blk=blk),
        grid=(W // wb, T // tb),
        in_specs=[xspec, xspec, pl.BlockSpec((CONV_WIDTH, wb), lambda c, t: (0, c)), vspec,
                  gspec, vspec, gspec, vspec, vspec],
        out_specs=xspec,
        out_shape=jax.ShapeDtypeStruct((T, W), bf16),
        scratch_shapes=[pltpu.VMEM((8, wb), f32), pltpu.VMEM((8, wb), f32),
                        pltpu.VMEM((tb, wb), f32), pltpu.VMEM((tb, wb), f32), pltpu.VMEM((tb, wb), f32)],
        compiler_params=_cparams("parallel", "arbitrary"),
        name="lru_core",
    )(xr, gb, conv_w, conv_b.reshape(1, W), gx_w.astype(bf16), gx_b.reshape(1, W),
      ga_w.astype(bf16), ga_b.reshape(1, W), a_param.reshape(1, W))


def _attn_kernel(q_ref, kp_ref, kc_ref, vp_ref, vc_ref, bias_ref, sink_ref, o_ref, *, group, n_kv):
    blk = q_ref.shape[0]
    hd = HEAD_DIM
    col = lax.broadcasted_iota(jnp.int32, (group * blk, 2 * blk), 1)
    no_prev = jnp.logical_and(pl.program_id(0) == 0, col < blk)
    for kv in range(n_kv):
        q = jnp.concatenate([q_ref[:, (kv * group + g) * hd:(kv * group + g + 1) * hd] for g in range(group)],
                            axis=0)
        ksl = slice(kv * hd, (kv + 1) * hd)
        k = jnp.concatenate([kp_ref[:, ksl], kc_ref[:, ksl]], axis=0)
        v = jnp.concatenate([vp_ref[:, ksl], vc_ref[:, ksl]], axis=0)
        s = lax.dot_general(q, k, (((1,), (1,)), ((), ())), preferred_element_type=f32) * (hd ** -0.5)
        logits = jnp.where(no_prev, -jnp.inf, s + bias_ref[kv])
        sink = sink_ref[kv]
        m = jnp.maximum(jnp.max(logits, axis=-1, keepdims=True), sink)
        p = jnp.exp(logits - m)
        probs = p / (jnp.sum(p, axis=-1, keepdims=True) + jnp.exp(sink - m))
        o = jnp.dot(probs.astype(v.dtype), v, preferred_element_type=f32)
        for g in range(group):
            o_ref[:, (kv * group + g) * hd:(kv * group + g + 1) * hd] = (
                o[g * blk:(g + 1) * blk, :].astype(o_ref.dtype))


def _attention(qkv, sinks, n_q, n_kv):
    T = qkv.shape[0]
    group = n_q // n_kv
    blk = WINDOW
    nb = T // blk
    i = jnp.arange(blk)[:, None]
    j = jnp.arange(2 * blk)[None, :]
    dist = blk + i - j
    in_band = (dist >= 0) & (dist < WINDOW)
    slopes = jnp.exp2(-8.0 * jnp.arange(1, n_q + 1, dtype=f32) / n_q)
    bias = jnp.where(in_band[None], -slopes[:, None, None] * dist.astype(f32)[None], -jnp.inf)
    bias = bias.reshape(n_kv, group * blk, 2 * blk)
    sink = jnp.repeat(sinks.astype(f32).reshape(n_kv, group), blk, axis=1).reshape(n_kv, group * blk, 1)

    assert group * n_kv == n_q
    prev = lambda n: jnp.maximum(n - 1, 0)
    kvw = n_kv * HEAD_DIM
    return pl.pallas_call(
        functools.partial(_attn_kernel, group=group, n_kv=n_kv),
        grid=(nb,),
        in_specs=[pl.BlockSpec((blk, n_q * HEAD_DIM), lambda n: (n, 0)),
                  pl.BlockSpec((blk, kvw), lambda n: (prev(n), group)),
                  pl.BlockSpec((blk, kvw), lambda n: (n, group)),
                  pl.BlockSpec((blk, kvw), lambda n: (prev(n), group + 1)),
                  pl.BlockSpec((blk, kvw), lambda n: (n, group + 1)),
                  pl.BlockSpec((n_kv, group * blk, 2 * blk), lambda n: (0, 0, 0)),
                  pl.BlockSpec((n_kv, group * blk, 1), lambda n: (0, 0, 0))],
        out_specs=pl.BlockSpec((blk, n_q * HEAD_DIM), lambda n: (n, 0)),
        out_shape=jax.ShapeDtypeStruct((T, n_q * HEAD_DIM), bf16),
        compiler_params=_cparams("arbitrary"),
        name="swa_attention",
    )(qkv, qkv, qkv, qkv, qkv, bias, sink)


def _first_index_of_max(x, rows):
    m = jnp.max(x, axis=0, keepdims=True)
    idx = jnp.min(jnp.where(x == m, rows, x.shape[0]), axis=0, keepdims=True)
    return m, idx


def _router_kernel(h_ref, rwt_ref, rb_ref, idx_ref, gate_ref, rank_ref, cnt_ref, carry_sc, *, n_exp):
    tm = h_ref.shape[0]
    gsz = n_exp // N_GROUPS

    @pl.when(pl.program_id(0) == 0)
    def _():
        carry_sc[...] = jnp.zeros_like(carry_sc)

    h = h_ref[...]
    h_hi = h.astype(bf16)
    h_lo = (h - h_hi.astype(f32)).astype(bf16)
    nt = (((1,), (1,)), ((), ()))
    p = lax.dot_general(rwt_ref[...], h_hi, nt, preferred_element_type=f32)
    q = lax.dot_general(rwt_ref[:n_exp, :], h_lo, nt, preferred_element_type=f32)
    logits = p[:n_exp, :] + p[n_exp:, :] + q
    scores = jax.nn.sigmoid(logits)
    sel = scores + rb_ref[...]

    grow = lax.broadcasted_iota(jnp.int32, (gsz, tm), 0)
    gscores = []
    for g in range(N_GROUPS):
        sg = sel[g * gsz:(g + 1) * gsz, :]
        m1, i1 = _first_index_of_max(sg, grow)
        m2 = jnp.max(jnp.where(grow == i1, -jnp.inf, sg), axis=0, keepdims=True)
        gscores.append(m1 + m2)
    gs = jnp.concatenate(gscores, axis=0)
    g_iota = lax.broadcasted_iota(jnp.int32, (N_GROUPS, tm), 0)
    gpick = jnp.zeros((N_GROUPS, tm), jnp.bool_)
    for _ in range(TOPK_GROUPS):
        _, gi = _first_index_of_max(jnp.where(gpick, -jnp.inf, gs), g_iota)
        gpick = jnp.logical_or(gpick, g_iota == gi)

    erow = lax.broadcasted_iota(jnp.int32, (n_exp, tm), 0)
    emask = jnp.concatenate(
        [jnp.broadcast_to(gpick[g:g + 1, :], (gsz, tm)) for g in range(N_GROUPS)], axis=0)
    cand = jnp.where(emask, sel, -jnp.inf)
    idxs, gates = [], []
    taken = jnp.zeros((n_exp, tm), jnp.bool_)
    for _ in range(TOP_K):
        _, ei = _first_index_of_max(jnp.where(taken, -jnp.inf, cand), erow)
        hit = erow == ei
        taken = jnp.logical_or(taken, hit)
        idxs.append(ei)
        gates.append(jnp.sum(jnp.where(hit, scores, 0.0), axis=0, keepdims=True))
    gate = jnp.concatenate(gates, axis=0)
    idx_ref[...] = jnp.concatenate(idxs, axis=0)
    gate_ref[...] = gate / jnp.sum(gate, axis=0, keepdims=True) * ROUTED_SCALE

    onehot = jnp.where(taken, 1.0, 0.0).astype(bf16)
    r_i = lax.broadcasted_iota(jnp.int32, (tm, tm), 0)
    c_i = lax.broadcasted_iota(jnp.int32, (tm, tm), 1)
    before = jnp.where(r_i < c_i, 1.0, 0.0).astype(bf16)
    prefix = jnp.dot(onehot, before, preferred_element_type=f32) + carry_sc[:, 0:1]
    ranks = [jnp.sum(jnp.where(erow == idxs[k], prefix, 0.0), axis=0, keepdims=True) for k in range(TOP_K)]
    rank_ref[...] = jnp.concatenate(ranks, axis=0).astype(jnp.int32)

    total = carry_sc[...] + jnp.sum(jnp.where(taken, 1.0, 0.0), axis=1, keepdims=True)
    carry_sc[...] = total
    cnt_ref[...] = total


def _router(h, router_w, router_b):
    N, D = h.shape
    E = router_w.shape[1]
    tm = _pick(N, 512)
    wt = router_w.T
    w_hi = wt.astype(bf16)
    w_split = jnp.concatenate([w_hi, (wt - w_hi.astype(f32)).astype(bf16)], axis=0)
    kspec = pl.BlockSpec((TOP_K, tm), lambda i: (0, i))
    idx, gate, rank, cnt = pl.pallas_call(
        functools.partial(_router_kernel, n_exp=E),
        grid=(N // tm,),
        in_specs=[pl.BlockSpec((tm, D), lambda i: (i, 0)),
                  pl.BlockSpec((2 * E, D), lambda i: (0, 0)),
                  pl.BlockSpec((E, 1), lambda i: (0, 0))],
        out_specs=[kspec, kspec, kspec, pl.BlockSpec((E, 128), lambda i: (0, 0))],
        out_shape=[jax.ShapeDtypeStruct((TOP_K, N), jnp.int32),
                   jax.ShapeDtypeStruct((TOP_K, N), f32),
                   jax.ShapeDtypeStruct((TOP_K, N), jnp.int32),
                   jax.ShapeDtypeStruct((E, 128), f32)],
        scratch_shapes=[pltpu.VMEM((E, 128), f32)],
        compiler_params=_cparams("arbitrary"),
        name="moe_router",
    )(h, w_split, router_b.reshape(E, 1))
    return idx, gate, rank, cnt[:, 0].astype(jnp.int32)


def _routing_tables(idx, rank, counts, n_tok, tile):
    B = EXPERT_ROWS
    E = counts.shape[0]
    A = TOP_K * n_tok
    assert A % B == 0
    n_blocks = A // B + E
    stride = A + B
    blocks = (counts + B - 1) // B
    bend = jnp.cumsum(blocks)
    n_used = bend[-1:].astype(jnp.int32)
    block_ids = jnp.arange(n_blocks, dtype=jnp.int32)
    block_expert = jnp.minimum(jnp.sum(bend[None, :] <= block_ids[:, None], axis=1), E - 1).astype(jnp.int32)

    first_row = ((bend - blocks) * B).astype(jnp.int32)
    onehot = idx[:, :, None] == jnp.arange(E, dtype=jnp.int32)
    dest = rank + jnp.sum(jnp.where(onehot, first_row, 0), axis=-1)
    dest = dest.reshape(TOP_K, n_tok // tile, tile).transpose(1, 0, 2)

    token = jnp.arange(n_tok, dtype=jnp.int32)[None, :]
    real = (idx * stride + token).reshape(-1)
    pad_count = blocks * B - counts
    j = jnp.arange(B, dtype=jnp.int32)[None, :]
    e = jnp.arange(E, dtype=jnp.int32)[:, None]
    pad = jnp.where(j < pad_count[:, None], e * stride + A + j, E * stride)
    keys = lax.sort(jnp.concatenate([real, pad.reshape(-1).astype(jnp.int32)]))
    low = keys % stride
    is_real = jnp.logical_and(low < A, keys < E * stride)
    src = jnp.where(is_real, low, 0).reshape(n_blocks, 1, B)

    prev = jnp.concatenate([jnp.full((1,), -1, jnp.int32), block_expert[:-1]])
    is_first = jnp.logical_and(block_ids < n_used[0], block_expert != prev)
    ordinal = jnp.cumsum(is_first.astype(jnp.int32)) - 1
    by_ordinal = jnp.full((E + 1,), -1, jnp.int32).at[jnp.where(is_first, ordinal, E)].set(
        jnp.where(is_first, block_expert, -1))
    next_expert = by_ordinal[jnp.minimum(ordinal + 1, E)]
    return src, dest, block_expert, n_used, ordinal.astype(jnp.int32), next_expert


def _unpack_pairs(u):
    lo = lax.bitcast_convert_type(u << 16, f32)
    hi = lax.bitcast_convert_type(u & jnp.uint32(0xFFFF0000), f32)
    return lo, hi


def _silu(x):
    return x * jax.nn.sigmoid(x)


X_CHUNK = 256


def _expert_kernel(be_ref, nu_ref, lay_ref, ord_ref, nxt_ref, tab_hbm, hp_hbm, wu_hbm, wd_hbm, ys_ref,
                   tab_sm, xbuf, wu_buf, wd_buf, wu_sc, wd_sc, sem_t, gsem, wsem):
    lay = lay_ref[0]
    i = pl.program_id(0)
    n_used = nu_ref[0]
    last_block = pl.num_programs(0) - 1
    B, half = xbuf.shape[1], xbuf.shape[2]
    n_chunks = half // X_CHUNK
    rows_per_chunk = B // n_chunks

    def table_copy(j):
        return pltpu.make_async_copy(tab_hbm.at[jnp.minimum(j, last_block)], tab_sm.at[j % 3], sem_t.at[j % 3])

    def gather_copy(j, r):
        return pltpu.make_async_copy(hp_hbm.at[pl.ds(tab_sm[j % 3, 0, r], 1)],
                                     xbuf.at[j % 2, pl.ds(r, 1)], gsem.at[j % 2])

    def wait_rows(j):
        pltpu.make_async_copy(hp_hbm.at[pl.ds(0, B)], xbuf.at[j % 2], gsem.at[j % 2]).wait()

    def weight_copies(e, slot):
        return (pltpu.make_async_copy(wu_hbm.at[lay, e], wu_buf.at[slot], wsem.at[slot]),
                pltpu.make_async_copy(wd_hbm.at[lay, e], wd_buf.at[slot], wsem.at[slot]))

    @pl.when(i == 0)
    def _():
        for cp in weight_copies(be_ref[0], 0):
            cp.start()
        table_copy(0).start()
        table_copy(0).wait()
        table_copy(1).start()
        for r in range(B):
            gather_copy(0, r).start()

    @pl.when(i < n_used)
    def _():
        table_copy(i + 1).wait()

        @pl.when(i + 1 < n_used)
        def _():
            table_copy(i + 2).start()

        @pl.when(jnp.logical_or(i == 0, be_ref[i] != be_ref[jnp.maximum(i - 1, 0)]))
        def _():
            slot = ord_ref[i] % 2
            for cp in weight_copies(be_ref[i], slot):
                cp.wait()
            wu_sc[...] = wu_buf[slot].astype(bf16)
            wd_sc[...] = wd_buf[slot].astype(bf16)

            @pl.when(nxt_ref[i] >= 0)
            def _():
                for cp in weight_copies(nxt_ref[i], 1 - slot):
                    cp.start()

        wait_rows(i)
        hu = None
        for c in range(n_chunks):
            cols = slice(c * X_CHUNK, (c + 1) * X_CHUNK)
            lo, hi = _unpack_pairs(xbuf[i % 2, :, cols])
            part = (jnp.dot(lo.astype(bf16), wu_sc[cols, :], preferred_element_type=f32)
                    + jnp.dot(hi.astype(bf16), wu_sc[half + c * X_CHUNK:half + (c + 1) * X_CHUNK, :],
                              preferred_element_type=f32))
            hu = part if hu is None else hu + part
            for r in range(c * rows_per_chunk, (c + 1) * rows_per_chunk):
                gather_copy(i + 1, r).start()
        hid = wd_sc.shape[0]
        act = (_silu(hu[:, :hid]) * hu[:, hid:]).astype(bf16)
        ys_ref[...] = _pack_pairs(jnp.dot(act, wd_sc[...], preferred_element_type=f32))

        @pl.when(i == n_used - 1)
        def _():
            wait_rows(i + 1)


def _experts(hp, src, block_expert, n_used, ordinal, next_expert, w_up, w_down, layer):
    N, D2 = hp.shape
    _, E, D, H2 = w_up.shape
    H = H2 // 2
    B = EXPERT_ROWS
    n_blocks = src.shape[0]
    live = lambda i, be, nu, lay, od, nx: (jnp.minimum(i, nu[0] - 1), 0)
    hbm = pl.BlockSpec(memory_space=pl.ANY)
    return pl.pallas_call(
        _expert_kernel,
        grid_spec=pltpu.PrefetchScalarGridSpec(
            num_scalar_prefetch=5,
            grid=(n_blocks,),
            in_specs=[hbm, hbm, hbm, hbm],
            out_specs=pl.BlockSpec((B, D2), live),
            scratch_shapes=[pltpu.SMEM((3, 1, B), jnp.int32),
                            pltpu.VMEM((2, B, D2), jnp.uint32),
                            pltpu.VMEM((2, D, H2), f32), pltpu.VMEM((2, H, D), f32),
                            pltpu.VMEM((D, H2), bf16), pltpu.VMEM((H, D), bf16),
                            pltpu.SemaphoreType.DMA((3,)), pltpu.SemaphoreType.DMA((2,)),
                            pltpu.SemaphoreType.DMA((2,))],
        ),
        out_shape=jax.ShapeDtypeStruct((n_blocks * B, D2), jnp.uint32),
        compiler_params=_cparams("arbitrary"),
        name="moe_experts",
    )(block_expert, n_used, jnp.full((1,), layer, jnp.int32), ordinal, next_expert, src, hp, w_up, w_down)


COMBINE_TOKENS = 128


def _combine_kernel(dest_hbm, ys_hbm, gate_ref, h_ref, hb_ref, sg_ref, sv_ref, sd_ref, g_ref, b_ref,
                    ho_ref, hbo_ref, hpo_ref, tab_sm, ybuf, z_sc, gate_sc, sem_t, gsem, fence, *, alpha):
    i = pl.program_id(0)
    last_tile = pl.num_programs(0) - 1
    tm, d = h_ref.shape
    half = d // 2
    lanes = gate_sc.shape[2]
    n_chunks = half // lanes
    copies_per_chunk = TOP_K * tm // n_chunks

    def table_copy(j):
        return pltpu.make_async_copy(dest_hbm.at[jnp.minimum(j, last_tile)], tab_sm.at[j % 3], sem_t.at[j % 3])

    def row_copy(j, q, slot=None):
        k, t = q // tm, q % tm
        slot = j % 2 if slot is None else slot
        return pltpu.make_async_copy(ys_hbm.at[pl.ds(tab_sm[j % 3, k, t], 1)],
                                     ybuf.at[slot, k, pl.ds(t, 1)], gsem.at[slot])

    def wait_rows(j):
        for k in range(TOP_K):
            pltpu.make_async_copy(ys_hbm.at[pl.ds(0, tm)], ybuf.at[j % 2, k], gsem.at[j % 2]).wait()

    @pl.when(i == 0)
    def _():
        table_copy(0).start()
        table_copy(0).wait()
        table_copy(1).start()
        for q in range(TOP_K * tm):
            row_copy(0, q).start()

    table_copy(i + 1).wait()

    @pl.when(i < last_tile)
    def _():
        table_copy(i + 2).start()

    hb = hb_ref[...]
    act = (_silu(jnp.dot(hb, sg_ref[...], preferred_element_type=f32))
           * jnp.dot(hb, sv_ref[...], preferred_element_type=f32)).astype(bf16)
    z_sc[...] = alpha * h_ref[...] + jnp.dot(act, sd_ref[...], preferred_element_type=f32)
    gate = gate_ref[...]
    for k in range(TOP_K):
        gate_sc[k] = jnp.broadcast_to(gate[:, k:k + 1], (tm, lanes))
    wait_rows(i)

    def accumulate(slot):
        for c in range(n_chunks):
            lo_sl = slice(c * lanes, (c + 1) * lanes)
            hi_sl = slice(half + c * lanes, half + (c + 1) * lanes)
            z_lo, z_hi = z_sc[:, lo_sl], z_sc[:, hi_sl]
            for k in range(TOP_K):
                lo, hi = _unpack_pairs(ybuf[slot, k, :, lo_sl])
                g = gate_sc[k]
                z_lo = z_lo + lo * g
                z_hi = z_hi + hi * g
            z_sc[:, lo_sl] = z_lo
            z_sc[:, hi_sl] = z_hi
            for q in range(c * copies_per_chunk, (c + 1) * copies_per_chunk):
                row_copy(i + 1, q, 1 - slot).start()
            pl.semaphore_signal(fence, 1)
            pl.semaphore_wait(fence, 1)

    for parity in (0, 1):
        pl.when(i % 2 == parity)(functools.partial(accumulate, parity))

    _store_hidden(_layer_norm_rows(z_sc[...], g_ref[...], b_ref[...]), ho_ref, hbo_ref, hpo_ref)

    @pl.when(i == last_tile)
    def _():
        wait_rows(i + 1)


def _combine(ys, dest, gate_t, h, hb, s_gate, s_val, s_down, ln_g, ln_b, alpha):
    N, D = h.shape
    H = s_gate.shape[1]
    nt, _, tm = dest.shape
    specs, shapes = _hidden_out(N, D, tm)
    row = pl.BlockSpec((tm, D), lambda i: (i, 0))
    full = lambda a, b: pl.BlockSpec((a, b), lambda i: (0, 0))
    hbm = pl.BlockSpec(memory_space=pl.ANY)
    return pl.pallas_call(
        functools.partial(_combine_kernel, alpha=alpha),
        grid=(nt,),
        in_specs=[hbm, hbm, pl.BlockSpec((tm, TOP_K), lambda i: (i, 0)),
                  row, row, full(D, H), full(D, H), full(H, D), full(1, D), full(1, D)],
        out_specs=specs,
        out_shape=shapes,
        scratch_shapes=[pltpu.SMEM((3, TOP_K, tm), jnp.int32),
                        pltpu.VMEM((2, TOP_K, tm, D // 2), jnp.uint32),
                        pltpu.VMEM((tm, D), f32), pltpu.VMEM((TOP_K, tm, 128), f32),
                        pltpu.SemaphoreType.DMA((3,)), pltpu.SemaphoreType.DMA((2,)),
                        pltpu.SemaphoreType.REGULAR],
        compiler_params=_cparams("arbitrary"),
        name="moe_combine",
    )(dest, ys, gate_t, h, hb, s_gate, s_val, s_down, ln_g.reshape(1, D), ln_b.reshape(1, D))


def _moe_layer(h, hb, hp, router_w, router_b, w_up, w_down, layer, s_gate, s_val, s_down,
               ln_g, ln_b, alpha):
    n_tok = h.shape[0]
    idx, gate, rank, counts = _router(h, router_w, router_b)
    src, dest, block_expert, n_used, ordinal, next_expert = _routing_tables(
        idx, rank, counts, n_tok, _pick(n_tok, COMBINE_TOKENS))
    ys = _experts(hp, src, block_expert, n_used, ordinal, next_expert, w_up, w_down, layer)
    return _combine(ys, dest, gate.T, h, hb, s_gate, s_val, s_down, ln_g, ln_b, alpha)


def kernel(x, lru_w_in, lru_b_in, lru_conv_w, lru_conv_b, lru_gx_w, lru_gx_b, lru_ga_w, lru_ga_b,
           lru_a_param, lru_w_out, attn_w_qkv, attn_sinks, attn_w_o, ln_g, ln_b,
           moe_router_w, moe_router_b, moe_w_up, moe_w_down, moe_shared_up, moe_shared_down):
    Bsz, T, D = x.shape
    depth = ln_g.shape[0]
    alpha = (2.0 * depth) ** 0.25
    W = lru_w_in.shape[2] // 2
    H = moe_w_down.shape[2]
    n_q = D // HEAD_DIM
    n_kv = (attn_w_qkv.shape[2] // HEAD_DIM - n_q) // 2
    N = Bsz * T
    assert Bsz == 1, "sequence mixers below assume one sequence"

    h = x.reshape(N, D)
    hb = h.astype(bf16)
    hp = None
    for layer in range(depth):
        j = layer // 2
        if layer % 2 == 0:
            xr, gb = _lru_in(hb, lru_w_in, lru_b_in.reshape(-1, 1, 2 * W), j)
            y = _lru_core(xr, gb, lru_conv_w[j], lru_conv_b[j], lru_gx_w[j], lru_gx_b[j].reshape(-1),
                          lru_ga_w[j], lru_ga_b[j].reshape(-1), lru_a_param[j])
            z = _matmul_residual(y, lru_w_out, j, h, alpha)
        else:
            qkv = _matmul(hb, attn_w_qkv, j, bf16)
            o = _attention(qkv, attn_sinks[j], n_q, n_kv)
            z = _matmul_residual(o, attn_w_o, j, h, alpha)
        h, hb, hp = _layer_norm(z, ln_g[layer, 0], ln_b[layer, 0])
        h, hb, hp = _moe_layer(
            h, hb, hp, moe_router_w[layer], moe_router_b[layer], moe_w_up, moe_w_down, layer,
            moe_shared_up[layer, :, :H].astype(bf16), moe_shared_up[layer, :, H:].astype(bf16),
            moe_shared_down[layer].astype(bf16), ln_g[layer, 1], ln_b[layer, 1], alpha)
    return h.reshape(Bsz, T, D)
```

```python
import functools

import jax
import jax.numpy as jnp
from jax import lax
from jax.experimental import pallas as pl
from jax.experimental.pallas import tpu as pltpu

HEAD_DIM = 128
WINDOW = 128
CONV_WIDTH = 4
LRU_C = 8.0
TOP_K = 8
N_GROUPS = 8
TOPK_GROUPS = 4
ROUTED_SCALE = 2.5
LN_EPS = 1e-5

V7X_VMEM_BYTES = 64 * 1024 * 1024
VMEM_LIMIT = V7X_VMEM_BYTES - 8 * 1024 * 1024
EXPERT_ROWS = 256

f32 = jnp.float32
bf16 = jnp.bfloat16


def _cparams(*sem):
    return pltpu.CompilerParams(dimension_semantics=sem, vmem_limit_bytes=VMEM_LIMIT)


def _pick(n, pref):
    t = min(pref, n)
    while n % t:
        t //= 2
    return t


def _mm_kernel(x_ref, w_ref, o_ref):
    w = w_ref[...].astype(bf16)
    o_ref[...] = jnp.dot(x_ref[...], w, preferred_element_type=f32).astype(o_ref.dtype)


def _matmul(x, w_stack, li, out_dtype):
    M, K = x.shape
    N = w_stack.shape[2]
    tm, tn = _pick(M, 1024), _pick(N, 512)
    return pl.pallas_call(
        _mm_kernel,
        grid=(M // tm, N // tn),
        in_specs=[pl.BlockSpec((tm, K), lambda i, j: (i, 0)),
                  pl.BlockSpec((None, K, tn), lambda i, j: (li, 0, j))],
        out_specs=pl.BlockSpec((tm, tn), lambda i, j: (i, j)),
        out_shape=jax.ShapeDtypeStruct((M, N), out_dtype),
        compiler_params=_cparams("parallel", "arbitrary"),
        name="matmul",
    )(x, w_stack)


def _mm_res_kernel(x_ref, w_ref, r_ref, o_ref, *, alpha):
    w = w_ref[...].astype(bf16)
    o_ref[...] = alpha * r_ref[...] + jnp.dot(x_ref[...], w, preferred_element_type=f32)


def _matmul_residual(x, w_stack, li, res, alpha):
    M, K = x.shape
    N = w_stack.shape[2]
    tm, tn = _pick(M, 1024), _pick(N, 512)
    return pl.pallas_call(
        functools.partial(_mm_res_kernel, alpha=alpha),
        grid=(M // tm, N // tn),
        in_specs=[pl.BlockSpec((tm, K), lambda i, j: (i, 0)),
                  pl.BlockSpec((None, K, tn), lambda i, j: (li, 0, j)),
                  pl.BlockSpec((tm, tn), lambda i, j: (i, j))],
        out_specs=pl.BlockSpec((tm, tn), lambda i, j: (i, j)),
        out_shape=jax.ShapeDtypeStruct((M, N), f32),
        compiler_params=_cparams("parallel", "arbitrary"),
        name="matmul_residual",
    )(x, w_stack, res)


def _gelu_tanh(x):
    return 0.5 * x * (1.0 + jnp.tanh(0.7978845608028654 * (x + 0.044715 * (x * x * x))))


def _lru_in_kernel(x_ref, wx_ref, wy_ref, bx_ref, by_ref, xr_ref, gb_ref):
    x = x_ref[...]
    xr_ref[...] = jnp.dot(x, wx_ref[...].astype(bf16), preferred_element_type=f32) + bx_ref[...]
    gb_ref[...] = _gelu_tanh(jnp.dot(x, wy_ref[...].astype(bf16), preferred_element_type=f32) + by_ref[...])


def _lru_in(x, w_stack, b_stack, li):
    M, K = x.shape
    W = w_stack.shape[2] // 2
    tm, tn = _pick(M, 1024), _pick(W, 256)
    gate0 = W // tn
    wx_spec = pl.BlockSpec((None, K, tn), lambda i, j: (li, 0, j))
    wy_spec = pl.BlockSpec((None, K, tn), lambda i, j: (li, 0, gate0 + j))
    bx_spec = pl.BlockSpec((None, 1, tn), lambda i, j: (li, 0, j))
    by_spec = pl.BlockSpec((None, 1, tn), lambda i, j: (li, 0, gate0 + j))
    ospec = pl.BlockSpec((tm, tn), lambda i, j: (i, j))
    return pl.pallas_call(
        _lru_in_kernel,
        grid=(M // tm, W // tn),
        in_specs=[pl.BlockSpec((tm, K), lambda i, j: (i, 0)), wx_spec, wy_spec, bx_spec, by_spec],
        out_specs=[ospec, ospec],
        out_shape=[jax.ShapeDtypeStruct((M, W), f32)] * 2,
        compiler_params=_cparams("parallel", "arbitrary"),
        name="lru_in",
    )(x, w_stack, w_stack, b_stack, b_stack)


def _layer_norm_rows(z, g, b):
    mu = jnp.mean(z, axis=-1, keepdims=True)
    zc = z - mu
    var = jnp.mean(zc * zc, axis=-1, keepdims=True)
    return zc * lax.rsqrt(var + LN_EPS) * g + b


def _pack_pairs(x):
    half = x.shape[1] // 2
    bits = lax.bitcast_convert_type(x.astype(bf16).astype(f32), jnp.uint32)
    return (bits[:, :half] >> 16) | bits[:, half:]


def _store_hidden(h, h_ref, hb_ref, hp_ref):
    h_ref[...] = h
    hb_ref[...] = h.astype(bf16)
    hp_ref[...] = _pack_pairs(h)


def _ln_kernel(z_ref, g_ref, b_ref, h_ref, hb_ref, hp_ref):
    _store_hidden(_layer_norm_rows(z_ref[...], g_ref[...], b_ref[...]), h_ref, hb_ref, hp_ref)


def _hidden_out(M, D, tm):
    specs = [pl.BlockSpec((tm, D), lambda i: (i, 0)),
             pl.BlockSpec((tm, D), lambda i: (i, 0)),
             pl.BlockSpec((tm, D // 2), lambda i: (i, 0))]
    shapes = [jax.ShapeDtypeStruct((M, D), f32),
              jax.ShapeDtypeStruct((M, D), bf16),
              jax.ShapeDtypeStruct((M, D // 2), jnp.uint32)]
    return specs, shapes


def _layer_norm(z, g, b):
    M, D = z.shape
    tm = _pick(M, 256)
    specs, shapes = _hidden_out(M, D, tm)
    vspec = pl.BlockSpec((1, D), lambda i: (0, 0))
    return pl.pallas_call(
        _ln_kernel,
        grid=(M // tm,),
        in_specs=[pl.BlockSpec((tm, D), lambda i: (i, 0)), vspec, vspec],
        out_specs=specs,
        out_shape=shapes,
        compiler_params=_cparams("parallel"),
        name="layer_norm",
    )(z, g.reshape(1, D), b.reshape(1, D))


def _sigmoid(x):
    return 0.5 + 0.5 * jnp.tanh(0.5 * x)


def _softplus(x):
    return jnp.maximum(x, 0.0) + jnp.log1p(jnp.exp(-jnp.abs(x)))


def _lru_core_kernel(xr_ref, gb_ref, cw_ref, cb_ref, gxw_ref, gxb_ref, gaw_ref, gab_ref, ap_ref,
                     y_ref, tail_sc, h_sc, a_sc, b_sc, hs_sc, *, heads, blk):
    tb = xr_ref.shape[0]

    @pl.when(pl.program_id(1) == 0)
    def _():
        tail_sc[...] = jnp.zeros_like(tail_sc)
        h_sc[...] = jnp.zeros_like(h_sc)

    xr = xr_ref[...]
    ext = jnp.concatenate([tail_sc[...], xr], axis=0)
    tail_sc[...] = xr[tb - 8:, :]
    cw = cw_ref[...]
    xc = cb_ref[...]
    for j in range(CONV_WIDTH):
        off = 8 - (CONV_WIDTH - 1) + j
        xc = xc + cw[j:j + 1, :] * ext[off:off + tb, :]
    xcb = xc.astype(bf16)

    for hd in range(heads):
        sl = slice(hd * blk, (hd + 1) * blk)
        xh = xcb[:, sl]
        gate_x = _sigmoid(jnp.dot(xh, gxw_ref[hd], preferred_element_type=f32) + gxb_ref[:, sl])
        gate_a = _sigmoid(jnp.dot(xh, gaw_ref[hd], preferred_element_type=f32) + gab_ref[:, sl])
        a = jnp.exp((-LRU_C * _softplus(-ap_ref[:, sl])) * gate_a)
        a_sc[:, sl] = a
        b_sc[:, sl] = xc[:, sl] * gate_x * jnp.sqrt(1.0 - a * a)

    def body(i, h):
        base = pl.multiple_of(i * 8, 8)
        a8 = a_sc[pl.ds(base, 8), :]
        b8 = b_sc[pl.ds(base, 8), :]
        rows = []
        for r in range(8):
            h = a8[r:r + 1, :] * h + b8[r:r + 1, :]
            rows.append(h)
        hs_sc[pl.ds(base, 8), :] = jnp.concatenate(rows, axis=0)
        return h

    h_sc[0:1, :] = lax.fori_loop(0, tb // 8, body, h_sc[0:1, :])
    y_ref[...] = (hs_sc[...] * gb_ref[...]).astype(y_ref.dtype)


def _lru_core(xr, gb, conv_w, conv_b, gx_w, gx_b, ga_w, ga_b, a_param):
    T, W = xr.shape
    nh, blk = gx_w.shape[0], gx_w.shape[1]
    wb = _pick(W, 1024)
    heads = wb // blk
    tb = _pick(T, 256)
    xspec = pl.BlockSpec((tb, wb), lambda c, t: (t, c))
    vspec = pl.BlockSpec((1, wb), lambda c, t: (0, c))
    gspec = pl.BlockSpec((heads, blk, blk), lambda c, t: (c, 0, 0))
    return pl.pallas_call(
        functools.partial(_lru_core_kernel, heads=heads, blk=blk),
        grid=(W // wb, T // tb),
        in_specs=[xspec, xspec, pl.BlockSpec((CONV_WIDTH, wb), lambda c, t: (0, c)), vspec,
                  gspec, vspec, gspec, vspec, vspec],
        out_specs=xspec,
        out_shape=jax.ShapeDtypeStruct((T, W), bf16),
        scratch_shapes=[pltpu.VMEM((8, wb), f32), pltpu.VMEM((8, wb), f32),
                        pltpu.VMEM((tb, wb), f32), pltpu.VMEM((tb, wb), f32), pltpu.VMEM((tb, wb), f32)],
        compiler_params=_cparams("parallel", "arbitrary"),
        name="lru_core",
    )(xr, gb, conv_w, conv_b.reshape(1, W), gx_w.astype(bf16), gx_b.reshape(1, W),
      ga_w.astype(bf16), ga_b.reshape(1, W), a_param.reshape(1, W))


def _attn_kernel(q_ref, kp_ref, kc_ref, vp_ref, vc_ref, bias_ref, sink_ref, o_ref, *, group, n_kv):
    blk = q_ref.shape[0]
    hd = HEAD_DIM
    col = lax.broadcasted_iota(jnp.int32, (group * blk, 2 * blk), 1)
    no_prev = jnp.logical_and(pl.program_id(0) == 0, col < blk)
    for kv in range(n_kv):
        q = jnp.concatenate([q_ref[:, (kv * group + g) * hd:(kv * group + g + 1) * hd] for g in range(group)],
                            axis=0)
        ksl = slice(kv * hd, (kv + 1) * hd)
        k = jnp.concatenate([kp_ref[:, ksl], kc_ref[:, ksl]], axis=0)
        v = jnp.concatenate([vp_ref[:, ksl], vc_ref[:, ksl]], axis=0)
        s = lax.dot_general(q, k, (((1,), (1,)), ((), ())), preferred_element_type=f32) * (hd ** -0.5)
        logits = jnp.where(no_prev, -jnp.inf, s + bias_ref[kv])
        sink = sink_ref[kv]
        m = jnp.maximum(jnp.max(logits, axis=-1, keepdims=True), sink)
        p = jnp.exp(logits - m)
        probs = p / (jnp.sum(p, axis=-1, keepdims=True) + jnp.exp(sink - m))
        o = jnp.dot(probs.astype(v.dtype), v, preferred_element_type=f32)
        for g in range(group):
            o_ref[:, (kv * group + g) * hd:(kv * group + g + 1) * hd] = (
                o[g * blk:(g + 1) * blk, :].astype(o_ref.dtype))


def _attention(qkv, sinks, n_q, n_kv):
    T = qkv.shape[0]
    group = n_q // n_kv
    blk = WINDOW
    nb = T // blk
    i = jnp.arange(blk)[:, None]
    j = jnp.arange(2 * blk)[None, :]
    dist = blk + i - j
    in_band = (dist >= 0) & (dist < WINDOW)
    slopes = jnp.exp2(-8.0 * jnp.arange(1, n_q + 1, dtype=f32) / n_q)
    bias = jnp.where(in_band[None], -slopes[:, None, None] * dist.astype(f32)[None], -jnp.inf)
    bias = bias.reshape(n_kv, group * blk, 2 * blk)
    sink = jnp.repeat(sinks.astype(f32).reshape(n_kv, group), blk, axis=1).reshape(n_kv, group * blk, 1)

    assert group * n_kv == n_q
    prev = lambda n: jnp.maximum(n - 1, 0)
    kvw = n_kv * HEAD_DIM
    return pl.pallas_call(
        functools.partial(_attn_kernel, group=group, n_kv=n_kv),
        grid=(nb,),
        in_specs=[pl.BlockSpec((blk, n_q * HEAD_DIM), lambda n: (n, 0)),
                  pl.BlockSpec((blk, kvw), lambda n: (prev(n), group)),
                  pl.BlockSpec((blk, kvw), lambda n: (n, group)),
                  pl.BlockSpec((blk, kvw), lambda n: (prev(n), group + 1)),
                  pl.BlockSpec((blk, kvw), lambda n: (n, group + 1)),
                  pl.BlockSpec((n_kv, group * blk, 2 * blk), lambda n: (0, 0, 0)),
                  pl.BlockSpec((n_kv, group * blk, 1), lambda n: (0, 0, 0))],
        out_specs=pl.BlockSpec((blk, n_q * HEAD_DIM), lambda n: (n, 0)),
        out_shape=jax.ShapeDtypeStruct((T, n_q * HEAD_DIM), bf16),
        compiler_params=_cparams("arbitrary"),
        name="swa_attention",
    )(qkv, qkv, qkv, qkv, qkv, bias, sink)


def _first_index_of_max(x, rows):
    m = jnp.max(x, axis=0, keepdims=True)
    idx = jnp.min(jnp.where(x == m, rows, x.shape[0]), axis=0, keepdims=True)
    return m, idx


def _router_kernel(h_ref, rwt_ref, rb_ref, idx_ref, gate_ref, rank_ref, cnt_ref, carry_sc, *, n_exp):
    tm = h_ref.shape[0]
    gsz = n_exp // N_GROUPS

    @pl.when(pl.program_id(0) == 0)
    def _():
        carry_sc[...] = jnp.zeros_like(carry_sc)

    h = h_ref[...]
    h_hi = h.astype(bf16)
    h_lo = (h - h_hi.astype(f32)).astype(bf16)
    nt = (((1,), (1,)), ((), ()))
    p = lax.dot_general(rwt_ref[...], h_hi, nt, preferred_element_type=f32)
    q = lax.dot_general(rwt_ref[:n_exp, :], h_lo, nt, preferred_element_type=f32)
    logits = p[:n_exp, :] + p[n_exp:, :] + q
    scores = jax.nn.sigmoid(logits)
    sel = scores + rb_ref[...]

    grow = lax.broadcasted_iota(jnp.int32, (gsz, tm), 0)
    gscores = []
    for g in range(N_GROUPS):
        sg = sel[g * gsz:(g + 1) * gsz, :]
        m1, i1 = _first_index_of_max(sg, grow)
        m2 = jnp.max(jnp.where(grow == i1, -jnp.inf, sg), axis=0, keepdims=True)
        gscores.append(m1 + m2)
    gs = jnp.concatenate(gscores, axis=0)
    g_iota = lax.broadcasted_iota(jnp.int32, (N_GROUPS, tm), 0)
    gpick = jnp.zeros((N_GROUPS, tm), jnp.bool_)
    for _ in range(TOPK_GROUPS):
        _, gi = _first_index_of_max(jnp.where(gpick, -jnp.inf, gs), g_iota)
        gpick = jnp.logical_or(gpick, g_iota == gi)

    erow = lax.broadcasted_iota(jnp.int32, (n_exp, tm), 0)
    emask = jnp.concatenate(
        [jnp.broadcast_to(gpick[g:g + 1, :], (gsz, tm)) for g in range(N_GROUPS)], axis=0)
    cand = jnp.where(emask, sel, -jnp.inf)
    idxs, gates = [], []
    taken = jnp.zeros((n_exp, tm), jnp.bool_)
    for _ in range(TOP_K):
        _, ei = _first_index_of_max(jnp.where(taken, -jnp.inf, cand), erow)
        hit = erow == ei
        taken = jnp.logical_or(taken, hit)
        idxs.append(ei)
        gates.append(jnp.sum(jnp.where(hit, scores, 0.0), axis=0, keepdims=True))
    gate = jnp.concatenate(gates, axis=0)
    idx_ref[...] = jnp.concatenate(idxs, axis=0)
    gate_ref[...] = gate / jnp.sum(gate, axis=0, keepdims=True) * ROUTED_SCALE

    onehot = jnp.where(taken, 1.0, 0.0).astype(bf16)
    r_i = lax.broadcasted_iota(jnp.int32, (tm, tm), 0)
    c_i = lax.broadcasted_iota(jnp.int32, (tm, tm), 1)
    before = jnp.where(r_i < c_i, 1.0, 0.0).astype(bf16)
    prefix = jnp.dot(onehot, before, preferred_element_type=f32) + carry_sc[:, 0:1]
    ranks = [jnp.sum(jnp.where(erow == idxs[k], prefix, 0.0), axis=0, keepdims=True) for k in range(TOP_K)]
    rank_ref[...] = jnp.concatenate(ranks, axis=0).astype(jnp.int32)

    total = carry_sc[...] + jnp.sum(jnp.where(taken, 1.0, 0.0), axis=1, keepdims=True)
    carry_sc[...] = total
    cnt_ref[...] = total


def _router(h, router_w, router_b):
    N, D = h.shape
    E = router_w.shape[1]
    tm = _pick(N, 512)
    wt = router_w.T
    w_hi = wt.astype(bf16)
    w_split = jnp.concatenate([w_hi, (wt - w_hi.astype(f32)).astype(bf16)], axis=0)
    kspec = pl.BlockSpec((TOP_K, tm), lambda i: (0, i))
    idx, gate, rank, cnt = pl.pallas_call(
        functools.partial(_router_kernel, n_exp=E),
        grid=(N // tm,),
        in_specs=[pl.BlockSpec((tm, D), lambda i: (i, 0)),
                  pl.BlockSpec((2 * E, D), lambda i: (0, 0)),
                  pl.BlockSpec((E, 1), lambda i: (0, 0))],
        out_specs=[kspec, kspec, kspec, pl.BlockSpec((E, 128), lambda i: (0, 0))],
        out_shape=[jax.ShapeDtypeStruct((TOP_K, N), jnp.int32),
                   jax.ShapeDtypeStruct((TOP_K, N), f32),
                   jax.ShapeDtypeStruct((TOP_K, N), jnp.int32),
                   jax.ShapeDtypeStruct((E, 128), f32)],
        scratch_shapes=[pltpu.VMEM((E, 128), f32)],
        compiler_params=_cparams("arbitrary"),
        name="moe_router",
    )(h, w_split, router_b.reshape(E, 1))
    return idx, gate, rank, cnt[:, 0].astype(jnp.int32)


def _routing_tables(idx, rank, counts, n_tok, tile):
    B = EXPERT_ROWS
    E = counts.shape[0]
    A = TOP_K * n_tok
    assert A % B == 0
    n_blocks = A // B + E
    stride = A + B
    blocks = (counts + B - 1) // B
    bend = jnp.cumsum(blocks)
    n_used = bend[-1:].astype(jnp.int32)
    block_ids = jnp.arange(n_blocks, dtype=jnp.int32)
    block_expert = jnp.minimum(jnp.sum(bend[None, :] <= block_ids[:, None], axis=1), E - 1).astype(jnp.int32)

    first_row = ((bend - blocks) * B).astype(jnp.int32)
    onehot = idx[:, :, None] == jnp.arange(E, dtype=jnp.int32)
    dest = rank + jnp.sum(jnp.where(onehot, first_row, 0), axis=-1)
    dest = dest.reshape(TOP_K, n_tok // tile, tile).transpose(1, 0, 2)

    token = jnp.arange(n_tok, dtype=jnp.int32)[None, :]
    real = (idx * stride + token).reshape(-1)
    pad_count = blocks * B - counts
    j = jnp.arange(B, dtype=jnp.int32)[None, :]
    e = jnp.arange(E, dtype=jnp.int32)[:, None]
    pad = jnp.where(j < pad_count[:, None], e * stride + A + j, E * stride)
    keys = lax.sort(jnp.concatenate([real, pad.reshape(-1).astype(jnp.int32)]))
    low = keys % stride
    is_real = jnp.logical_and(low < A, keys < E * stride)
    src = jnp.where(is_real, low, 0).reshape(n_blocks, 1, B)
    return src, dest, block_expert, n_used


def _unpack_pairs(u):
    lo = lax.bitcast_convert_type(u << 16, f32)
    hi = lax.bitcast_convert_type(u & jnp.uint32(0xFFFF0000), f32)
    return lo, hi


def _silu(x):
    return x * jax.nn.sigmoid(x)


X_CHUNK = 256


def _expert_kernel(be_ref, nu_ref, lay_ref, tab_hbm, hp_hbm, wu_ref, wd_ref, ys_ref,
                   tab_sm, xbuf, wu_sc, wd_sc, sem_t, gsem):
    del lay_ref
    i = pl.program_id(0)
    n_used = nu_ref[0]
    last_block = pl.num_programs(0) - 1
    B, half = xbuf.shape[1], xbuf.shape[2]
    n_chunks = half // X_CHUNK
    rows_per_chunk = B // n_chunks

    def table_copy(j):
        return pltpu.make_async_copy(tab_hbm.at[jnp.minimum(j, last_block)], tab_sm.at[j % 3], sem_t.at[j % 3])

    def gather_copy(j, r):
        return pltpu.make_async_copy(hp_hbm.at[pl.ds(tab_sm[j % 3, 0, r], 1)],
                                     xbuf.at[j % 2, pl.ds(r, 1)], gsem.at[j % 2])

    def wait_rows(j):
        pltpu.make_async_copy(hp_hbm.at[pl.ds(0, B)], xbuf.at[j % 2], gsem.at[j % 2]).wait()

    @pl.when(i == 0)
    def _():
        table_copy(0).start()
        table_copy(0).wait()
        table_copy(1).start()
        for r in range(B):
            gather_copy(0, r).start()

    @pl.when(i < n_used)
    def _():
        table_copy(i + 1).wait()

        @pl.when(i + 1 < n_used)
        def _():
            table_copy(i + 2).start()

        @pl.when(jnp.logical_or(i == 0, be_ref[i] != be_ref[jnp.maximum(i - 1, 0)]))
        def _():
            wu_sc[...] = wu_ref[...].astype(bf16)
            wd_sc[...] = wd_ref[...].astype(bf16)

        wait_rows(i)
        hu = None
        for c in range(n_chunks):
            cols = slice(c * X_CHUNK, (c + 1) * X_CHUNK)
            lo, hi = _unpack_pairs(xbuf[i % 2, :, cols])
            part = (jnp.dot(lo.astype(bf16), wu_sc[cols, :], preferred_element_type=f32)
                    + jnp.dot(hi.astype(bf16), wu_sc[half + c * X_CHUNK:half + (c + 1) * X_CHUNK, :],
                              preferred_element_type=f32))
            hu = part if hu is None else hu + part
            for r in range(c * rows_per_chunk, (c + 1) * rows_per_chunk):
                gather_copy(i + 1, r).start()
        hid = wd_sc.shape[0]
        act = (_silu(hu[:, :hid]) * hu[:, hid:]).astype(bf16)
        ys_ref[...] = _pack_pairs(jnp.dot(act, wd_sc[...], preferred_element_type=f32))

        @pl.when(i == n_used - 1)
        def _():
            wait_rows(i + 1)

    @pl.when(i >= n_used)
    def _():
        ys_ref[...] = jnp.zeros_like(ys_ref)


def _experts(hp, src, block_expert, n_used, w_up, w_down, layer):
    N, D2 = hp.shape
    _, E, D, H2 = w_up.shape
    H = H2 // 2
    B = EXPERT_ROWS
    n_blocks = src.shape[0]
    live = lambda i, be, nu, lay: (i, 0)
    return pl.pallas_call(
        _expert_kernel,
        grid_spec=pltpu.PrefetchScalarGridSpec(
            num_scalar_prefetch=3,
            grid=(n_blocks,),
            in_specs=[pl.BlockSpec(memory_space=pl.ANY),
                      pl.BlockSpec(memory_space=pl.ANY),
                      pl.BlockSpec((None, None, D, H2), lambda i, be, nu, lay: (lay[0], be[i], 0, 0)),
                      pl.BlockSpec((None, None, H, D), lambda i, be, nu, lay: (lay[0], be[i], 0, 0))],
            out_specs=pl.BlockSpec((B, D2), live),
            scratch_shapes=[pltpu.SMEM((3, 1, B), jnp.int32),
                            pltpu.VMEM((2, B, D2), jnp.uint32),
                            pltpu.VMEM((D, H2), bf16), pltpu.VMEM((H, D), bf16),
                            pltpu.SemaphoreType.DMA((3,)), pltpu.SemaphoreType.DMA((2,))],
        ),
        out_shape=jax.ShapeDtypeStruct((n_blocks * B, D2), jnp.uint32),
        compiler_params=_cparams("arbitrary"),
        name="moe_experts",
    )(block_expert, n_used, jnp.full((1,), layer, jnp.int32), src, hp, w_up, w_down)


COMBINE_TOKENS = 128


def _combine_kernel(dest_hbm, ys_hbm, gate_ref, h_ref, hb_ref, sg_ref, sv_ref, sd_ref, g_ref, b_ref,
                    ho_ref, hbo_ref, hpo_ref, tab_sm, ybuf, z_sc, gate_sc, sem_t, gsem, fence, *, alpha):
    i = pl.program_id(0)
    last_tile = pl.num_programs(0) - 1
    tm, d = h_ref.shape
    half = d // 2
    lanes = gate_sc.shape[2]
    n_chunks = half // lanes
    copies_per_chunk = TOP_K * tm // n_chunks

    def table_copy(j):
        return pltpu.make_async_copy(dest_hbm.at[jnp.minimum(j, last_tile)], tab_sm.at[j % 3], sem_t.at[j % 3])

    def row_copy(j, q, slot=None):
        k, t = q // tm, q % tm
        slot = j % 2 if slot is None else slot
        return pltpu.make_async_copy(ys_hbm.at[pl.ds(tab_sm[j % 3, k, t], 1)],
                                     ybuf.at[slot, k, pl.ds(t, 1)], gsem.at[slot])

    def wait_rows(j):
        for k in range(TOP_K):
            pltpu.make_async_copy(ys_hbm.at[pl.ds(0, tm)], ybuf.at[j % 2, k], gsem.at[j % 2]).wait()

    @pl.when(i == 0)
    def _():
        table_copy(0).start()
        table_copy(0).wait()
        table_copy(1).start()
        for q in range(TOP_K * tm):
            row_copy(0, q).start()

    table_copy(i + 1).wait()

    @pl.when(i < last_tile)
    def _():
        table_copy(i + 2).start()

    hb = hb_ref[...]
    act = (_silu(jnp.dot(hb, sg_ref[...], preferred_element_type=f32))
           * jnp.dot(hb, sv_ref[...], preferred_element_type=f32)).astype(bf16)
    z_sc[...] = alpha * h_ref[...] + jnp.dot(act, sd_ref[...], preferred_element_type=f32)
    gate = gate_ref[...]
    for k in range(TOP_K):
        gate_sc[k] = jnp.broadcast_to(gate[:, k:k + 1], (tm, lanes))
    wait_rows(i)

    def accumulate(slot):
        for c in range(n_chunks):
            lo_sl = slice(c * lanes, (c + 1) * lanes)
            hi_sl = slice(half + c * lanes, half + (c + 1) * lanes)
            z_lo, z_hi = z_sc[:, lo_sl], z_sc[:, hi_sl]
            for k in range(TOP_K):
                lo, hi = _unpack_pairs(ybuf[slot, k, :, lo_sl])
                g = gate_sc[k]
                z_lo = z_lo + lo * g
                z_hi = z_hi + hi * g
            z_sc[:, lo_sl] = z_lo
            z_sc[:, hi_sl] = z_hi
            for q in range(c * copies_per_chunk, (c + 1) * copies_per_chunk):
                row_copy(i + 1, q, 1 - slot).start()
            pl.semaphore_signal(fence, 1)
            pl.semaphore_wait(fence, 1)

    for parity in (0, 1):
        pl.when(i % 2 == parity)(functools.partial(accumulate, parity))

    _store_hidden(_layer_norm_rows(z_sc[...], g_ref[...], b_ref[...]), ho_ref, hbo_ref, hpo_ref)

    @pl.when(i == last_tile)
    def _():
        wait_rows(i + 1)


def _combine(ys, dest, gate_t, h, hb, s_gate, s_val, s_down, ln_g, ln_b, alpha):
    N, D = h.shape
    H = s_gate.shape[1]
    nt, _, tm = dest.shape
    specs, shapes = _hidden_out(N, D, tm)
    row = pl.BlockSpec((tm, D), lambda i: (i, 0))
    full = lambda a, b: pl.BlockSpec((a, b), lambda i: (0, 0))
    hbm = pl.BlockSpec(memory_space=pl.ANY)
    return pl.pallas_call(
        functools.partial(_combine_kernel, alpha=alpha),
        grid=(nt,),
        in_specs=[hbm, hbm, pl.BlockSpec((tm, TOP_K), lambda i: (i, 0)),
                  row, row, full(D, H), full(D, H), full(H, D), full(1, D), full(1, D)],
        out_specs=specs,
        out_shape=shapes,
        scratch_shapes=[pltpu.SMEM((3, TOP_K, tm), jnp.int32),
                        pltpu.VMEM((2, TOP_K, tm, D // 2), jnp.uint32),
                        pltpu.VMEM((tm, D), f32), pltpu.VMEM((TOP_K, tm, 128), f32),
                        pltpu.SemaphoreType.DMA((3,)), pltpu.SemaphoreType.DMA((2,)),
                        pltpu.SemaphoreType.REGULAR],
        compiler_params=_cparams("arbitrary"),
        name="moe_combine",
    )(dest, ys, gate_t, h, hb, s_gate, s_val, s_down, ln_g.reshape(1, D), ln_b.reshape(1, D))


def _moe_layer(h, hb, hp, router_w, router_b, w_up, w_down, layer, s_gate, s_val, s_down,
               ln_g, ln_b, alpha):
    n_tok = h.shape[0]
    idx, gate, rank, counts = _router(h, router_w, router_b)
    src, dest, block_expert, n_used = _routing_tables(idx, rank, counts, n_tok, _pick(n_tok, COMBINE_TOKENS))
    ys = _experts(hp, src, block_expert, n_used, w_up, w_down, layer)
    return _combine(ys, dest, gate.T, h, hb, s_gate, s_val, s_down, ln_g, ln_b, alpha)


def kernel(x, lru_w_in, lru_b_in, lru_conv_w, lru_conv_b, lru_gx_w, lru_gx_b, lru_ga_w, lru_ga_b,
           lru_a_param, lru_w_out, attn_w_qkv, attn_sinks, attn_w_o, ln_g, ln_b,
           moe_router_w, moe_router_b, moe_w_up, moe_w_down, moe_shared_up, moe_shared_down):
    Bsz, T, D = x.shape
    depth = ln_g.shape[0]
    alpha = (2.0 * depth) ** 0.25
    W = lru_w_in.shape[2] // 2
    H = moe_w_down.shape[2]
    n_q = D // HEAD_DIM
    n_kv = (attn_w_qkv.shape[2] // HEAD_DIM - n_q) // 2
    N = Bsz * T
    assert Bsz == 1, "sequence mixers below assume one sequence"

    h = x.reshape(N, D)
    hb = h.astype(bf16)
    hp = None
    for layer in range(depth):
        j = layer // 2
        if layer % 2 == 0:
            xr, gb = _lru_in(hb, lru_w_in, lru_b_in.reshape(-1, 1, 2 * W), j)
            y = _lru_core(xr, gb, lru_conv_w[j], lru_conv_b[j], lru_gx_w[j], lru_gx_b[j].reshape(-1),
                          lru_ga_w[j], lru_ga_b[j].reshape(-1), lru_a_param[j])
            z = _matmul_residual(y, lru_w_out, j, h, alpha)
        else:
            qkv = _matmul(hb, attn_w_qkv, j, bf16)
            o = _attention(qkv, attn_sinks[j], n_q, n_kv)
            z = _matmul_residual(o, attn_w_o, j, h, alpha)
        h, hb, hp = _layer_norm(z, ln_g[layer, 0], ln_b[layer, 0])
        h, hb, hp = _moe_layer(
            h, hb, hp, moe_router_w[layer], moe_router_b[layer], moe_w_up, moe_w_down, layer,
            moe_shared_up[layer, :, :H].astype(bf16), moe_shared_up[layer, :, H:].astype(bf16),
            moe_shared_down[layer].astype(bf16), ln_g[layer, 1], ln_b[layer, 1], alpha)
    return h.reshape(Bsz, T, D)
```
